```python
import math
import functools
import numpy as np
import jax
import jax.numpy as jnp
from jax import lax

D_MODEL = 1024
BATCH = 8
SEQ = 2048
DEPTH = 2
DEC_BATCH = 128
DEC_SEQ = 1
PAST_LEN = 16384
PAGE_SIZE = 128

D_MIX = 2 * D_MODEL
ML_WIDTH = 3 * D_MIX // 8
ML_HEADS = 6
ML_DH = ML_WIDTH // ML_HEADS
SSD_WIDTH = 3 * D_MIX // 8
SSD_HEADDIM = 64
SSD_HEADS = SSD_WIDTH // SSD_HEADDIM
SSD_STATE = 128
SSD_GROUPS = 2
SSD_CONV_DIM = SSD_WIDTH + 2 * SSD_GROUPS * SSD_STATE
CONV_W = 4
RET_WIDTH = D_MIX // 4
RET_HEADS = 4
RET_DH = RET_WIDTH // RET_HEADS
ROPE_BASE = 10000.0
CHUNK = 128
MEM_LEN = 256
CA_HEADS = 4
CA_DH = D_MODEL // CA_HEADS
D_FF = 11 * D_MODEL // 4
N_EXPERTS = 8
TOP_K = 2
D_EXPERT = D_FF // 2
N_DENSE = (DEPTH + 1) // 2
N_MOE = DEPTH // 2
D_IN_PROJ = 4 * ML_WIDTH + 2 * ML_HEADS + SSD_WIDTH + SSD_CONV_DIM + SSD_HEADS + 4 * RET_WIDTH
EPS = 1e-6

kernel_name = 'hybrid_mlstm_ssd_retention_decoder_step'


def _in_proj_splits():
    sizes = [ML_WIDTH] * 4 + [ML_HEADS, ML_HEADS, SSD_WIDTH, SSD_CONV_DIM, SSD_HEADS] + [RET_WIDTH] * 4
    return [int(s) for s in np.cumsum(sizes)[:-1]]


def rmsnorm(x, g):
    xf = x.astype(jnp.float32)
    y = xf * lax.rsqrt(jnp.mean(xf * xf, axis=-1, keepdims=True) + EPS)
    return (y * g.astype(jnp.float32)).astype(x.dtype)


def head_norm(x, g):
    y = x * lax.rsqrt(jnp.mean(x * x, axis=-1, keepdims=True) + EPS)
    return y.reshape(x.shape[0], x.shape[1], -1) * g.astype(jnp.float32)


def _chunk_len(length):
    return CHUNK if length % CHUNK == 0 else length


def _to_chunks(a, c):
    b, length = a.shape[0], a.shape[1]
    return jnp.moveaxis(a.reshape((b, length // c, c) + a.shape[2:]), 1, 0)


def _from_chunks(a):
    n, b, c = a.shape[0], a.shape[1], a.shape[2]
    return jnp.moveaxis(a, 0, 1).reshape((b, n * c) + a.shape[3:])


def _causal_mask(c):
    return jnp.tril(jnp.ones((c, c), dtype=bool))


def rotary(x, pos):
    half = x.shape[-1] // 2
    inv = ROPE_BASE ** (-jnp.arange(half, dtype=jnp.float32) / half)
    ang = pos[:, None] * inv[None, :]
    cos = jnp.cos(ang)[None, :, None, :]
    sin = jnp.sin(ang)[None, :, None, :]
    x1, x2 = x[..., :half], x[..., half:]
    return jnp.concatenate([x1 * cos - x2 * sin, x2 * cos + x1 * sin], axis=-1)


def mlstm_chunked(q, k, v, i_pre, f_pre, c0, n0, m0):
    c = _chunk_len(q.shape[1])
    mask = _causal_mask(c)
    k = k * (k.shape[-1] ** -0.5)
    log_f = jax.nn.log_sigmoid(f_pre)

    def step(carry, inp):
        cm, nv, m = carry
        qc, kc, vc, ic, fc = inp
        b = jnp.moveaxis(jnp.cumsum(fc, axis=1), 1, 2)
        ig = jnp.moveaxis(ic, 1, 2)
        log_w = jnp.where(mask, b[..., :, None] - b[..., None, :] + ig[..., None, :], -jnp.inf)
        log_a = b + m[..., None]
        m_t = jnp.maximum(log_a, jnp.max(log_w, axis=-1))
        w_intra = jnp.exp(log_w - m_t[..., None])
        w_inter = jnp.exp(log_a - m_t)
        s = w_intra * jnp.einsum('bthd,bshd->bhts', qc, kc)
        num = jnp.einsum('bhts,bshv->bthv', s, vc) + jnp.einsum('bht,bhvd,bthd->bthv', w_inter, cm, qc)
        den = jnp.sum(s, axis=-1) + w_inter * jnp.einsum('bhd,bthd->bht', nv, qc)
        den = jnp.maximum(jnp.abs(den), jnp.exp(-m_t))
        h = num / jnp.moveaxis(den, 1, 2)[..., None]
        m_new = m_t[..., -1]
        decay = jnp.exp(log_a[..., -1] - m_new)
        w_end = jnp.exp(b[..., -1:] - b + ig - m_new[..., None])
        cm_new = decay[..., None, None] * cm + jnp.einsum('bhs,bshv,bshd->bhvd', w_end, vc, kc)
        nv_new = decay[..., None] * nv + jnp.einsum('bhs,bshd->bhd', w_end, kc)
        return (cm_new, nv_new, m_new), h

    xs = tuple(_to_chunks(a, c) for a in (q, k, v, i_pre, log_f))
    (c1, n1, m1), h = lax.scan(step, (c0, n0, m0), xs)
    return _from_chunks(h), c1, n1, m1


def ssd_chunked(x, dt, a, bm, cm, h0):
    bsz, length = x.shape[0], x.shape[1]
    r = SSD_HEADS // SSD_GROUPS
    c = _chunk_len(length)
    mask = _causal_mask(c)
    xdt = (x * dt[..., None]).reshape(bsz, length, SSD_GROUPS, r, SSD_HEADDIM)
    log_a = (dt * a).reshape(bsz, length, SSD_GROUPS, r)

    def step(h, inp):
        xc, lac, bc, cc = inp
        acum = jnp.moveaxis(jnp.cumsum(lac, axis=1), 1, -1)
        decay = jnp.exp(jnp.where(mask, acum[..., :, None] - acum[..., None, :], -jnp.inf))
        cb = jnp.einsum('btgn,bsgn->bgts', cc, bc)
        y = jnp.einsum('bgrts,bsgrp->btgrp', decay * cb[:, :, None], xc)
        y = y + jnp.einsum('btgn,bgrpn,bgrt->btgrp', cc, h, jnp.exp(acum))
        w_end = jnp.exp(acum[..., -1:] - acum)
        h_new = jnp.exp(acum[..., -1])[..., None, None] * h + jnp.einsum('bgrs,bsgrp,bsgn->bgrpn', w_end, xc, bc)
        return h_new, y

    xs = tuple(_to_chunks(t, c) for t in (xdt, log_a, bm, cm))
    h1, y = lax.scan(step, h0.reshape(bsz, SSD_GROUPS, r, SSD_HEADDIM, SSD_STATE), xs)
    y = _from_chunks(y).reshape(bsz, length, SSD_HEADS, SSD_HEADDIM)
    return y, h1.reshape(bsz, SSD_HEADS, SSD_HEADDIM, SSD_STATE)


def retention_chunked(q, k, v, s0):
    c = _chunk_len(q.shape[1])
    log_g = jnp.log1p(-jnp.exp2(-5.0 - jnp.arange(RET_HEADS, dtype=jnp.float32)))
    j = jnp.arange(c, dtype=jnp.float32)
    rel = j[:, None] - j[None, :]
    w_intra = jnp.exp(jnp.where(_causal_mask(c), rel[None] * log_g[:, None, None], -jnp.inf))
    w_inter = jnp.exp((j[None, :] + 1.0) * log_g[:, None])
    w_end = jnp.exp((c - 1.0 - j[None, :]) * log_g[:, None])
    decay = jnp.exp(c * log_g)

    def step(s, inp):
        qc, kc, vc = inp
        att = jnp.einsum('bthd,bshd->bhts', qc, kc) * w_intra
        o = jnp.einsum('bhts,bshv->bthv', att, vc) + jnp.einsum('bthd,bhdv,ht->bthv', qc, s, w_inter)
        s_new = decay[:, None, None] * s + jnp.einsum('hs,bshd,bshv->bhdv', w_end, kc, vc)
        return s_new, o

    xs = tuple(_to_chunks(t, c) for t in (q, k, v))
    s1, o = lax.scan(step, s0, xs)
    return _from_chunks(o), s1


def causal_conv(u, conv_state, w, b):
    up = jnp.concatenate([conv_state, u], axis=1)
    y = lax.conv_general_dilated(up, w[:, None, :], window_strides=(1,), padding='VALID',
                                 dimension_numbers=('NWC', 'WIO', 'NWC'), feature_group_count=u.shape[-1])
    return jax.nn.silu(y + b), up[:, -(CONV_W - 1):]


def token_mixers(h, state, pos, lp):
    f32 = jnp.float32
    c0, n0, m0, conv0, ssd0, ret0 = state
    bsz, length = h.shape[0], h.shape[1]
    proj = jnp.einsum('bld,de->ble', h, lp['w_in'])
    (q_ml, k_ml, v_ml, o_ml, i_ml, f_ml, z, xbc, dt_raw, q_r, k_r, v_r, g_r) = jnp.split(proj, _in_proj_splits(), axis=-1)

    def heads(a, n):
        return a.astype(f32).reshape(bsz, length, n, -1)

    h_ml, c1, n1, m1 = mlstm_chunked(heads(q_ml, ML_HEADS), heads(k_ml, ML_HEADS), heads(v_ml, ML_HEADS),
                                     i_ml.astype(f32) + lp['ml_i_bias'].astype(f32),
                                     f_ml.astype(f32) + lp['ml_f_bias'].astype(f32), c0, n0, m0)
    y_ml = head_norm(h_ml, lp['ml_norm']) * jax.nn.sigmoid(o_ml.astype(f32))

    u, conv1 = causal_conv(xbc.astype(f32), conv0, lp['ssd_conv_w'].astype(f32), lp['ssd_conv_b'].astype(f32))
    xs_, b_, c_ = jnp.split(u, [SSD_WIDTH, SSD_WIDTH + SSD_GROUPS * SSD_STATE], axis=-1)
    xs_ = xs_.reshape(bsz, length, SSD_HEADS, SSD_HEADDIM)
    dt = jax.nn.softplus(dt_raw.astype(f32) + lp['ssd_dt_bias'].astype(f32))
    a = -jnp.exp(lp['ssd_a_log'].astype(f32))
    y_s, ssd1 = ssd_chunked(xs_, dt, a, b_.reshape(bsz, length, SSD_GROUPS, SSD_STATE),
                            c_.reshape(bsz, length, SSD_GROUPS, SSD_STATE), ssd0)
    y_s = y_s + lp['ssd_d'].astype(f32)[:, None] * xs_
    yz = (y_s.reshape(bsz, length, SSD_WIDTH) * jax.nn.silu(z.astype(f32))).reshape(bsz, length, SSD_GROUPS, -1)
    y_s = head_norm(yz, lp['ssd_norm'])

    qr = rotary(heads(q_r, RET_HEADS), pos)
    kr = rotary(heads(k_r, RET_HEADS), pos) * (RET_DH ** -0.5)
    o_r, ret1 = retention_chunked(qr, kr, heads(v_r, RET_HEADS), ret0)
    y_r = head_norm(o_r, lp['ret_norm']) * jax.nn.silu(g_r.astype(f32))

    y = jnp.concatenate([y_ml, y_s, y_r], axis=-1).astype(h.dtype)
    return jnp.einsum('ble,ed->bld', y, lp['w_out']), (c1, n1, m1, conv1, ssd1, ret1)


def memory_kv(mem, g, w_k, w_v):
    mn = rmsnorm(mem, g)
    b = mem.shape[0]
    k = jnp.einsum('bmd,de->bme', mn, w_k).reshape(b, MEM_LEN, CA_HEADS, CA_DH)
    v = jnp.einsum('bmd,de->bme', mn, w_v).reshape(b, MEM_LEN, CA_HEADS, CA_DH)
    return k, v


def cross_attn(h, mem_k, mem_v, w_q, w_o):
    b, length = h.shape[0], h.shape[1]
    q = jnp.einsum('bld,de->ble', h, w_q).reshape(b, length, CA_HEADS, CA_DH)
    s = jnp.einsum('bqhd,bmhd->bhqm', q, mem_k).astype(jnp.float32) * (CA_DH ** -0.5)
    p = jax.nn.softmax(s, axis=-1).astype(mem_v.dtype)
    o = jnp.einsum('bhqm,bmhd->bqhd', p, mem_v).reshape(b, length, D_MODEL)
    return jnp.einsum('bld,de->ble', o.astype(h.dtype), w_o)


def swiglu(h, w_gate, w_up, w_down):
    return (jax.nn.silu(h @ w_gate) * (h @ w_up)) @ w_down


def moe_swiglu(h, w_router, w_gate, w_up, w_down):
    logits = jnp.einsum('bld,de->ble', h, w_router).astype(jnp.float32)
    top_val, top_idx = lax.top_k(logits, TOP_K)
    gate = jax.nn.softmax(top_val, axis=-1)
    combine = jnp.einsum('blk,blke->ble', gate, jax.nn.one_hot(top_idx, N_EXPERTS, dtype=jnp.float32)).astype(h.dtype)
    out = jnp.zeros_like(h)
    for e in range(N_EXPERTS):
        act = jax.nn.silu(h @ w_gate[e]) * (h @ w_up[e])
        out = out + combine[..., e:e + 1] * (act @ w_down[e])
    return out


def run_layer(x, state, mem_k, mem_v, pos, lp, ffn):
    y, new_state = token_mixers(rmsnorm(x, lp['norm_mix']), state, pos, lp)
    x = x + y
    x = x + cross_attn(rmsnorm(x, lp['norm_ca']), mem_k, mem_v, lp['w_ca_q'], lp['w_ca_o'])
    x = x + ffn(rmsnorm(x, lp['norm_ffn']))
    return x, new_state


def setup_inputs(seed: int = 0) -> dict:
    key = jax.random.key(seed)
    keys = jax.random.split(key, 64)
    counter = [0]

    def _next():
        k = keys[counter[0]]
        counter[0] += 1
        return k

    def nrm(shape, scale=1.0):
        return scale * jax.random.normal(_next(), shape, jnp.float32)

    def gain(shape):
        return 1.0 + nrm(shape, 0.01)

    def unif(shape, lo, hi):
        return jax.random.uniform(_next(), shape, jnp.float32, lo, hi)

    dt0 = jnp.exp(unif((DEPTH, SSD_HEADS), math.log(1e-3), math.log(1e-1)))
    a_init = unif((DEPTH, SSD_HEADS), 1.0, 16.0)
    return {
        'x_prompt': nrm((BATCH, SEQ, D_MODEL)),
        'x_sample': nrm((DEC_BATCH, DEC_SEQ, D_MODEL)),
        'state_mlstm_c': nrm((DEPTH, DEC_BATCH, ML_HEADS, ML_DH, ML_DH), 0.5),
        'state_mlstm_n': nrm((DEPTH, DEC_BATCH, ML_HEADS, ML_DH), 0.5),
        'state_mlstm_m': nrm((DEPTH, DEC_BATCH, ML_HEADS)),
        'state_ssd_conv': nrm((DEPTH, DEC_BATCH, CONV_W - 1, SSD_CONV_DIM)),
        'state_ssd': nrm((DEPTH, DEC_BATCH, SSD_HEADS, SSD_HEADDIM, SSD_STATE), 0.5),
        'state_ret': nrm((DEPTH, DEC_BATCH, RET_HEADS, RET_DH, RET_DH), 0.5),
        'cache_mem_k': nrm((DEPTH, DEC_BATCH, MEM_LEN, CA_HEADS, CA_DH)),
        'cache_mem_v': nrm((DEPTH, DEC_BATCH, MEM_LEN, CA_HEADS, CA_DH)),
        'mem_prompt': nrm((BATCH, MEM_LEN, D_MODEL)),
        'norm_mix': gain((DEPTH, D_MODEL)),
        'w_in': nrm((DEPTH, D_MODEL, D_IN_PROJ), D_MODEL ** -0.5),
        'ml_i_bias': nrm((DEPTH, ML_HEADS), 0.1),
        'ml_f_bias': 3.0 + nrm((DEPTH, ML_HEADS), 0.5),
        'ml_norm': gain((DEPTH, ML_WIDTH)),
        'ssd_conv_w': nrm((DEPTH, CONV_W, SSD_CONV_DIM), CONV_W ** -0.5),
        'ssd_conv_b': nrm((DEPTH, SSD_CONV_DIM), 0.01),
        'ssd_dt_bias': dt0 + jnp.log(-jnp.expm1(-dt0)),
        'ssd_a_log': jnp.log(a_init),
        'ssd_d': 1.0 + nrm((DEPTH, SSD_HEADS), 0.1),
        'ssd_norm': gain((DEPTH, SSD_WIDTH)),
        'ret_norm': gain((DEPTH, RET_WIDTH)),
        'w_out': nrm((DEPTH, D_MIX, D_MODEL), D_MIX ** -0.5),
        'norm_ca': gain((DEPTH, D_MODEL)),
        'norm_mem': gain((DEPTH, D_MODEL)),
        'w_ca_q': nrm((DEPTH, D_MODEL, D_MODEL), D_MODEL ** -0.5),
        'w_ca_k': nrm((DEPTH, D_MODEL, D_MODEL), D_MODEL ** -0.5),
        'w_ca_v': nrm((DEPTH, D_MODEL, D_MODEL), D_MODEL ** -0.5),
        'w_ca_o': nrm((DEPTH, D_MODEL, D_MODEL), D_MODEL ** -0.5),
        'norm_ffn': gain((DEPTH, D_MODEL)),
        'ffn_w_gate': nrm((N_DENSE, D_MODEL, D_FF), D_MODEL ** -0.5),
        'ffn_w_up': nrm((N_DENSE, D_MODEL, D_FF), D_MODEL ** -0.5),
        'ffn_w_down': nrm((N_DENSE, D_FF, D_MODEL), D_FF ** -0.5),
        'moe_w_router': nrm((N_MOE, D_MODEL, N_EXPERTS), D_MODEL ** -0.5),
        'moe_w_gate': nrm((N_MOE, N_EXPERTS, D_MODEL, D_EXPERT), D_MODEL ** -0.5),
        'moe_w_up': nrm((N_MOE, N_EXPERTS, D_MODEL, D_EXPERT), D_MODEL ** -0.5),
        'moe_w_down': nrm((N_MOE, N_EXPERTS, D_EXPERT, D_MODEL), D_EXPERT ** -0.5),
        'norm_final': gain((D_MODEL,)),
    }


def reference(x_prompt, x_sample, state_mlstm_c, state_mlstm_n, state_mlstm_m, state_ssd_conv, state_ssd, state_ret,
              cache_mem_k, cache_mem_v, mem_prompt,
              norm_mix, w_in, ml_i_bias, ml_f_bias, ml_norm, ssd_conv_w, ssd_conv_b, ssd_dt_bias, ssd_a_log, ssd_d,
              ssd_norm, ret_norm, w_out, norm_ca, norm_mem, w_ca_q, w_ca_k, w_ca_v, w_ca_o, norm_ffn,
              ffn_w_gate, ffn_w_up, ffn_w_down, moe_w_router, moe_w_gate, moe_w_up, moe_w_down, norm_final):
    f32 = jnp.float32
    bp = x_prompt.shape[0]
    pos_p = jnp.arange(x_prompt.shape[1], dtype=f32)
    pos_s = PAST_LEN + jnp.arange(x_sample.shape[1], dtype=f32)
    zero_p = (jnp.zeros((bp, ML_HEADS, ML_DH, ML_DH), f32), jnp.zeros((bp, ML_HEADS, ML_DH), f32),
              jnp.zeros((bp, ML_HEADS), f32), jnp.zeros((bp, CONV_W - 1, SSD_CONV_DIM), f32),
              jnp.zeros((bp, SSD_HEADS, SSD_HEADDIM, SSD_STATE), f32), jnp.zeros((bp, RET_HEADS, RET_DH, RET_DH), f32))
    sample_states = (state_mlstm_c, state_mlstm_n, state_mlstm_m, state_ssd_conv, state_ssd, state_ret)
    new_p = [[] for _ in range(6)]
    new_s = [[] for _ in range(6)]
    mem_k_p, mem_v_p = [], []
    xp, xs = x_prompt, x_sample
    for l in range(DEPTH):
        lp = {'norm_mix': norm_mix[l], 'w_in': w_in[l], 'ml_i_bias': ml_i_bias[l], 'ml_f_bias': ml_f_bias[l],
              'ml_norm': ml_norm[l], 'ssd_conv_w': ssd_conv_w[l], 'ssd_conv_b': ssd_conv_b[l],
              'ssd_dt_bias': ssd_dt_bias[l], 'ssd_a_log': ssd_a_log[l], 'ssd_d': ssd_d[l], 'ssd_norm': ssd_norm[l],
              'ret_norm': ret_norm[l], 'w_out': w_out[l], 'norm_ca': norm_ca[l], 'w_ca_q': w_ca_q[l],
              'w_ca_o': w_ca_o[l], 'norm_ffn': norm_ffn[l]}
        j = l // 2
        if l % 2 == 0:
            ffn = functools.partial(swiglu, w_gate=ffn_w_gate[j], w_up=ffn_w_up[j], w_down=ffn_w_down[j])
        else:
            ffn = functools.partial(moe_swiglu, w_router=moe_w_router[j], w_gate=moe_w_gate[j],
                                    w_up=moe_w_up[j], w_down=moe_w_down[j])
        mk, mv = memory_kv(mem_prompt, norm_mem[l], w_ca_k[l], w_ca_v[l])
        xp, st_p = run_layer(xp, zero_p, mk, mv, pos_p, lp, ffn)
        mem_k_p.append(mk)
        mem_v_p.append(mv)
        st_in = tuple(a[l].astype(f32) for a in sample_states)
        xs, st_s = run_layer(xs, st_in, cache_mem_k[l], cache_mem_v[l], pos_s, lp, ffn)
        for lst, a in zip(new_p, st_p):
            lst.append(a)
        for lst, a in zip(new_s, st_s):
            lst.append(a)
    y_prompt = rmsnorm(xp, norm_final)
    y_sample = rmsnorm(xs, norm_final)
    return (y_prompt, y_sample,
            jnp.stack(new_p[0]), jnp.stack(new_p[1]), jnp.stack(new_p[2]), jnp.stack(new_p[3]),
            jnp.stack(new_p[4]), jnp.stack(new_p[5]), jnp.stack(mem_k_p), jnp.stack(mem_v_p),
            jnp.stack(new_s[0]), jnp.stack(new_s[1]), jnp.stack(new_s[2]), jnp.stack(new_s[3]),
            jnp.stack(new_s[4]), jnp.stack(new_s[5]))
```

```python
import functools
import math

import numpy as np
import jax
import jax.numpy as jnp
from jax import lax
from jax.experimental import pallas as pl
from jax.experimental.pallas import tpu as pltpu

F32 = jnp.float32
BF16 = jnp.bfloat16
HIGHEST = lax.Precision.HIGHEST

D_MODEL = 1024
DEPTH = 2
PAST_LEN = 16384
D_MIX = 2 * D_MODEL
ML_WIDTH = 3 * D_MIX // 8
ML_HEADS = 6
ML_DH = ML_WIDTH // ML_HEADS
SSD_WIDTH = 3 * D_MIX // 8
SSD_HEADDIM = 64
SSD_HEADS = SSD_WIDTH // SSD_HEADDIM
SSD_STATE = 128
SSD_GROUPS = 2
SSD_GROUP_WIDTH = SSD_WIDTH // SSD_GROUPS
SSD_CONV_DIM = SSD_WIDTH + 2 * SSD_GROUPS * SSD_STATE
CONV_W = 4
RET_WIDTH = D_MIX // 4
RET_HEADS = 4
RET_DH = RET_WIDTH // RET_HEADS
ROPE_BASE = 10000.0
CHUNK = 128
MEM_LEN = 256
CA_HEADS = 4
CA_DH = D_MODEL // CA_HEADS
D_FF = 11 * D_MODEL // 4
N_EXPERTS = 8
D_EXPERT = D_FF // 2
EPS = 1e-6

LANES = 128

OFF_QML = 0
OFF_KML = OFF_QML + ML_WIDTH
OFF_VML = OFF_KML + ML_WIDTH
OFF_OML = OFF_VML + ML_WIDTH
OFF_Z = OFF_OML + ML_WIDTH
OFF_XBC = OFF_Z + SSD_WIDTH
OFF_QR = OFF_XBC + SSD_CONV_DIM
OFF_KR = OFF_QR + RET_WIDTH
OFF_VR = OFF_KR + RET_WIDTH
OFF_GR = OFF_VR + RET_WIDTH
OFF_SMALL = OFF_GR + RET_WIDTH
D_PROJ = OFF_SMALL + LANES
LANE_I = 0
LANE_F = ML_HEADS
LANE_DT = 2 * ML_HEADS

VMEM_LIMIT = 56 * 1024 * 1024


def _cparams(sem):
    return pltpu.CompilerParams(dimension_semantics=sem, vmem_limit_bytes=VMEM_LIMIT)


def _const_spec(shape):
    nd = len(shape)
    return pl.BlockSpec(shape, lambda *_: (0,) * nd, pipeline_mode=pl.Buffered(1))


def _rms(x, g):
    return x * lax.rsqrt(jnp.mean(x * x, axis=-1, keepdims=True) + EPS) * g


def _sigmoid(x):
    return 1.0 / (1.0 + jnp.exp(-x))


def _silu(x):
    return x * _sigmoid(x)


def _dot(a, b):
    return jnp.dot(a, b, preferred_element_type=F32)


def _dot_nt(a, b):
    return lax.dot_general(a, b, (((1,), (1,)), ((), ())), preferred_element_type=F32)


def _dot_exact(a, b):
    return jnp.dot(a, b, preferred_element_type=F32, precision=HIGHEST)


def _norm_matmul_kernel(*refs, norm, residual):
    x_ref, g_ref, w_ref = refs[:3]
    o_ref = refs[-1]
    x = x_ref[...].astype(F32)
    if norm:
        x = _rms(x, g_ref[...])
    acc = _dot(x.astype(BF16), w_ref[...])
    if residual:
        acc = acc + refs[3][...]
    o_ref[...] = acc


def norm_matmul(x, g, w, residual=None, *, norm=True, tm=None, tn=None):
    m, k = x.shape
    n = w.shape[1]
    tm = tm or min(m, 512)
    tn = tn or min(n, 1024)
    in_specs = [pl.BlockSpec((tm, k), lambda j, i: (i, 0)),
                pl.BlockSpec((1, k), lambda j, i: (0, 0)),
                pl.BlockSpec((k, tn), lambda j, i: (0, j))]
    args = [x, g, w]
    if residual is not None:
        in_specs.append(pl.BlockSpec((tm, tn), lambda j, i: (i, j)))
        args.append(residual)
    return pl.pallas_call(
        functools.partial(_norm_matmul_kernel, norm=norm, residual=residual is not None),
        grid=(pl.cdiv(n, tn), m // tm),
        in_specs=in_specs,
        out_specs=pl.BlockSpec((tm, tn), lambda j, i: (i, j)),
        out_shape=jax.ShapeDtypeStruct((m, n), F32),
        compiler_params=_cparams(("parallel", "parallel")),
        name="norm_matmul",
    )(*args)


def _kv_proj_kernel(x_ref, g_ref, wk_ref, wv_ref, k_ref, v_ref):
    xn = _rms(x_ref[...], g_ref[...]).astype(BF16)
    k_ref[...] = _dot(xn, wk_ref[...])
    v_ref[...] = _dot(xn, wv_ref[...])


def kv_proj(mem2d, g, wk, wv, *, tm=512):
    m, d = mem2d.shape
    depth = wk.shape[0]
    out = jax.ShapeDtypeStruct((depth, m, d), F32)
    return pl.pallas_call(
        _kv_proj_kernel,
        grid=(depth, m // tm),
        in_specs=[pl.BlockSpec((tm, d), lambda l, i: (i, 0)),
                  pl.BlockSpec((None, 1, d), lambda l, i: (l, 0, 0)),
                  pl.BlockSpec((None, d, d), lambda l, i: (l, 0, 0)),
                  pl.BlockSpec((None, d, d), lambda l, i: (l, 0, 0))],
        out_specs=[pl.BlockSpec((None, tm, d), lambda l, i: (l, i, 0)),
                   pl.BlockSpec((None, tm, d), lambda l, i: (l, i, 0))],
        out_shape=[out, out],
        compiler_params=_cparams(("parallel", "parallel")),
        name="kv_proj",
    )(mem2d, g, wk, wv)


def _ca_prompt_kernel(x_ref, g_ref, wq_ref, wo_ref, k_ref, v_ref, o_ref):
    x = x_ref[...]
    xn = _rms(x, g_ref[...]).astype(BF16)
    q = _dot(xn, wq_ref[...])
    kb = k_ref[...].astype(BF16)
    vb = v_ref[...].astype(BF16)
    outs = []
    for h in range(CA_HEADS):
        sl = slice(h * CA_DH, (h + 1) * CA_DH)
        s = _dot_nt(q[:, sl].astype(BF16), kb[:, sl]) * (CA_DH ** -0.5)
        p = jnp.exp(s - jnp.max(s, axis=-1, keepdims=True))
        p = p / jnp.sum(p, axis=-1, keepdims=True)
        outs.append(_dot(p.astype(BF16), vb[:, sl]).astype(BF16))
    o = jnp.concatenate(outs, axis=-1)
    o_ref[...] = x + _dot(o, wo_ref[...])


def ca_prompt(x, g, wq, wo, mem_k, mem_v, *, tm=512):
    b, length, d = x.shape
    mlen = mem_k.shape[1]
    return pl.pallas_call(
        _ca_prompt_kernel,
        grid=(b, length // tm),
        in_specs=[pl.BlockSpec((None, tm, d), lambda i, j: (i, j, 0)),
                  _const_spec((1, d)), _const_spec((d, d)), _const_spec((d, d)),
                  pl.BlockSpec((None, mlen, d), lambda i, j: (i, 0, 0)),
                  pl.BlockSpec((None, mlen, d), lambda i, j: (i, 0, 0))],
        out_specs=pl.BlockSpec((None, tm, d), lambda i, j: (i, j, 0)),
        out_shape=jax.ShapeDtypeStruct(x.shape, F32),
        compiler_params=_cparams(("parallel", "parallel")),
        name="ca_prompt",
    )(x, g, wq, wo, mem_k, mem_v)


def _ca_sample_kernel(q_ref, k_ref, v_ref, o_ref, *, bt):
    for j in range(bt):
        q = q_ref[j:j + 1, :]
        kq = k_ref[j] * q
        vj = v_ref[j]
        outs = []
        for h in range(CA_HEADS):
            sl = slice(h * CA_DH, (h + 1) * CA_DH)
            s = jnp.sum(kq[:, sl], axis=-1, keepdims=True) * (CA_DH ** -0.5)
            p = jnp.exp(s - jnp.max(s, axis=0, keepdims=True))
            p = p / jnp.sum(p, axis=0, keepdims=True)
            outs.append(jnp.sum(p * vj[:, sl], axis=0, keepdims=True))
        o_ref[j:j + 1, :] = jnp.concatenate(outs, axis=-1)


def ca_sample(q, mem_k, mem_v, *, bt=8):
    b, d = q.shape
    mlen = mem_k.shape[1]
    return pl.pallas_call(
        functools.partial(_ca_sample_kernel, bt=bt),
        grid=(b // bt,),
        in_specs=[pl.BlockSpec((bt, d), lambda i: (i, 0)),
                  pl.BlockSpec((bt, mlen, d), lambda i: (i, 0, 0)),
                  pl.BlockSpec((bt, mlen, d), lambda i: (i, 0, 0))],
        out_specs=pl.BlockSpec((bt, d), lambda i: (i, 0)),
        out_shape=jax.ShapeDtypeStruct((b, d), F32),
        compiler_params=_cparams(("parallel",)),
        name="ca_sample",
    )(q, mem_k, mem_v)


def _ffn_kernel(x_ref, g_ref, wg_ref, wu_ref, wd_ref, o_ref, a_ref, *, fchunk):
    x = x_ref[...]
    xn = _rms(x, g_ref[...]).astype(BF16)
    dff = wg_ref.shape[1]
    for f0 in range(0, dff, fchunk):
        gate = _dot(xn, wg_ref[:, f0:f0 + fchunk])
        up = _dot(xn, wu_ref[:, f0:f0 + fchunk])
        a_ref[:, f0:f0 + fchunk] = (_silu(gate) * up).astype(BF16)
    o_ref[...] = x + _dot(a_ref[...], wd_ref[...])


def ffn_dense(x, g, wg, wu, wd, *, tm=512):
    m, d = x.shape
    dff = wg.shape[1]
    return pl.pallas_call(
        functools.partial(_ffn_kernel, fchunk=dff // 2),
        grid=(m // tm,),
        in_specs=[pl.BlockSpec((tm, d), lambda i: (i, 0)),
                  _const_spec((1, d)), _const_spec((d, dff)), _const_spec((d, dff)), _const_spec((dff, d))],
        out_specs=pl.BlockSpec((tm, d), lambda i: (i, 0)),
        out_shape=jax.ShapeDtypeStruct((m, d), F32),
        scratch_shapes=[pltpu.VMEM((tm, dff), BF16)],
        compiler_params=_cparams(("parallel",)),
        name="ffn_dense",
    )(x, g, wg, wu, wd)


def _moe_kernel(x_ref, g_ref, wrt_ref, wg_ref, wu_ref, wd_ref, gf_ref, o_ref,
                xn_ref, gate_t_ref, pos_t_ref, gate_c_ref, pos_c_ref, *, tm, sub, final_norm):
    e = pl.program_id(1)
    ne = pl.num_programs(1)
    epad = gate_t_ref.shape[0]

    @pl.when(e == 0)
    def _route():
        x = x_ref[...]
        xn = _rms(x, g_ref[...])
        xn_ref[...] = xn.astype(BF16)
        x_hi = xn.astype(BF16)
        x_lo = (xn - x_hi.astype(F32)).astype(BF16)
        w = wrt_ref[...]
        w_hi = w.astype(BF16)
        w_lo = (w - w_hi.astype(F32)).astype(BF16)
        logits = _dot_nt(w_hi, x_hi) + (_dot_nt(w_hi, x_lo) + _dot_nt(w_lo, x_hi))
        row = lax.broadcasted_iota(jnp.int32, (epad, tm), 0)
        logits = jnp.where(row < N_EXPERTS, logits, -jnp.inf)
        m1 = jnp.max(logits, axis=0, keepdims=True)
        i1 = jnp.min(jnp.where(logits == m1, row, epad), axis=0, keepdims=True)
        rest = jnp.where(row == i1, -jnp.inf, logits)
        m2 = jnp.max(rest, axis=0, keepdims=True)
        i2 = jnp.min(jnp.where(rest == m2, row, epad), axis=0, keepdims=True)
        e2 = jnp.exp(m2 - m1)
        den = 1.0 + e2
        gate_t = jnp.where(row == i1, 1.0 / den, 0.0) + jnp.where(row == i2, e2 / den, 0.0)
        sel_t = jnp.where((row == i1) | (row == i2), 1.0, 0.0)
        r_i = lax.broadcasted_iota(jnp.int32, (tm, tm), 0)
        c_i = lax.broadcasted_iota(jnp.int32, (tm, tm), 1)
        upper = jnp.where(r_i < c_i, 1.0, 0.0).astype(BF16)
        pos_t = _dot(sel_t.astype(BF16), upper)
        pos_t = jnp.where(sel_t > 0, pos_t, -1.0)
        gate_t_ref[...] = gate_t
        pos_t_ref[...] = pos_t
        pad = jnp.zeros((LANES - epad, tm), F32)
        gate_c_ref[...] = jnp.concatenate([gate_t, pad], axis=0).T
        pos_c_ref[...] = jnp.concatenate([pos_t, pad - 1.0], axis=0).T
        o_ref[...] = x

    pos_row = pos_t_ref[pl.ds(e, 1), :]
    lane = lax.broadcasted_iota(jnp.int32, (tm, LANES), 1)
    pos_col = jnp.sum(jnp.where(lane == e, pos_c_ref[...], 0.0), axis=-1, keepdims=True)
    gate_col = jnp.sum(jnp.where(lane == e, gate_c_ref[...], 0.0), axis=-1, keepdims=True)
    count = jnp.sum(jnp.where(pos_row >= 0, 1.0, 0.0)).astype(jnp.int32)
    nsub = (count + (sub - 1)) // sub

    def sub_body(j, carry):
        base = (j * sub).astype(F32)
        slot_r = lax.broadcasted_iota(jnp.int32, (sub, tm), 0).astype(F32) + base
        gather = jnp.where(pos_row == slot_r, 1.0, 0.0).astype(BF16)
        slot_c = lax.broadcasted_iota(jnp.int32, (tm, sub), 1).astype(F32) + base
        scatter = jnp.where(pos_col == slot_c, 1.0, 0.0).astype(BF16)
        xg = _dot(gather, xn_ref[...]).astype(BF16)
        act = (_silu(_dot(xg, wg_ref[...])) * _dot(xg, wu_ref[...])).astype(BF16)
        ye = _dot(act, wd_ref[...]).astype(BF16)
        o_ref[...] += gate_col * _dot(scatter, ye)
        return carry

    lax.fori_loop(0, nsub, sub_body, 0)

    if final_norm:
        @pl.when(e == ne - 1)
        def _final():
            o_ref[...] = _rms(o_ref[...], gf_ref[...])


def moe_ffn(x, g, w_router_t, wg, wu, wd, g_final, *, tm, sub, final_norm):
    m, d = x.shape
    ne, _, de = wg.shape
    epad = w_router_t.shape[0]
    return pl.pallas_call(
        functools.partial(_moe_kernel, tm=tm, sub=sub, final_norm=final_norm),
        grid=(m // tm, ne),
        in_specs=[pl.BlockSpec((tm, d), lambda i, e: (i, 0)),
                  pl.BlockSpec((1, d), lambda i, e: (0, 0)),
                  pl.BlockSpec((epad, d), lambda i, e: (0, 0)),
                  pl.BlockSpec((None, d, de), lambda i, e: (e, 0, 0)),
                  pl.BlockSpec((None, d, de), lambda i, e: (e, 0, 0)),
                  pl.BlockSpec((None, de, d), lambda i, e: (e, 0, 0)),
                  pl.BlockSpec((1, d), lambda i, e: (0, 0))],
        out_specs=pl.BlockSpec((tm, d), lambda i, e: (i, 0)),
        out_shape=jax.ShapeDtypeStruct((m, d), F32),
        scratch_shapes=[pltpu.VMEM((tm, d), BF16),
                        pltpu.VMEM((epad, tm), F32), pltpu.VMEM((epad, tm), F32),
                        pltpu.VMEM((tm, LANES), F32), pltpu.VMEM((tm, LANES), F32)],
        compiler_params=_cparams(("parallel", "arbitrary")),
        name="moe_ffn",
    )(x, g, w_router_t, wg, wu, wd, g_final)


def _retention_constants(c):
    log_g = np.log1p(-np.exp2(-5.0 - np.arange(RET_HEADS, dtype=np.float32))).astype(np.float32)
    j = np.arange(c, dtype=np.float32)
    rel = j[:, None] - j[None, :]
    w_intra = np.where(rel >= 0, np.exp(rel[None] * log_g[:, None, None]), 0.0).astype(np.float32)
    w_inter = np.exp((j[None, :] + 1.0) * log_g[:, None]).astype(np.float32)
    w_end = np.exp((c - 1.0 - j[None, :]) * log_g[:, None]).astype(np.float32)
    decay = np.exp(c * log_g).astype(np.float32)
    cols = np.zeros((c, LANES), np.float32)
    cols[:, :RET_HEADS] = w_inter.T
    cols[:, RET_HEADS:2 * RET_HEADS] = w_end.T
    return w_intra, cols, [float(d) for d in decay]


def _rotary_tables(pos):
    half = RET_DH // 2
    inv = ROPE_BASE ** (-jnp.arange(half, dtype=F32) / half)
    ang = pos[:, None] * inv[None, :]
    cos, sin = jnp.cos(ang), jnp.sin(ang)
    return jnp.concatenate([cos, cos], axis=-1), jnp.concatenate([-sin, sin], axis=-1)


def _head_expand_matrix():
    e = np.zeros((LANES, SSD_WIDTH), np.float32)
    for h in range(SSD_HEADS):
        e[LANE_DT + h, h * SSD_HEADDIM:(h + 1) * SSD_HEADDIM] = 1.0
    return e


def _softplus_parts(v):
    sp = jnp.log1p(jnp.exp(-jnp.abs(v)))
    return jnp.minimum(v, 0.0) - sp, jnp.maximum(v, 0.0) + sp


def _mixer_prompt_kernel(x_ref, g_ref, win_ref, wout_ref, bias_ref, alog_ref, convw_ref, convb_ref,
                         mlg_ref, ssdd_ref, ssdg_ref, retg_ref, cos_ref, sin_ref, retw_ref, retcol_ref,
                         expand_ref,
                         xo_ref, c_ref, n_ref, m_ref, conv_ref, ssd_ref, ret_ref,
                         proj_s, y_s, u_s, *, tm, ret_decay):
    c = CHUNK
    ti = pl.program_id(1)

    @pl.when(ti == 0)
    def _init():
        c_ref[...] = jnp.zeros_like(c_ref)
        n_ref[...] = jnp.zeros_like(n_ref)
        m_ref[...] = jnp.zeros_like(m_ref)
        ssd_ref[...] = jnp.zeros_like(ssd_ref)
        ret_ref[...] = jnp.zeros_like(ret_ref)
        u_s[0:8, :] = jnp.zeros((8, SSD_CONV_DIM), F32)

    x = x_ref[...]
    xn = _rms(x, g_ref[...]).astype(BF16)
    for off in range(0, D_PROJ, 1024):
        wdt = min(1024, D_PROJ - off)
        proj_s[:, off:off + wdt] = _dot(xn, win_ref[:, off:off + wdt])

    r_i = lax.broadcasted_iota(jnp.int32, (c, c), 0)
    c_i = lax.broadcasted_iota(jnp.int32, (c, c), 1)
    causal = r_i >= c_i
    tril = jnp.where(causal, 1.0, 0.0)
    lane1 = lax.broadcasted_iota(jnp.int32, (1, LANES), 1)
    lane_c = lax.broadcasted_iota(jnp.int32, (c, LANES), 1)
    a_vec = -jnp.exp(alog_ref[...])
    expand = expand_ref[...]
    bias = bias_ref[...]

    def chunk_body(ci, carry):
        r0 = pl.multiple_of(ci * c, c)
        rows = pl.ds(r0, c)
        small = proj_s[rows, OFF_SMALL:OFF_SMALL + LANES] + bias
        logf, dt = _softplus_parts(small)
        is_f = (lane_c >= LANE_F) & (lane_c < LANE_DT)
        is_dt = (lane_c >= LANE_DT) & (lane_c < LANE_DT + SSD_HEADS)
        incr = jnp.where(is_f, logf, jnp.where(is_dt, dt * a_vec, 0.0))
        cs = _dot_exact(tril, incr)
        cs_t = cs.T
        small_t = small.T

        m_vec = m_ref[...]
        m_out = m_vec
        for h in range(ML_HEADS):
            hs = slice(h * ML_DH, (h + 1) * ML_DH)
            b_col = cs[:, LANE_F + h:LANE_F + h + 1]
            b_row = cs_t[LANE_F + h:LANE_F + h + 1, :]
            i_col = small[:, LANE_I + h:LANE_I + h + 1]
            i_row = small_t[LANE_I + h:LANE_I + h + 1, :]
            m_prev = m_vec[:, h:h + 1]
            lw = jnp.where(causal, b_col + (i_row - b_row), -jnp.inf)
            log_a = b_col + m_prev
            m_t = jnp.maximum(log_a, jnp.max(lw, axis=-1, keepdims=True))
            w_intra = jnp.exp(lw - m_t)
            w_inter = jnp.exp(log_a - m_t)
            q = proj_s[rows, OFF_QML + h * ML_DH:OFF_QML + (h + 1) * ML_DH]
            k = proj_s[rows, OFF_KML + h * ML_DH:OFF_KML + (h + 1) * ML_DH] * (ML_DH ** -0.5)
            v = proj_s[rows, OFF_VML + h * ML_DH:OFF_VML + (h + 1) * ML_DH]
            og = proj_s[rows, OFF_OML + h * ML_DH:OFF_OML + (h + 1) * ML_DH]
            qb, kb, vb = q.astype(BF16), k.astype(BF16), v.astype(BF16)
            s = w_intra * _dot_nt(qb, kb)
            cm = c_ref[h]
            n_row = n_ref[h:h + 1, :]
            num = _dot(s.astype(BF16), vb) + w_inter * _dot_nt(qb, cm.astype(BF16))
            den = jnp.sum(s, axis=-1, keepdims=True) + w_inter * jnp.sum(q * n_row, axis=-1, keepdims=True)
            den = jnp.maximum(jnp.abs(den), jnp.exp(-m_t))
            hh = num / den
            m_new = m_t[c - 1:c, :]
            decay = jnp.exp(log_a[c - 1:c, :] - m_new)
            w_end = jnp.exp(b_col[c - 1:c, :] - b_col + i_col - m_new)
            c_ref[h] = decay * cm + _dot((v * w_end).T.astype(BF16), kb)
            n_ref[h:h + 1, :] = decay * n_row + jnp.sum(k * w_end, axis=0, keepdims=True)
            m_out = jnp.where(lane1 == h, m_new, m_out)
            hn = hh * lax.rsqrt(jnp.mean(hh * hh, axis=-1, keepdims=True) + EPS)
            y_s[rows, hs] = (hn * mlg_ref[:, hs] * _sigmoid(og)).astype(BF16)
        m_ref[...] = m_out

        u = proj_s[rows, OFF_XBC:OFF_XBC + SSD_CONV_DIM]
        u_s[8:8 + c, :] = u
        conv = convb_ref[...] + convw_ref[3:4, :] * u
        for jj in range(CONV_W - 1):
            conv = conv + convw_ref[jj:jj + 1, :] * u_s[5 + jj:5 + jj + c, :]
        u_s[0:8, :] = u_s[c:c + 8, :]
        xc = _silu(conv)
        xs = xc[:, :SSD_WIDTH]
        dt_exp = _dot_exact(jnp.where(is_dt, dt, 0.0), expand)
        acum_exp = _dot_exact(jnp.where(is_dt, cs, 0.0), expand)
        xdt = xs * dt_exp
        w_end_exp = jnp.exp(acum_exp[c - 1:c, :] - acum_exp)
        xw = xdt * w_end_exp
        ea_exp = jnp.exp(acum_exp)
        lane_lo = lane_c < SSD_HEADDIM
        hpg = SSD_HEADS // SSD_GROUPS
        y_cols = []
        for g in range(SSD_GROUPS):
            gw = slice(g * SSD_GROUP_WIDTH, (g + 1) * SSD_GROUP_WIDTH)
            bg = xc[:, SSD_WIDTH + g * SSD_STATE:SSD_WIDTH + (g + 1) * SSD_STATE].astype(BF16)
            cg = xc[:, SSD_WIDTH + (SSD_GROUPS + g) * SSD_STATE:
                    SSD_WIDTH + (SSD_GROUPS + g + 1) * SSD_STATE].astype(BF16)
            cb = _dot_nt(cg, bg)
            st = ssd_ref[gw, :]
            inter = _dot_nt(cg, st.astype(BF16)) * ea_exp[:, gw]
            for pr in range(hpg // 2):
                h0 = g * hpg + 2 * pr
                xp = xdt[:, h0 * SSD_HEADDIM:(h0 + 2) * SSD_HEADDIM].astype(BF16)
                pair = []
                for hidx in (h0, h0 + 1):
                    a_col = cs[:, LANE_DT + hidx:LANE_DT + hidx + 1]
                    a_row = cs_t[LANE_DT + hidx:LANE_DT + hidx + 1, :]
                    dec = jnp.exp(jnp.where(causal, a_col - a_row, -jnp.inf))
                    pair.append(_dot((dec * cb).astype(BF16), xp))
                y_cols.append(jnp.where(lane_lo, pair[0], pair[1]) + inter[:, pr * LANES:(pr + 1) * LANES])
            upd = _dot(xw[:, gw].T.astype(BF16), bg)
            for r in range(hpg):
                hidx = g * hpg + r
                scal = jnp.exp(cs[c - 1:c, LANE_DT + hidx:LANE_DT + hidx + 1])
                hr = slice(hidx * SSD_HEADDIM, (hidx + 1) * SSD_HEADDIM)
                ssd_ref[hr, :] = scal * ssd_ref[hr, :] + upd[r * SSD_HEADDIM:(r + 1) * SSD_HEADDIM, :]
        ys = jnp.concatenate(y_cols, axis=-1) + ssdd_ref[...] * xs
        yz = ys * _silu(proj_s[rows, OFF_Z:OFF_Z + SSD_WIDTH])
        for g in range(SSD_GROUPS):
            gw = slice(g * SSD_GROUP_WIDTH, (g + 1) * SSD_GROUP_WIDTH)
            seg = yz[:, gw]
            seg = seg * lax.rsqrt(jnp.mean(seg * seg, axis=-1, keepdims=True) + EPS) * ssdg_ref[:, gw]
            y_s[rows, ML_WIDTH + g * SSD_GROUP_WIDTH:ML_WIDTH + (g + 1) * SSD_GROUP_WIDTH] = seg.astype(BF16)

        cosb = cos_ref[rows, :]
        sinb = sin_ref[rows, :]
        retcol = retcol_ref[...]
        for h in range(RET_HEADS):
            hs = slice(h * RET_DH, (h + 1) * RET_DH)
            q = proj_s[rows, OFF_QR + h * RET_DH:OFF_QR + (h + 1) * RET_DH]
            k = proj_s[rows, OFF_KR + h * RET_DH:OFF_KR + (h + 1) * RET_DH]
            v = proj_s[rows, OFF_VR + h * RET_DH:OFF_VR + (h + 1) * RET_DH]
            gr = proj_s[rows, OFF_GR + h * RET_DH:OFF_GR + (h + 1) * RET_DH]
            qr = q * cosb + pltpu.roll(q, RET_DH // 2, 1) * sinb
            kr = (k * cosb + pltpu.roll(k, RET_DH // 2, 1) * sinb) * (RET_DH ** -0.5)
            qb, kb, vb = qr.astype(BF16), kr.astype(BF16), v.astype(BF16)
            att = _dot_nt(qb, kb) * retw_ref[h]
            sm = ret_ref[h]
            o = _dot(att.astype(BF16), vb) + _dot(qb, sm.astype(BF16)) * retcol[:, h:h + 1]
            kw = kr * retcol[:, RET_HEADS + h:RET_HEADS + h + 1]
            ret_ref[h] = ret_decay[h] * sm + _dot(kw.T.astype(BF16), vb)
            on = o * lax.rsqrt(jnp.mean(o * o, axis=-1, keepdims=True) + EPS)
            y_s[rows, ML_WIDTH + SSD_WIDTH + h * RET_DH:ML_WIDTH + SSD_WIDTH + (h + 1) * RET_DH] = (
                on * retg_ref[:, hs] * _silu(gr)).astype(BF16)
        return carry

    lax.fori_loop(0, tm // c, chunk_body, 0)
    xo_ref[...] = x + _dot(y_s[...], wout_ref[...])
    conv_ref[...] = u_s[5:8, :]


def mixer_prompt(x, lw, consts, *, tm):
    b, length, d = x.shape
    c = CHUNK
    row = lambda w: _const_spec((1, w))
    state_spec = lambda *shape: pl.BlockSpec((None,) + shape, lambda i, j: (i,) + (0,) * len(shape))
    out_shapes = [jax.ShapeDtypeStruct(x.shape, F32),
                  jax.ShapeDtypeStruct((b, ML_HEADS, ML_DH, ML_DH), F32),
                  jax.ShapeDtypeStruct((b, ML_HEADS, ML_DH), F32),
                  jax.ShapeDtypeStruct((b, 1, LANES), F32),
                  jax.ShapeDtypeStruct((b, CONV_W - 1, SSD_CONV_DIM), F32),
                  jax.ShapeDtypeStruct((b, SSD_WIDTH, SSD_STATE), F32),
                  jax.ShapeDtypeStruct((b, RET_HEADS, RET_DH, RET_DH), F32)]
    return pl.pallas_call(
        functools.partial(_mixer_prompt_kernel, tm=tm, ret_decay=consts["ret_decay"]),
        grid=(b, length // tm),
        in_specs=[pl.BlockSpec((None, tm, d), lambda i, j: (i, j, 0)),
                  row(d), _const_spec((d, D_PROJ)), _const_spec((D_MIX, d)),
                  row(LANES), row(LANES), _const_spec((CONV_W, SSD_CONV_DIM)), row(SSD_CONV_DIM),
                  row(ML_WIDTH), row(SSD_WIDTH), row(SSD_WIDTH), row(RET_WIDTH),
                  pl.BlockSpec((tm, LANES), lambda i, j: (j, 0)),
                  pl.BlockSpec((tm, LANES), lambda i, j: (j, 0)),
                  _const_spec((RET_HEADS, c, c)), _const_spec((c, LANES)), _const_spec((LANES, SSD_WIDTH))],
        out_specs=[pl.BlockSpec((None, tm, d), lambda i, j: (i, j, 0)),
                   state_spec(ML_HEADS, ML_DH, ML_DH), state_spec(ML_HEADS, ML_DH), state_spec(1, LANES),
                   state_spec(CONV_W - 1, SSD_CONV_DIM), state_spec(SSD_WIDTH, SSD_STATE),
                   state_spec(RET_HEADS, RET_DH, RET_DH)],
        out_shape=out_shapes,
        scratch_shapes=[pltpu.VMEM((tm, D_PROJ), F32), pltpu.VMEM((tm, D_MIX), BF16),
                        pltpu.VMEM((c + 8, SSD_CONV_DIM), F32)],
        compiler_params=_cparams(("parallel", "arbitrary")),
        name="mixer_prompt",
    )(x, lw["norm_mix"], lw["w_in"], lw["w_out"], lw["bias"], lw["alog"], lw["conv_w"], lw["conv_b"],
      lw["ml_norm"], lw["ssd_d"], lw["ssd_norm"], lw["ret_norm"], consts["cos_p"], consts["sin_p"],
      consts["ret_w"], consts["ret_cols"], consts["expand"])


def _mixer_sample_kernel(proj_ref, small_ref, bias_ref, alog_ref, convw_ref, convb_ref, mlg_ref, ssdd_ref,
                         ssdg_ref, retg_ref, cos_ref, sin_ref, expand_ref,
                         c_in, n_in, m_in, conv_in, ssd_in, ret_in,
                         y_ref, c_out, n_out, m_out, conv_out, ssd_out, ret_out,
                         wi_s, wt_s, en_s, dtx_s, eax_s, *, bt, ret_gamma):
    small = small_ref[...] + bias_ref[...]
    logf, dt = _softplus_parts(small)
    lane_b = lax.broadcasted_iota(jnp.int32, (bt, LANES), 1)
    is_dt = (lane_b >= LANE_DT) & (lane_b < LANE_DT + SSD_HEADS)
    log_a = pltpu.roll(logf, LANES - LANE_F, 1) + m_in[...]
    m_t = jnp.maximum(log_a, small)
    m_out[...] = m_t
    wi_s[...] = jnp.exp(small - m_t)
    wt_s[...] = jnp.exp(log_a - m_t)
    en_s[...] = jnp.exp(-m_t)
    a_vec = -jnp.exp(alog_ref[...])
    expand = expand_ref[...]
    dtx_s[...] = _dot_exact(jnp.where(is_dt, dt, 0.0), expand)
    eax_s[...] = jnp.exp(_dot_exact(jnp.where(is_dt, dt * a_vec, 0.0), expand))

    r_i = lax.broadcasted_iota(jnp.int32, (LANES, LANES), 0)
    c_i = lax.broadcasted_iota(jnp.int32, (LANES, LANES), 1)
    eye = jnp.where(r_i == c_i, 1.0, 0.0)

    def to_col(rowv):
        return jnp.sum(eye * rowv, axis=-1, keepdims=True)

    def to_row(colv):
        return jnp.sum(eye * colv, axis=0, keepdims=True)

    cosb = cos_ref[...]
    sinb = sin_ref[...]
    hpg = SSD_HEADS // SSD_GROUPS

    sub_l = lax.broadcasted_iota(jnp.int32, (bt, LANES), 0)
    sub_w = lax.broadcasted_iota(jnp.int32, (bt, SSD_WIDTH), 0)

    def body(j, carry):
        pick_l = lambda ref: jnp.sum(jnp.where(sub_l == j, ref[...], 0.0), axis=0, keepdims=True)
        pick_w = lambda ref: jnp.sum(jnp.where(sub_w == j, ref[...], 0.0), axis=0, keepdims=True)
        wi = pick_l(wi_s)
        wt = pick_l(wt_s)
        en = pick_l(en_s)
        for h in range(ML_HEADS):
            hs = slice(h * ML_DH, (h + 1) * ML_DH)
            q = proj_ref[j, :,OFF_QML + h * ML_DH:OFF_QML + (h + 1) * ML_DH]
            k = proj_ref[j, :,OFF_KML + h * ML_DH:OFF_KML + (h + 1) * ML_DH] * (ML_DH ** -0.5)
            v = proj_ref[j, :,OFF_VML + h * ML_DH:OFF_VML + (h + 1) * ML_DH]
            og = proj_ref[j, :,OFF_OML + h * ML_DH:OFF_OML + (h + 1) * ML_DH]
            w_in_h = wi[:, h:h + 1]
            w_tr_h = wt[:, h:h + 1]
            c_new = w_tr_h * c_in[j, h] + w_in_h * (to_col(v) * k)
            n_new = w_tr_h * n_in[j, h:h + 1, :] + w_in_h * k
            c_out[j, h] = c_new
            n_out[j, h:h + 1, :] = n_new
            num = to_row(jnp.sum(c_new * q, axis=-1, keepdims=True))
            den = jnp.sum(n_new * q, axis=-1, keepdims=True)
            den = jnp.maximum(jnp.abs(den), en[:, h:h + 1])
            hh = num / den
            hn = hh * lax.rsqrt(jnp.mean(hh * hh, axis=-1, keepdims=True) + EPS)
            y_ref[j, :,hs] = hn * mlg_ref[:, hs] * _sigmoid(og)
        u = proj_ref[j, :,OFF_XBC:OFF_XBC + SSD_CONV_DIM]
        prev = conv_in[j]
        conv = convb_ref[...] + convw_ref[CONV_W - 1:CONV_W, :] * u
        for jj in range(CONV_W - 1):
            conv = conv + convw_ref[jj:jj + 1, :] * prev[jj:jj + 1, :]
        conv_out[j, 0:CONV_W - 2, :] = prev[1:CONV_W - 1, :]
        conv_out[j, CONV_W - 2:CONV_W - 1, :] = u
        xc = _silu(conv)
        xs = xc[:, :SSD_WIDTH]
        xdt = xs * pick_w(dtx_s)
        ea = pick_w(eax_s)
        y_cols = []
        for g in range(SSD_GROUPS):
            bg = xc[:, SSD_WIDTH + g * SSD_STATE:SSD_WIDTH + (g + 1) * SSD_STATE]
            cg = xc[:, SSD_WIDTH + (SSD_GROUPS + g) * SSD_STATE:SSD_WIDTH + (SSD_GROUPS + g + 1) * SSD_STATE]
            for pr in range(hpg // 2):
                h0 = g * hpg + 2 * pr
                pw = slice(h0 * SSD_HEADDIM, (h0 + 2) * SSD_HEADDIM)
                st_new = to_col(ea[:, pw]) * ssd_in[j, pw, :] + to_col(xdt[:, pw]) * bg
                ssd_out[j, pw, :] = st_new
                y_cols.append(to_row(jnp.sum(st_new * cg, axis=-1, keepdims=True)))
        ys = jnp.concatenate(y_cols, axis=-1) + ssdd_ref[...] * xs
        yz = ys * _silu(proj_ref[j, :,OFF_Z:OFF_Z + SSD_WIDTH])
        for g in range(SSD_GROUPS):
            gw = slice(g * SSD_GROUP_WIDTH, (g + 1) * SSD_GROUP_WIDTH)
            seg = yz[:, gw]
            y_ref[j, :,ML_WIDTH + g * SSD_GROUP_WIDTH:ML_WIDTH + (g + 1) * SSD_GROUP_WIDTH] = (
                seg * lax.rsqrt(jnp.mean(seg * seg, axis=-1, keepdims=True) + EPS) * ssdg_ref[:, gw])
        for h in range(RET_HEADS):
            hs = slice(h * RET_DH, (h + 1) * RET_DH)
            q = proj_ref[j, :,OFF_QR + h * RET_DH:OFF_QR + (h + 1) * RET_DH]
            k = proj_ref[j, :,OFF_KR + h * RET_DH:OFF_KR + (h + 1) * RET_DH]
            v = proj_ref[j, :,OFF_VR + h * RET_DH:OFF_VR + (h + 1) * RET_DH]
            gr = proj_ref[j, :,OFF_GR + h * RET_DH:OFF_GR + (h + 1) * RET_DH]
            qr = q * cosb + pltpu.roll(q, RET_DH // 2, 1) * sinb
            kr = (k * cosb + pltpu.roll(k, RET_DH // 2, 1) * sinb) * (RET_DH ** -0.5)
            s_new = ret_gamma[h] * ret_in[j, h] + to_col(kr) * v
            ret_out[j, h] = s_new
            o = jnp.sum(s_new * to_col(qr), axis=0, keepdims=True)
            on = o * lax.rsqrt(jnp.mean(o * o, axis=-1, keepdims=True) + EPS)
            y_ref[j, :,ML_WIDTH + SSD_WIDTH + h * RET_DH:ML_WIDTH + SSD_WIDTH + (h + 1) * RET_DH] = (
                on * retg_ref[:, hs] * _silu(gr))
        return carry

    lax.fori_loop(0, bt, body, 0)


def mixer_sample(proj, lw, consts, states, *, bt=8):
    b = proj.shape[0]
    c0, n0, m0, conv0, ssd0, ret0 = states
    row = lambda w: pl.BlockSpec((1, w), lambda i: (0, 0))
    blk = lambda *shape: pl.BlockSpec((bt,) + shape, lambda i: (i,) + (0,) * len(shape))
    state_specs = [blk(ML_HEADS, ML_DH, ML_DH), blk(ML_HEADS, ML_DH), blk(LANES),
                   blk(CONV_W - 1, SSD_CONV_DIM), blk(SSD_WIDTH, SSD_STATE), blk(RET_HEADS, RET_DH, RET_DH)]
    state_shapes = [jax.ShapeDtypeStruct(s.shape, F32) for s in (c0, n0, m0, conv0, ssd0, ret0)]
    proj3 = proj.reshape(b, 1, D_PROJ)
    small = proj[:, OFF_SMALL:OFF_SMALL + LANES]
    outs = pl.pallas_call(
        functools.partial(_mixer_sample_kernel, bt=bt, ret_gamma=consts["ret_gamma"]),
        grid=(b // bt,),
        in_specs=[blk(1, D_PROJ), blk(LANES), row(LANES), row(LANES),
                  pl.BlockSpec((CONV_W, SSD_CONV_DIM), lambda i: (0, 0)), row(SSD_CONV_DIM),
                  row(ML_WIDTH), row(SSD_WIDTH), row(SSD_WIDTH), row(RET_WIDTH), row(LANES), row(LANES),
                  pl.BlockSpec((LANES, SSD_WIDTH), lambda i: (0, 0))] + state_specs,
        out_specs=[blk(1, D_MIX)] + state_specs,
        out_shape=[jax.ShapeDtypeStruct((b, 1, D_MIX), F32)] + state_shapes,
        scratch_shapes=[pltpu.VMEM((bt, LANES), F32)] * 3 + [pltpu.VMEM((bt, SSD_WIDTH), F32)] * 2,
        compiler_params=_cparams(("parallel",)),
        name="mixer_sample",
    )(proj3, small, lw["bias"], lw["alog"], lw["conv_w"], lw["conv_b"], lw["ml_norm"], lw["ssd_d"],
      lw["ssd_norm"], lw["ret_norm"], consts["cos_s"], consts["sin_s"], consts["expand"],
      c0, n0, m0, conv0, ssd0, ret0)
    return [outs[0].reshape(b, D_MIX)] + list(outs[1:])


def _pack_in_proj(w):
    sizes = [ML_WIDTH] * 4 + [ML_HEADS, ML_HEADS, SSD_WIDTH, SSD_CONV_DIM, SSD_HEADS] + [RET_WIDTH] * 4
    offs = np.concatenate([[0], np.cumsum(sizes)])
    seg = lambda i: w[:, int(offs[i]):int(offs[i + 1])]
    small = jnp.concatenate([seg(4), seg(5), seg(8)], axis=1)
    small = jnp.pad(small, ((0, 0), (0, LANES - small.shape[1])))
    cols = [seg(0), seg(1), seg(2), seg(3), seg(6), seg(7), seg(9), seg(10), seg(11), seg(12), small]
    return jnp.concatenate(cols, axis=1).astype(BF16)


def _lane_row(parts):
    v = jnp.concatenate([p.astype(F32) for p in parts])
    return jnp.pad(v, (0, LANES - v.shape[0]))[None, :]


def kernel(x_prompt, x_sample, state_mlstm_c, state_mlstm_n, state_mlstm_m, state_ssd_conv, state_ssd, state_ret,
           cache_mem_k, cache_mem_v, mem_prompt,
           norm_mix, w_in, ml_i_bias, ml_f_bias, ml_norm, ssd_conv_w, ssd_conv_b, ssd_dt_bias, ssd_a_log, ssd_d,
           ssd_norm, ret_norm, w_out, norm_ca, norm_mem, w_ca_q, w_ca_k, w_ca_v, w_ca_o, norm_ffn,
           ffn_w_gate, ffn_w_up, ffn_w_down, moe_w_router, moe_w_gate, moe_w_up, moe_w_down, norm_final):
    bp, seq, d = x_prompt.shape
    bs = x_sample.shape[0]
    assert x_sample.shape[1] == 1 and seq % CHUNK == 0 and DEPTH % 2 == 0

    ret_w, ret_cols, ret_decay = _retention_constants(CHUNK)
    _, _, ret_gamma = _retention_constants(1)
    cos_p, sin_p = _rotary_tables(jnp.arange(seq, dtype=F32))
    cos_s, sin_s = _rotary_tables(PAST_LEN + jnp.arange(1, dtype=F32))
    consts = {"ret_w": jnp.asarray(ret_w), "ret_cols": jnp.asarray(ret_cols), "ret_decay": ret_decay,
              "ret_gamma": ret_gamma, "cos_p": cos_p, "sin_p": sin_p, "cos_s": cos_s, "sin_s": sin_s,
              "expand": jnp.asarray(_head_expand_matrix())}
    zeros12 = jnp.zeros((2 * ML_HEADS,), F32)
    ones_row = jnp.ones((1, D_MIX), F32)

    mem2d = mem_prompt.reshape(bp * MEM_LEN, d)
    mem_k_p, mem_v_p = kv_proj(mem2d, norm_mem[:, None, :], w_ca_k.astype(BF16), w_ca_v.astype(BF16))

    xp = x_prompt
    xs = x_sample.reshape(bs, d)
    new_p = [[] for _ in range(6)]
    new_s = [[] for _ in range(6)]
    for l in range(DEPTH):
        lw = {"norm_mix": norm_mix[l][None, :], "w_in": _pack_in_proj(w_in[l]), "w_out": w_out[l].astype(BF16),
              "bias": _lane_row([ml_i_bias[l], ml_f_bias[l], ssd_dt_bias[l]]),
              "alog": _lane_row([zeros12, ssd_a_log[l]]),
              "conv_w": ssd_conv_w[l], "conv_b": ssd_conv_b[l][None, :], "ml_norm": ml_norm[l][None, :],
              "ssd_d": jnp.repeat(ssd_d[l], SSD_HEADDIM)[None, :], "ssd_norm": ssd_norm[l][None, :],
              "ret_norm": ret_norm[l][None, :]}
        wq, wo = w_ca_q[l].astype(BF16), w_ca_o[l].astype(BF16)
        g_ca, g_ffn = norm_ca[l][None, :], norm_ffn[l][None, :]
        last = l == DEPTH - 1
        j = l // 2
        if l % 2 == 0:
            ffn_w = (ffn_w_gate[j].astype(BF16), ffn_w_up[j].astype(BF16), ffn_w_down[j].astype(BF16))
        else:
            wrt = jnp.pad(moe_w_router[j].T, ((0, 2 * N_EXPERTS - N_EXPERTS), (0, 0)))
            ffn_w = (wrt, moe_w_gate[j].astype(BF16), moe_w_up[j].astype(BF16), moe_w_down[j].astype(BF16))

        def run_ffn(x2d, tm, sub):
            if l % 2 == 0:
                return ffn_dense(x2d, g_ffn, *ffn_w, tm=tm)
            return moe_ffn(x2d, g_ffn, *ffn_w, norm_final[None, :], tm=tm, sub=sub, final_norm=last)

        xp, c1, n1, m1, conv1, ssd1, ret1 = mixer_prompt(xp, lw, consts, tm=256)
        st_p = (c1, n1, m1[:, 0, :ML_HEADS], conv1,
                ssd1.reshape(bp, SSD_HEADS, SSD_HEADDIM, SSD_STATE), ret1)
        xp = ca_prompt(xp, g_ca, wq, wo, mem_k_p[l].reshape(bp, MEM_LEN, d), mem_v_p[l].reshape(bp, MEM_LEN, d))
        xp = run_ffn(xp.reshape(bp * seq, d), 1024 if l % 2 else 512, 256).reshape(bp, seq, d)

        proj = norm_matmul(xs, lw["norm_mix"], lw["w_in"], tn=D_PROJ // 3)
        m_pad = jnp.pad(state_mlstm_m[l], ((0, 0), (0, LANES - ML_HEADS)))
        y, c1, n1, m1, conv1, ssd1, ret1 = mixer_sample(
            proj, lw, consts,
            (state_mlstm_c[l], state_mlstm_n[l], m_pad, state_ssd_conv[l],
             state_ssd[l].reshape(bs, SSD_WIDTH, SSD_STATE), state_ret[l]))
        st_s = (c1, n1, m1[:, :ML_HEADS], conv1, ssd1.reshape(bs, SSD_HEADS, SSD_HEADDIM, SSD_STATE), ret1)
        xs = norm_matmul(y, ones_row, lw["w_out"], residual=xs, norm=False)
        q = norm_matmul(xs, g_ca, wq)
        o = ca_sample(q, cache_mem_k[l].reshape(bs, MEM_LEN, d), cache_mem_v[l].reshape(bs, MEM_LEN, d))
        xs = norm_matmul(o, ones_row[:, :d], wo, residual=xs, norm=False)
        xs = run_ffn(xs, bs, bs)

        for lst, a in zip(new_p, st_p):
            lst.append(a)
        for lst, a in zip(new_s, st_s):
            lst.append(a)

    shape5 = (DEPTH, bp, MEM_LEN, CA_HEADS, CA_DH)
    return (xp, xs.reshape(bs, 1, d),
            jnp.stack(new_p[0]), jnp.stack(new_p[1]), jnp.stack(new_p[2]), jnp.stack(new_p[3]),
            jnp.stack(new_p[4]), jnp.stack(new_p[5]), mem_k_p.reshape(shape5), mem_v_p.reshape(shape5),
            jnp.stack(new_s[0]), jnp.stack(new_s[1]), jnp.stack(new_s[2]), jnp.stack(new_s[3]),
            jnp.stack(new_s[4]), jnp.stack(new_s[5]))
```

```python
import functools
import math

import numpy as np
import jax
import jax.numpy as jnp
from jax import lax
from jax.experimental import pallas as pl
from jax.experimental.pallas import tpu as pltpu

F32 = jnp.float32
BF16 = jnp.bfloat16
HIGHEST = lax.Precision.HIGHEST

D_MODEL = 1024
DEPTH = 2
PAST_LEN = 16384
D_MIX = 2 * D_MODEL
ML_WIDTH = 3 * D_MIX // 8
ML_HEADS = 6
ML_DH = ML_WIDTH // ML_HEADS
SSD_WIDTH = 3 * D_MIX // 8
SSD_HEADDIM = 64
SSD_HEADS = SSD_WIDTH // SSD_HEADDIM
SSD_STATE = 128
SSD_GROUPS = 2
SSD_GROUP_WIDTH = SSD_WIDTH // SSD_GROUPS
SSD_CONV_DIM = SSD_WIDTH + 2 * SSD_GROUPS * SSD_STATE
CONV_W = 4
RET_WIDTH = D_MIX // 4
RET_HEADS = 4
RET_DH = RET_WIDTH // RET_HEADS
ROPE_BASE = 10000.0
CHUNK = 128
MEM_LEN = 256
CA_HEADS = 4
CA_DH = D_MODEL // CA_HEADS
D_FF = 11 * D_MODEL // 4
N_EXPERTS = 8
D_EXPERT = D_FF // 2
EPS = 1e-6

LANES = 128

OFF_QML = 0
OFF_KML = OFF_QML + ML_WIDTH
OFF_VML = OFF_KML + ML_WIDTH
OFF_OML = OFF_VML + ML_WIDTH
OFF_Z = OFF_OML + ML_WIDTH
OFF_XBC = OFF_Z + SSD_WIDTH
OFF_QR = OFF_XBC + SSD_CONV_DIM
OFF_KR = OFF_QR + RET_WIDTH
OFF_VR = OFF_KR + RET_WIDTH
OFF_GR = OFF_VR + RET_WIDTH
OFF_SMALL = OFF_GR + RET_WIDTH
D_PROJ = OFF_SMALL + LANES
LANE_I = 0
LANE_F = ML_HEADS
LANE_DT = 2 * ML_HEADS

VMEM_LIMIT = 56 * 1024 * 1024


def _cparams(sem):
    return pltpu.CompilerParams(dimension_semantics=sem, vmem_limit_bytes=VMEM_LIMIT)


def _const_spec(shape):
    nd = len(shape)
    return pl.BlockSpec(shape, lambda *_: (0,) * nd, pipeline_mode=pl.Buffered(1))


def _rms(x, g):
    return x * lax.rsqrt(jnp.mean(x * x, axis=-1, keepdims=True) + EPS) * g


def _sigmoid(x):
    return 1.0 / (1.0 + jnp.exp(-x))


def _silu(x):
    return x * _sigmoid(x)


def _dot(a, b):
    return jnp.dot(a, b, preferred_element_type=F32)


def _dot_nt(a, b):
    return lax.dot_general(a, b, (((1,), (1,)), ((), ())), preferred_element_type=F32)


def _dot_exact(a, b):
    return jnp.dot(a, b, preferred_element_type=F32, precision=HIGHEST)


def _norm_matmul_kernel(*refs, norm, residual):
    x_ref, g_ref, w_ref = refs[:3]
    o_ref = refs[-1]
    x = x_ref[...].astype(F32)
    if norm:
        x = _rms(x, g_ref[...])
    acc = _dot(x.astype(BF16), w_ref[...])
    if residual:
        acc = acc + refs[3][...]
    o_ref[...] = acc


def norm_matmul(x, g, w, residual=None, *, norm=True, tm=None, tn=None):
    m, k = x.shape
    n = w.shape[1]
    tm = tm or min(m, 512)
    tn = tn or min(n, 1024)
    in_specs = [pl.BlockSpec((tm, k), lambda j, i: (i, 0)),
                pl.BlockSpec((1, k), lambda j, i: (0, 0)),
                pl.BlockSpec((k, tn), lambda j, i: (0, j))]
    args = [x, g, w]
    if residual is not None:
        in_specs.append(pl.BlockSpec((tm, tn), lambda j, i: (i, j)))
        args.append(residual)
    return pl.pallas_call(
        functools.partial(_norm_matmul_kernel, norm=norm, residual=residual is not None),
        grid=(pl.cdiv(n, tn), m // tm),
        in_specs=in_specs,
        out_specs=pl.BlockSpec((tm, tn), lambda j, i: (i, j)),
        out_shape=jax.ShapeDtypeStruct((m, n), F32),
        compiler_params=_cparams(("parallel", "parallel")),
        name="norm_matmul",
    )(*args)


def _kv_proj_kernel(x_ref, g_ref, wk_ref, wv_ref, k_ref, v_ref):
    xn = _rms(x_ref[...], g_ref[...]).astype(BF16)
    k_ref[...] = _dot(xn, wk_ref[...])
    v_ref[...] = _dot(xn, wv_ref[...])


def kv_proj(mem2d, g, wk, wv, *, tm=512):
    m, d = mem2d.shape
    depth = wk.shape[0]
    out = jax.ShapeDtypeStruct((depth, m, d), F32)
    return pl.pallas_call(
        _kv_proj_kernel,
        grid=(depth, m // tm),
        in_specs=[pl.BlockSpec((tm, d), lambda l, i: (i, 0)),
                  pl.BlockSpec((None, 1, d), lambda l, i: (l, 0, 0)),
                  pl.BlockSpec((None, d, d), lambda l, i: (l, 0, 0)),
                  pl.BlockSpec((None, d, d), lambda l, i: (l, 0, 0))],
        out_specs=[pl.BlockSpec((None, tm, d), lambda l, i: (l, i, 0)),
                   pl.BlockSpec((None, tm, d), lambda l, i: (l, i, 0))],
        out_shape=[out, out],
        compiler_params=_cparams(("parallel", "parallel")),
        name="kv_proj",
    )(mem2d, g, wk, wv)


def _ca_prompt_kernel(x_ref, g_ref, wq_ref, wo_ref, k_ref, v_ref, o_ref):
    x = x_ref[...]
    xn = _rms(x, g_ref[...]).astype(BF16)
    q = _dot(xn, wq_ref[...])
    kb = k_ref[...].astype(BF16)
    vb = v_ref[...].astype(BF16)
    outs = []
    for h in range(CA_HEADS):
        sl = slice(h * CA_DH, (h + 1) * CA_DH)
        s = _dot_nt(q[:, sl].astype(BF16), kb[:, sl]) * (CA_DH ** -0.5)
        p = jnp.exp(s - jnp.max(s, axis=-1, keepdims=True))
        p = p / jnp.sum(p, axis=-1, keepdims=True)
        outs.append(_dot(p.astype(BF16), vb[:, sl]).astype(BF16))
    o = jnp.concatenate(outs, axis=-1)
    o_ref[...] = x + _dot(o, wo_ref[...])


def ca_prompt(x, g, wq, wo, mem_k, mem_v, layer, *, tm=512):
    b, length, d = x.shape
    mlen = mem_k.shape[2]
    return pl.pallas_call(
        _ca_prompt_kernel,
        grid=(b, length // tm),
        in_specs=[pl.BlockSpec((None, tm, d), lambda i, j: (i, j, 0)),
                  _const_spec((1, d)), _const_spec((d, d)), _const_spec((d, d)),
                  pl.BlockSpec((None, None, mlen, d), lambda i, j: (layer, i, 0, 0)),
                  pl.BlockSpec((None, None, mlen, d), lambda i, j: (layer, i, 0, 0))],
        out_specs=pl.BlockSpec((None, tm, d), lambda i, j: (i, j, 0)),
        out_shape=jax.ShapeDtypeStruct(x.shape, F32),
        compiler_params=_cparams(("parallel", "parallel")),
        name="ca_prompt",
    )(x, g, wq, wo, mem_k, mem_v)


def _ca_sample_kernel(q_ref, k_ref, v_ref, o_ref, *, bt):
    for j in range(bt):
        q = q_ref[j:j + 1, :]
        kq = k_ref[j] * q
        vj = v_ref[j]
        outs = []
        for h in range(CA_HEADS):
            sl = slice(h * CA_DH, (h + 1) * CA_DH)
            s = jnp.sum(kq[:, sl], axis=-1, keepdims=True) * (CA_DH ** -0.5)
            p = jnp.exp(s - jnp.max(s, axis=0, keepdims=True))
            p = p / jnp.sum(p, axis=0, keepdims=True)
            outs.append(jnp.sum(p * vj[:, sl], axis=0, keepdims=True))
        o_ref[j:j + 1, :] = jnp.concatenate(outs, axis=-1)


def ca_sample(q, mem_k, mem_v, layer, *, bt=8):
    b, d = q.shape
    mlen = mem_k.shape[2]
    return pl.pallas_call(
        functools.partial(_ca_sample_kernel, bt=bt),
        grid=(b // bt,),
        in_specs=[pl.BlockSpec((bt, d), lambda i: (i, 0)),
                  pl.BlockSpec((None, bt, mlen, d), lambda i: (layer, i, 0, 0)),
                  pl.BlockSpec((None, bt, mlen, d), lambda i: (layer, i, 0, 0))],
        out_specs=pl.BlockSpec((bt, d), lambda i: (i, 0)),
        out_shape=jax.ShapeDtypeStruct((b, d), F32),
        compiler_params=_cparams(("parallel",)),
        name="ca_sample",
    )(q, mem_k, mem_v)


def _ffn_kernel(x_ref, g_ref, wg_ref, wu_ref, wd_ref, o_ref, a_ref, *, fchunk):
    x = x_ref[...]
    xn = _rms(x, g_ref[...]).astype(BF16)
    dff = wg_ref.shape[1]
    for f0 in range(0, dff, fchunk):
        gate = _dot(xn, wg_ref[:, f0:f0 + fchunk])
        up = _dot(xn, wu_ref[:, f0:f0 + fchunk])
        a_ref[:, f0:f0 + fchunk] = (_silu(gate) * up).astype(BF16)
    o_ref[...] = x + _dot(a_ref[...], wd_ref[...])


def ffn_dense(x, g, wg, wu, wd, *, tm=512):
    m, d = x.shape
    dff = wg.shape[1]
    return pl.pallas_call(
        functools.partial(_ffn_kernel, fchunk=dff // 2),
        grid=(m // tm,),
        in_specs=[pl.BlockSpec((tm, d), lambda i: (i, 0)),
                  _const_spec((1, d)), _const_spec((d, dff)), _const_spec((d, dff)), _const_spec((dff, d))],
        out_specs=pl.BlockSpec((tm, d), lambda i: (i, 0)),
        out_shape=jax.ShapeDtypeStruct((m, d), F32),
        scratch_shapes=[pltpu.VMEM((tm, dff), BF16)],
        compiler_params=_cparams(("parallel",)),
        name="ffn_dense",
    )(x, g, wg, wu, wd)


def _moe_kernel(x_ref, g_ref, wrt_ref, wg_ref, wu_ref, wd_ref, gf_ref, o_ref,
                xn_ref, gate_t_ref, pos_t_ref, gate_c_ref, pos_c_ref, *, tm, sub, final_norm):
    e = pl.program_id(1)
    ne = pl.num_programs(1)
    epad = gate_t_ref.shape[0]

    @pl.when(e == 0)
    def _route():
        x = x_ref[...]
        xn = _rms(x, g_ref[...])
        xn_ref[...] = xn.astype(BF16)
        x_hi = xn.astype(BF16)
        x_lo = (xn - x_hi.astype(F32)).astype(BF16)
        w = wrt_ref[...]
        w_hi = w.astype(BF16)
        w_lo = (w - w_hi.astype(F32)).astype(BF16)
        logits = _dot_nt(w_hi, x_hi) + (_dot_nt(w_hi, x_lo) + _dot_nt(w_lo, x_hi))
        row = lax.broadcasted_iota(jnp.int32, (epad, tm), 0)
        logits = jnp.where(row < N_EXPERTS, logits, -jnp.inf)
        m1 = jnp.max(logits, axis=0, keepdims=True)
        i1 = jnp.min(jnp.where(logits == m1, row, epad), axis=0, keepdims=True)
        rest = jnp.where(row == i1, -jnp.inf, logits)
        m2 = jnp.max(rest, axis=0, keepdims=True)
        i2 = jnp.min(jnp.where(rest == m2, row, epad), axis=0, keepdims=True)
        e2 = jnp.exp(m2 - m1)
        den = 1.0 + e2
        gate_t = jnp.where(row == i1, 1.0 / den, 0.0) + jnp.where(row == i2, e2 / den, 0.0)
        sel_t = jnp.where((row == i1) | (row == i2), 1.0, 0.0)
        r_i = lax.broadcasted_iota(jnp.int32, (tm, tm), 0)
        c_i = lax.broadcasted_iota(jnp.int32, (tm, tm), 1)
        upper = jnp.where(r_i < c_i, 1.0, 0.0).astype(BF16)
        pos_t = _dot(sel_t.astype(BF16), upper)
        pos_t = jnp.where(sel_t > 0, pos_t, -1.0)
        gate_t_ref[...] = gate_t
        pos_t_ref[...] = pos_t
        pad = jnp.zeros((LANES - epad, tm), F32)
        gate_c_ref[...] = jnp.concatenate([gate_t, pad], axis=0).T
        pos_c_ref[...] = jnp.concatenate([pos_t, pad - 1.0], axis=0).T
        o_ref[...] = x

    pos_row = pos_t_ref[pl.ds(e, 1), :]
    lane = lax.broadcasted_iota(jnp.int32, (tm, LANES), 1)
    pos_col = jnp.sum(jnp.where(lane == e, pos_c_ref[...], 0.0), axis=-1, keepdims=True)
    gate_col = jnp.sum(jnp.where(lane == e, gate_c_ref[...], 0.0), axis=-1, keepdims=True)
    count = jnp.sum(jnp.where(pos_row >= 0, 1.0, 0.0)).astype(jnp.int32)
    nsub = (count + (sub - 1)) // sub

    def sub_body(j, carry):
        base = (j * sub).astype(F32)
        slot_r = lax.broadcasted_iota(jnp.int32, (sub, tm), 0).astype(F32) + base
        gather = jnp.where(pos_row == slot_r, 1.0, 0.0).astype(BF16)
        slot_c = lax.broadcasted_iota(jnp.int32, (tm, sub), 1).astype(F32) + base
        scatter = jnp.where(pos_col == slot_c, 1.0, 0.0).astype(BF16)
        xg = _dot(gather, xn_ref[...]).astype(BF16)
        act = (_silu(_dot(xg, wg_ref[...])) * _dot(xg, wu_ref[...])).astype(BF16)
        ye = _dot(act, wd_ref[...]).astype(BF16)
        o_ref[...] += gate_col * _dot(scatter, ye)
        return carry

    lax.fori_loop(0, nsub, sub_body, 0)

    if final_norm:
        @pl.when(e == ne - 1)
        def _final():
            o_ref[...] = _rms(o_ref[...], gf_ref[...])


def moe_ffn(x, g, w_router_t, wg, wu, wd, g_final, *, tm, sub, final_norm):
    m, d = x.shape
    ne, _, de = wg.shape
    epad = w_router_t.shape[0]
    return pl.pallas_call(
        functools.partial(_moe_kernel, tm=tm, sub=sub, final_norm=final_norm),
        grid=(m // tm, ne),
        in_specs=[pl.BlockSpec((tm, d), lambda i, e: (i, 0)),
                  pl.BlockSpec((1, d), lambda i, e: (0, 0)),
                  pl.BlockSpec((epad, d), lambda i, e: (0, 0)),
                  pl.BlockSpec((None, d, de), lambda i, e: (e, 0, 0)),
                  pl.BlockSpec((None, d, de), lambda i, e: (e, 0, 0)),
                  pl.BlockSpec((None, de, d), lambda i, e: (e, 0, 0)),
                  pl.BlockSpec((1, d), lambda i, e: (0, 0))],
        out_specs=pl.BlockSpec((tm, d), lambda i, e: (i, 0)),
        out_shape=jax.ShapeDtypeStruct((m, d), F32),
        scratch_shapes=[pltpu.VMEM((tm, d), BF16),
                        pltpu.VMEM((epad, tm), F32), pltpu.VMEM((epad, tm), F32),
                        pltpu.VMEM((tm, LANES), F32), pltpu.VMEM((tm, LANES), F32)],
        compiler_params=_cparams(("parallel", "arbitrary")),
        name="moe_ffn",
    )(x, g, w_router_t, wg, wu, wd, g_final)


def _retention_constants(c):
    log_g = np.log1p(-np.exp2(-5.0 - np.arange(RET_HEADS, dtype=np.float32))).astype(np.float32)
    j = np.arange(c, dtype=np.float32)
    rel = j[:, None] - j[None, :]
    w_intra = np.where(rel >= 0, np.exp(rel[None] * log_g[:, None, None]), 0.0).astype(np.float32)
    w_inter = np.exp((j[None, :] + 1.0) * log_g[:, None]).astype(np.float32)
    w_end = np.exp((c - 1.0 - j[None, :]) * log_g[:, None]).astype(np.float32)
    decay = np.exp(c * log_g).astype(np.float32)
    cols = np.zeros((c, LANES), np.float32)
    cols[:, :RET_HEADS] = w_inter.T
    cols[:, RET_HEADS:2 * RET_HEADS] = w_end.T
    return w_intra, cols, [float(d) for d in decay]


def _rotary_tables(pos):
    half = RET_DH // 2
    inv = ROPE_BASE ** (-jnp.arange(half, dtype=F32) / half)
    ang = pos[:, None] * inv[None, :]
    cos, sin = jnp.cos(ang), jnp.sin(ang)
    return jnp.concatenate([cos, cos], axis=-1), jnp.concatenate([-sin, sin], axis=-1)


def _head_expand_matrix():
    e = np.zeros((LANES, SSD_WIDTH), np.float32)
    for h in range(SSD_HEADS):
        e[LANE_DT + h, h * SSD_HEADDIM:(h + 1) * SSD_HEADDIM] = 1.0
    return e


def _softplus_parts(v):
    sp = jnp.log1p(jnp.exp(-jnp.abs(v)))
    return jnp.minimum(v, 0.0) - sp, jnp.maximum(v, 0.0) + sp


def _mixer_prompt_kernel(x_ref, g_ref, win_ref, wout_ref, bias_ref, alog_ref, convw_ref, convb_ref,
                         mlg_ref, ssdd_ref, ssdg_ref, retg_ref, cos_ref, sin_ref, retw_ref, retcol_ref,
                         expand_ref,
                         xo_ref, c_ref, n_ref, m_ref, conv_ref, ssd_ref, ret_ref,
                         proj_s, y_s, u_s, *, tm, ret_decay):
    c = CHUNK
    ti = pl.program_id(1)

    @pl.when(ti == 0)
    def _init():
        c_ref[...] = jnp.zeros_like(c_ref)
        n_ref[...] = jnp.zeros_like(n_ref)
        m_ref[...] = jnp.zeros_like(m_ref)
        ssd_ref[...] = jnp.zeros_like(ssd_ref)
        ret_ref[...] = jnp.zeros_like(ret_ref)
        u_s[0:8, :] = jnp.zeros((8, SSD_CONV_DIM), F32)

    x = x_ref[...]
    xn = _rms(x, g_ref[...]).astype(BF16)
    for off in range(0, D_PROJ, 1024):
        wdt = min(1024, D_PROJ - off)
        proj_s[:, off:off + wdt] = _dot(xn, win_ref[:, off:off + wdt])

    r_i = lax.broadcasted_iota(jnp.int32, (c, c), 0)
    c_i = lax.broadcasted_iota(jnp.int32, (c, c), 1)
    causal = r_i >= c_i
    tril = jnp.where(causal, 1.0, 0.0)
    lane1 = lax.broadcasted_iota(jnp.int32, (1, LANES), 1)
    lane_c = lax.broadcasted_iota(jnp.int32, (c, LANES), 1)
    a_vec = -jnp.exp(alog_ref[...])
    expand = expand_ref[...]
    bias = bias_ref[...]

    def chunk_body(ci, carry):
        r0 = pl.multiple_of(ci * c, c)
        rows = pl.ds(r0, c)
        small = proj_s[rows, OFF_SMALL:OFF_SMALL + LANES] + bias
        logf, dt = _softplus_parts(small)
        is_f = (lane_c >= LANE_F) & (lane_c < LANE_DT)
        is_dt = (lane_c >= LANE_DT) & (lane_c < LANE_DT + SSD_HEADS)
        incr = jnp.where(is_f, logf, jnp.where(is_dt, dt * a_vec, 0.0))
        cs = _dot_exact(tril, incr)
        cs_t = cs.T
        small_t = small.T

        m_vec = m_ref[...]
        m_out = m_vec
        for h in range(ML_HEADS):
            hs = slice(h * ML_DH, (h + 1) * ML_DH)
            b_col = cs[:, LANE_F + h:LANE_F + h + 1]
            b_row = cs_t[LANE_F + h:LANE_F + h + 1, :]
            i_col = small[:, LANE_I + h:LANE_I + h + 1]
            i_row = small_t[LANE_I + h:LANE_I + h + 1, :]
            m_prev = m_vec[:, h:h + 1]
            lw = jnp.where(causal, b_col + (i_row - b_row), -jnp.inf)
            log_a = b_col + m_prev
            m_t = jnp.maximum(log_a, jnp.max(lw, axis=-1, keepdims=True))
            w_intra = jnp.exp(lw - m_t)
            w_inter = jnp.exp(log_a - m_t)
            q = proj_s[rows, OFF_QML + h * ML_DH:OFF_QML + (h + 1) * ML_DH]
            k = proj_s[rows, OFF_KML + h * ML_DH:OFF_KML + (h + 1) * ML_DH] * (ML_DH ** -0.5)
            v = proj_s[rows, OFF_VML + h * ML_DH:OFF_VML + (h + 1) * ML_DH]
            og = proj_s[rows, OFF_OML + h * ML_DH:OFF_OML + (h + 1) * ML_DH]
            qb, kb, vb = q.astype(BF16), k.astype(BF16), v.astype(BF16)
            s = w_intra * _dot_nt(qb, kb)
            cm = c_ref[h]
            n_row = n_ref[h:h + 1, :]
            num = _dot(s.astype(BF16), vb) + w_inter * _dot_nt(qb, cm.astype(BF16))
            den = jnp.sum(s, axis=-1, keepdims=True) + w_inter * jnp.sum(q * n_row, axis=-1, keepdims=True)
            den = jnp.maximum(jnp.abs(den), jnp.exp(-m_t))
            hh = num / den
            m_new = m_t[c - 1:c, :]
            decay = jnp.exp(log_a[c - 1:c, :] - m_new)
            w_end = jnp.exp(b_col[c - 1:c, :] - b_col + i_col - m_new)
            c_ref[h] = decay * cm + _dot((v * w_end).T.astype(BF16), kb)
            n_ref[h:h + 1, :] = decay * n_row + jnp.sum(k * w_end, axis=0, keepdims=True)
            m_out = jnp.where(lane1 == h, m_new, m_out)
            hn = hh * lax.rsqrt(jnp.mean(hh * hh, axis=-1, keepdims=True) + EPS)
            y_s[rows, hs] = (hn * mlg_ref[:, hs] * _sigmoid(og)).astype(BF16)
        m_ref[...] = m_out

        u = proj_s[rows, OFF_XBC:OFF_XBC + SSD_CONV_DIM]
        u_s[8:8 + c, :] = u
        conv = convb_ref[...] + convw_ref[3:4, :] * u
        for jj in range(CONV_W - 1):
            conv = conv + convw_ref[jj:jj + 1, :] * u_s[5 + jj:5 + jj + c, :]
        u_s[0:8, :] = u_s[c:c + 8, :]
        xc = _silu(conv)
        xs = xc[:, :SSD_WIDTH]
        dt_exp = _dot_exact(jnp.where(is_dt, dt, 0.0), expand)
        acum_exp = _dot_exact(jnp.where(is_dt, cs, 0.0), expand)
        xdt = xs * dt_exp
        w_end_exp = jnp.exp(acum_exp[c - 1:c, :] - acum_exp)
        xw = xdt * w_end_exp
        ea_exp = jnp.exp(acum_exp)
        lane_lo = lane_c < SSD_HEADDIM
        hpg = SSD_HEADS // SSD_GROUPS
        y_cols = []
        for g in range(SSD_GROUPS):
            gw = slice(g * SSD_GROUP_WIDTH, (g + 1) * SSD_GROUP_WIDTH)
            bg = xc[:, SSD_WIDTH + g * SSD_STATE:SSD_WIDTH + (g + 1) * SSD_STATE].astype(BF16)
            cg = xc[:, SSD_WIDTH + (SSD_GROUPS + g) * SSD_STATE:
                    SSD_WIDTH + (SSD_GROUPS + g + 1) * SSD_STATE].astype(BF16)
            cb = _dot_nt(cg, bg)
            st = ssd_ref[gw, :]
            inter = _dot_nt(cg, st.astype(BF16)) * ea_exp[:, gw]
            for pr in range(hpg // 2):
                h0 = g * hpg + 2 * pr
                xp = xdt[:, h0 * SSD_HEADDIM:(h0 + 2) * SSD_HEADDIM].astype(BF16)
                pair = []
                for hidx in (h0, h0 + 1):
                    a_col = cs[:, LANE_DT + hidx:LANE_DT + hidx + 1]
                    a_row = cs_t[LANE_DT + hidx:LANE_DT + hidx + 1, :]
                    dec = jnp.exp(jnp.where(causal, a_col - a_row, -jnp.inf))
                    pair.append(_dot((dec * cb).astype(BF16), xp))
                y_cols.append(jnp.where(lane_lo, pair[0], pair[1]) + inter[:, pr * LANES:(pr + 1) * LANES])
            upd = _dot(xw[:, gw].T.astype(BF16), bg)
            for r in range(hpg):
                hidx = g * hpg + r
                scal = jnp.exp(cs[c - 1:c, LANE_DT + hidx:LANE_DT + hidx + 1])
                hr = slice(hidx * SSD_HEADDIM, (hidx + 1) * SSD_HEADDIM)
                ssd_ref[hr, :] = scal * ssd_ref[hr, :] + upd[r * SSD_HEADDIM:(r + 1) * SSD_HEADDIM, :]
        ys = jnp.concatenate(y_cols, axis=-1) + ssdd_ref[...] * xs
        yz = ys * _silu(proj_s[rows, OFF_Z:OFF_Z + SSD_WIDTH])
        for g in range(SSD_GROUPS):
            gw = slice(g * SSD_GROUP_WIDTH, (g + 1) * SSD_GROUP_WIDTH)
            seg = yz[:, gw]
            seg = seg * lax.rsqrt(jnp.mean(seg * seg, axis=-1, keepdims=True) + EPS) * ssdg_ref[:, gw]
            y_s[rows, ML_WIDTH + g * SSD_GROUP_WIDTH:ML_WIDTH + (g + 1) * SSD_GROUP_WIDTH] = seg.astype(BF16)

        cosb = cos_ref[rows, :]
        sinb = sin_ref[rows, :]
        retcol = retcol_ref[...]
        for h in range(RET_HEADS):
            hs = slice(h * RET_DH, (h + 1) * RET_DH)
            q = proj_s[rows, OFF_QR + h * RET_DH:OFF_QR + (h + 1) * RET_DH]
            k = proj_s[rows, OFF_KR + h * RET_DH:OFF_KR + (h + 1) * RET_DH]
            v = proj_s[rows, OFF_VR + h * RET_DH:OFF_VR + (h + 1) * RET_DH]
            gr = proj_s[rows, OFF_GR + h * RET_DH:OFF_GR + (h + 1) * RET_DH]
            qr = q * cosb + pltpu.roll(q, RET_DH // 2, 1) * sinb
            kr = (k * cosb + pltpu.roll(k, RET_DH // 2, 1) * sinb) * (RET_DH ** -0.5)
            qb, kb, vb = qr.astype(BF16), kr.astype(BF16), v.astype(BF16)
            att = _dot_nt(qb, kb) * retw_ref[h]
            sm = ret_ref[h]
            o = _dot(att.astype(BF16), vb) + _dot(qb, sm.astype(BF16)) * retcol[:, h:h + 1]
            kw = kr * retcol[:, RET_HEADS + h:RET_HEADS + h + 1]
            ret_ref[h] = ret_decay[h] * sm + _dot(kw.T.astype(BF16), vb)
            on = o * lax.rsqrt(jnp.mean(o * o, axis=-1, keepdims=True) + EPS)
            y_s[rows, ML_WIDTH + SSD_WIDTH + h * RET_DH:ML_WIDTH + SSD_WIDTH + (h + 1) * RET_DH] = (
                on * retg_ref[:, hs] * _silu(gr)).astype(BF16)
        return carry

    lax.fori_loop(0, tm // c, chunk_body, 0)
    xo_ref[...] = x + _dot(y_s[...], wout_ref[...])
    conv_ref[...] = u_s[5:8, :]


def mixer_prompt(x, lw, consts, *, tm):
    b, length, d = x.shape
    c = CHUNK
    row = lambda w: _const_spec((1, w))
    state_spec = lambda *shape: pl.BlockSpec((None,) + shape, lambda i, j: (i,) + (0,) * len(shape))
    out_shapes = [jax.ShapeDtypeStruct(x.shape, F32),
                  jax.ShapeDtypeStruct((b, ML_HEADS, ML_DH, ML_DH), F32),
                  jax.ShapeDtypeStruct((b, ML_HEADS, ML_DH), F32),
                  jax.ShapeDtypeStruct((b, 1, LANES), F32),
                  jax.ShapeDtypeStruct((b, CONV_W - 1, SSD_CONV_DIM), F32),
                  jax.ShapeDtypeStruct((b, SSD_WIDTH, SSD_STATE), F32),
                  jax.ShapeDtypeStruct((b, RET_HEADS, RET_DH, RET_DH), F32)]
    return pl.pallas_call(
        functools.partial(_mixer_prompt_kernel, tm=tm, ret_decay=consts["ret_decay"]),
        grid=(b, length // tm),
        in_specs=[pl.BlockSpec((None, tm, d), lambda i, j: (i, j, 0)),
                  row(d), _const_spec((d, D_PROJ)), _const_spec((D_MIX, d)),
                  row(LANES), row(LANES), _const_spec((CONV_W, SSD_CONV_DIM)), row(SSD_CONV_DIM),
                  row(ML_WIDTH), row(SSD_WIDTH), row(SSD_WIDTH), row(RET_WIDTH),
                  pl.BlockSpec((tm, LANES), lambda i, j: (j, 0)),
                  pl.BlockSpec((tm, LANES), lambda i, j: (j, 0)),
                  _const_spec((RET_HEADS, c, c)), _const_spec((c, LANES)), _const_spec((LANES, SSD_WIDTH))],
        out_specs=[pl.BlockSpec((None, tm, d), lambda i, j: (i, j, 0)),
                   state_spec(ML_HEADS, ML_DH, ML_DH), state_spec(ML_HEADS, ML_DH), state_spec(1, LANES),
                   state_spec(CONV_W - 1, SSD_CONV_DIM), state_spec(SSD_WIDTH, SSD_STATE),
                   state_spec(RET_HEADS, RET_DH, RET_DH)],
        out_shape=out_shapes,
        scratch_shapes=[pltpu.VMEM((tm, D_PROJ), F32), pltpu.VMEM((tm, D_MIX), BF16),
                        pltpu.VMEM((c + 8, SSD_CONV_DIM), F32)],
        compiler_params=_cparams(("parallel", "arbitrary")),
        name="mixer_prompt",
    )(x, lw["norm_mix"], lw["w_in"], lw["w_out"], lw["bias"], lw["alog"], lw["conv_w"], lw["conv_b"],
      lw["ml_norm"], lw["ssd_d"], lw["ssd_norm"], lw["ret_norm"], consts["cos_p"], consts["sin_p"],
      consts["ret_w"], consts["ret_cols"], consts["expand"])


N_SAMPLE_STATES = 6


def _mixer_sample_kernel(*refs, bt, ret_gamma, n_alias):
    (proj_ref, small_ref, bias_ref, alog_ref, convw_ref, convb_ref, mlg_ref, ssdd_ref,
     ssdg_ref, retg_ref, cos_ref, sin_ref, expand_ref,
     c_in, n_in, m_in, conv_in, ssd_in, ret_in) = refs[:19]
    (y_ref, c_out, n_out, m_out, conv_out, ssd_out, ret_out,
     wi_s, wt_s, en_s, dtx_s, eax_s) = refs[19 + n_alias:]
    small = small_ref[...] + bias_ref[...]
    logf, dt = _softplus_parts(small)
    lane_b = lax.broadcasted_iota(jnp.int32, (bt, LANES), 1)
    is_dt = (lane_b >= LANE_DT) & (lane_b < LANE_DT + SSD_HEADS)
    log_a = pltpu.roll(logf, LANES - LANE_F, 1) + m_in[...]
    m_t = jnp.maximum(log_a, small)
    m_out[...] = m_t
    wi_s[...] = jnp.exp(small - m_t)
    wt_s[...] = jnp.exp(log_a - m_t)
    en_s[...] = jnp.exp(-m_t)
    a_vec = -jnp.exp(alog_ref[...])
    expand = expand_ref[...]
    dtx_s[...] = _dot_exact(jnp.where(is_dt, dt, 0.0), expand)
    eax_s[...] = jnp.exp(_dot_exact(jnp.where(is_dt, dt * a_vec, 0.0), expand))

    r_i = lax.broadcasted_iota(jnp.int32, (LANES, LANES), 0)
    c_i = lax.broadcasted_iota(jnp.int32, (LANES, LANES), 1)
    eye = jnp.where(r_i == c_i, 1.0, 0.0)

    def to_col(rowv):
        return jnp.sum(eye * rowv, axis=-1, keepdims=True)

    def to_row(colv):
        return jnp.sum(eye * colv, axis=0, keepdims=True)

    cosb = cos_ref[...]
    sinb = sin_ref[...]
    hpg = SSD_HEADS // SSD_GROUPS

    sub_l = lax.broadcasted_iota(jnp.int32, (bt, LANES), 0)
    sub_w = lax.broadcasted_iota(jnp.int32, (bt, SSD_WIDTH), 0)

    def body(j, carry):
        pick_l = lambda ref: jnp.sum(jnp.where(sub_l == j, ref[...], 0.0), axis=0, keepdims=True)
        pick_w = lambda ref: jnp.sum(jnp.where(sub_w == j, ref[...], 0.0), axis=0, keepdims=True)
        wi = pick_l(wi_s)
        wt = pick_l(wt_s)
        en = pick_l(en_s)
        for h in range(ML_HEADS):
            hs = slice(h * ML_DH, (h + 1) * ML_DH)
            q = proj_ref[j, :,OFF_QML + h * ML_DH:OFF_QML + (h + 1) * ML_DH]
            k = proj_ref[j, :,OFF_KML + h * ML_DH:OFF_KML + (h + 1) * ML_DH] * (ML_DH ** -0.5)
            v = proj_ref[j, :,OFF_VML + h * ML_DH:OFF_VML + (h + 1) * ML_DH]
            og = proj_ref[j, :,OFF_OML + h * ML_DH:OFF_OML + (h + 1) * ML_DH]
            w_in_h = wi[:, h:h + 1]
            w_tr_h = wt[:, h:h + 1]
            c_new = w_tr_h * c_in[j, h] + w_in_h * (to_col(v) * k)
            n_new = w_tr_h * n_in[j, h:h + 1, :] + w_in_h * k
            c_out[j, h] = c_new
            n_out[j, h:h + 1, :] = n_new
            num = to_row(jnp.sum(c_new * q, axis=-1, keepdims=True))
            den = jnp.sum(n_new * q, axis=-1, keepdims=True)
            den = jnp.maximum(jnp.abs(den), en[:, h:h + 1])
            hh = num / den
            hn = hh * lax.rsqrt(jnp.mean(hh * hh, axis=-1, keepdims=True) + EPS)
            y_ref[j, :,hs] = hn * mlg_ref[:, hs] * _sigmoid(og)
        u = proj_ref[j, :,OFF_XBC:OFF_XBC + SSD_CONV_DIM]
        prev = conv_in[j]
        conv = convb_ref[...] + convw_ref[CONV_W - 1:CONV_W, :] * u
        for jj in range(CONV_W - 1):
            conv = conv + convw_ref[jj:jj + 1, :] * prev[jj:jj + 1, :]
        conv_out[j, 0:CONV_W - 2, :] = prev[1:CONV_W - 1, :]
        conv_out[j, CONV_W - 2:CONV_W - 1, :] = u
        xc = _silu(conv)
        xs = xc[:, :SSD_WIDTH]
        xdt = xs * pick_w(dtx_s)
        ea = pick_w(eax_s)
        y_cols = []
        for g in range(SSD_GROUPS):
            bg = xc[:, SSD_WIDTH + g * SSD_STATE:SSD_WIDTH + (g + 1) * SSD_STATE]
            cg = xc[:, SSD_WIDTH + (SSD_GROUPS + g) * SSD_STATE:SSD_WIDTH + (SSD_GROUPS + g + 1) * SSD_STATE]
            for pr in range(hpg // 2):
                h0 = g * hpg + 2 * pr
                pw = slice(h0 * SSD_HEADDIM, (h0 + 2) * SSD_HEADDIM)
                st_new = to_col(ea[:, pw]) * ssd_in[j, pw, :] + to_col(xdt[:, pw]) * bg
                ssd_out[j, pw, :] = st_new
                y_cols.append(to_row(jnp.sum(st_new * cg, axis=-1, keepdims=True)))
        ys = jnp.concatenate(y_cols, axis=-1) + ssdd_ref[...] * xs
        yz = ys * _silu(proj_ref[j, :,OFF_Z:OFF_Z + SSD_WIDTH])
        for g in range(SSD_GROUPS):
            gw = slice(g * SSD_GROUP_WIDTH, (g + 1) * SSD_GROUP_WIDTH)
            seg = yz[:, gw]
            y_ref[j, :,ML_WIDTH + g * SSD_GROUP_WIDTH:ML_WIDTH + (g + 1) * SSD_GROUP_WIDTH] = (
                seg * lax.rsqrt(jnp.mean(seg * seg, axis=-1, keepdims=True) + EPS) * ssdg_ref[:, gw])
        for h in range(RET_HEADS):
            hs = slice(h * RET_DH, (h + 1) * RET_DH)
            q = proj_ref[j, :,OFF_QR + h * RET_DH:OFF_QR + (h + 1) * RET_DH]
            k = proj_ref[j, :,OFF_KR + h * RET_DH:OFF_KR + (h + 1) * RET_DH]
            v = proj_ref[j, :,OFF_VR + h * RET_DH:OFF_VR + (h + 1) * RET_DH]
            gr = proj_ref[j, :,OFF_GR + h * RET_DH:OFF_GR + (h + 1) * RET_DH]
            qr = q * cosb + pltpu.roll(q, RET_DH // 2, 1) * sinb
            kr = (k * cosb + pltpu.roll(k, RET_DH // 2, 1) * sinb) * (RET_DH ** -0.5)
            s_new = ret_gamma[h] * ret_in[j, h] + to_col(kr) * v
            ret_out[j, h] = s_new
            o = jnp.sum(s_new * to_col(qr), axis=0, keepdims=True)
            on = o * lax.rsqrt(jnp.mean(o * o, axis=-1, keepdims=True) + EPS)
            y_ref[j, :,ML_WIDTH + SSD_WIDTH + h * RET_DH:ML_WIDTH + SSD_WIDTH + (h + 1) * RET_DH] = (
                on * retg_ref[:, hs] * _silu(gr))
        return carry

    lax.fori_loop(0, bt, body, 0)


def mixer_sample(proj, lw, consts, states, layer, prev_out=None, *, bt=8):
    b = proj.shape[0]
    row = lambda w: pl.BlockSpec((1, w), lambda i: (0, 0))
    blk = lambda *shape: pl.BlockSpec((bt,) + shape, lambda i: (i,) + (0,) * len(shape))
    lblk = lambda *shape: pl.BlockSpec((None, bt) + shape, lambda i: (layer, i) + (0,) * len(shape))
    state_specs = [lblk(ML_HEADS, ML_DH, ML_DH), lblk(ML_HEADS, ML_DH), lblk(LANES),
                   lblk(CONV_W - 1, SSD_CONV_DIM), lblk(SSD_WIDTH, SSD_STATE), lblk(RET_HEADS, RET_DH, RET_DH)]
    state_shapes = [jax.ShapeDtypeStruct(s.shape, F32) for s in states]
    proj3 = proj.reshape(b, 1, D_PROJ)
    small = proj[:, OFF_SMALL:OFF_SMALL + LANES]
    n_fixed = 13 + N_SAMPLE_STATES
    alias_args = list(prev_out) if prev_out is not None else []
    aliases = {n_fixed + k: 1 + k for k in range(len(alias_args))}
    outs = pl.pallas_call(
        functools.partial(_mixer_sample_kernel, bt=bt, ret_gamma=consts["ret_gamma"], n_alias=len(alias_args)),
        grid=(b // bt,),
        in_specs=[blk(1, D_PROJ), blk(LANES), row(LANES), row(LANES),
                  pl.BlockSpec((CONV_W, SSD_CONV_DIM), lambda i: (0, 0)), row(SSD_CONV_DIM),
                  row(ML_WIDTH), row(SSD_WIDTH), row(SSD_WIDTH), row(RET_WIDTH), row(LANES), row(LANES),
                  pl.BlockSpec((LANES, SSD_WIDTH), lambda i: (0, 0))] + state_specs
                 + [pl.BlockSpec(memory_space=pl.ANY)] * len(alias_args),
        out_specs=[blk(1, D_MIX)] + state_specs,
        out_shape=[jax.ShapeDtypeStruct((b, 1, D_MIX), F32)] + state_shapes,
        input_output_aliases=aliases,
        scratch_shapes=[pltpu.VMEM((bt, LANES), F32)] * 3 + [pltpu.VMEM((bt, SSD_WIDTH), F32)] * 2,
        compiler_params=_cparams(("parallel",)),
        name="mixer_sample",
    )(proj3, small, lw["bias"], lw["alog"], lw["conv_w"], lw["conv_b"], lw["ml_norm"], lw["ssd_d"],
      lw["ssd_norm"], lw["ret_norm"], consts["cos_s"], consts["sin_s"], consts["expand"],
      *states, *alias_args)
    return outs[0].reshape(b, D_MIX), list(outs[1:])


def _pack_in_proj(w):
    sizes = [ML_WIDTH] * 4 + [ML_HEADS, ML_HEADS, SSD_WIDTH, SSD_CONV_DIM, SSD_HEADS] + [RET_WIDTH] * 4
    offs = np.concatenate([[0], np.cumsum(sizes)])
    seg = lambda i: w[:, int(offs[i]):int(offs[i + 1])]
    small = jnp.concatenate([seg(4), seg(5), seg(8)], axis=1)
    small = jnp.pad(small, ((0, 0), (0, LANES - small.shape[1])))
    cols = [seg(0), seg(1), seg(2), seg(3), seg(6), seg(7), seg(9), seg(10), seg(11), seg(12), small]
    return jnp.concatenate(cols, axis=1).astype(BF16)


def _lane_row(parts):
    v = jnp.concatenate([p.astype(F32) for p in parts])
    return jnp.pad(v, (0, LANES - v.shape[0]))[None, :]


def kernel(x_prompt, x_sample, state_mlstm_c, state_mlstm_n, state_mlstm_m, state_ssd_conv, state_ssd, state_ret,
           cache_mem_k, cache_mem_v, mem_prompt,
           norm_mix, w_in, ml_i_bias, ml_f_bias, ml_norm, ssd_conv_w, ssd_conv_b, ssd_dt_bias, ssd_a_log, ssd_d,
           ssd_norm, ret_norm, w_out, norm_ca, norm_mem, w_ca_q, w_ca_k, w_ca_v, w_ca_o, norm_ffn,
           ffn_w_gate, ffn_w_up, ffn_w_down, moe_w_router, moe_w_gate, moe_w_up, moe_w_down, norm_final):
    bp, seq, d = x_prompt.shape
    bs = x_sample.shape[0]
    assert x_sample.shape[1] == 1 and seq % CHUNK == 0 and DEPTH % 2 == 0

    ret_w, ret_cols, ret_decay = _retention_constants(CHUNK)
    _, _, ret_gamma = _retention_constants(1)
    cos_p, sin_p = _rotary_tables(jnp.arange(seq, dtype=F32))
    cos_s, sin_s = _rotary_tables(PAST_LEN + jnp.arange(1, dtype=F32))
    consts = {"ret_w": jnp.asarray(ret_w), "ret_cols": jnp.asarray(ret_cols), "ret_decay": ret_decay,
              "ret_gamma": ret_gamma, "cos_p": cos_p, "sin_p": sin_p, "cos_s": cos_s, "sin_s": sin_s,
              "expand": jnp.asarray(_head_expand_matrix())}
    zeros12 = jnp.zeros((2 * ML_HEADS,), F32)
    ones_row = jnp.ones((1, D_MIX), F32)

    mem2d = mem_prompt.reshape(bp * MEM_LEN, d)
    mem_k_p, mem_v_p = kv_proj(mem2d, norm_mem[:, None, :], w_ca_k.astype(BF16), w_ca_v.astype(BF16))

    mem_k_p = mem_k_p.reshape(DEPTH, bp, MEM_LEN, d)
    mem_v_p = mem_v_p.reshape(DEPTH, bp, MEM_LEN, d)
    cache_k = cache_mem_k.reshape(DEPTH, bs, MEM_LEN, d)
    cache_v = cache_mem_v.reshape(DEPTH, bs, MEM_LEN, d)
    sample_states = (state_mlstm_c, state_mlstm_n,
                     jnp.pad(state_mlstm_m, ((0, 0), (0, 0), (0, LANES - ML_HEADS))), state_ssd_conv,
                     state_ssd.reshape(DEPTH, bs, SSD_WIDTH, SSD_STATE), state_ret)
    st_s = None

    xp = x_prompt
    xs = x_sample.reshape(bs, d)
    new_p = [[] for _ in range(6)]
    for l in range(DEPTH):
        lw = {"norm_mix": norm_mix[l][None, :], "w_in": _pack_in_proj(w_in[l]), "w_out": w_out[l].astype(BF16),
              "bias": _lane_row([ml_i_bias[l], ml_f_bias[l], ssd_dt_bias[l]]),
              "alog": _lane_row([zeros12, ssd_a_log[l]]),
              "conv_w": ssd_conv_w[l], "conv_b": ssd_conv_b[l][None, :], "ml_norm": ml_norm[l][None, :],
              "ssd_d": jnp.repeat(ssd_d[l], SSD_HEADDIM)[None, :], "ssd_norm": ssd_norm[l][None, :],
              "ret_norm": ret_norm[l][None, :]}
        wq, wo = w_ca_q[l].astype(BF16), w_ca_o[l].astype(BF16)
        g_ca, g_ffn = norm_ca[l][None, :], norm_ffn[l][None, :]
        last = l == DEPTH - 1
        j = l // 2
        if l % 2 == 0:
            ffn_w = (ffn_w_gate[j].astype(BF16), ffn_w_up[j].astype(BF16), ffn_w_down[j].astype(BF16))
        else:
            wrt = jnp.pad(moe_w_router[j].T, ((0, 2 * N_EXPERTS - N_EXPERTS), (0, 0)))
            ffn_w = (wrt, moe_w_gate[j].astype(BF16), moe_w_up[j].astype(BF16), moe_w_down[j].astype(BF16))

        def run_ffn(x2d, tm, sub):
            if l % 2 == 0:
                return ffn_dense(x2d, g_ffn, *ffn_w, tm=tm)
            return moe_ffn(x2d, g_ffn, *ffn_w, norm_final[None, :], tm=tm, sub=sub, final_norm=last)

        xp, c1, n1, m1, conv1, ssd1, ret1 = mixer_prompt(xp, lw, consts, tm=256)
        st_p = (c1, n1, m1[:, 0, :ML_HEADS], conv1,
                ssd1.reshape(bp, SSD_HEADS, SSD_HEADDIM, SSD_STATE), ret1)
        xp = ca_prompt(xp, g_ca, wq, wo, mem_k_p, mem_v_p, l)
        xp = run_ffn(xp.reshape(bp * seq, d), 1024 if l % 2 else 512, 256).reshape(bp, seq, d)

        proj = norm_matmul(xs, lw["norm_mix"], lw["w_in"], tn=D_PROJ // 3)
        y, st_s = mixer_sample(proj, lw, consts, sample_states, l, st_s)
        xs = norm_matmul(y, ones_row, lw["w_out"], residual=xs, norm=False)
        q = norm_matmul(xs, g_ca, wq)
        o = ca_sample(q, cache_k, cache_v, l)
        xs = norm_matmul(o, ones_row[:, :d], wo, residual=xs, norm=False)
        xs = run_ffn(xs, bs, bs)

        for lst, a in zip(new_p, st_p):
            lst.append(a)

    shape5 = (DEPTH, bp, MEM_LEN, CA_HEADS, CA_DH)
    c_s, n_s, m_s, conv_s, ssd_s, ret_s = st_s
    return (xp, xs.reshape(bs, 1, d),
            jnp.stack(new_p[0]), jnp.stack(new_p[1]), jnp.stack(new_p[2]), jnp.stack(new_p[3]),
            jnp.stack(new_p[4]), jnp.stack(new_p[5]), mem_k_p.reshape(shape5), mem_v_p.reshape(shape5),
            c_s, n_s, m_s[:, :, :ML_HEADS], conv_s,
            ssd_s.reshape(DEPTH, bs, SSD_HEADS, SSD_HEADDIM, SSD_STATE), ret_s)
```

```python
import functools
import math

import numpy as np
import jax
import jax.numpy as jnp
from jax import lax
from jax.experimental import pallas as pl
from jax.experimental.pallas import tpu as pltpu

F32 = jnp.float32
BF16 = jnp.bfloat16
HIGHEST = lax.Precision.HIGHEST

D_MODEL = 1024
DEPTH = 2
PAST_LEN = 16384
D_MIX = 2 * D_MODEL
ML_WIDTH = 3 * D_MIX // 8
ML_HEADS = 6
ML_DH = ML_WIDTH // ML_HEADS
SSD_WIDTH = 3 * D_MIX // 8
SSD_HEADDIM = 64
SSD_HEADS = SSD_WIDTH // SSD_HEADDIM
SSD_STATE = 128
SSD_GROUPS = 2
SSD_GROUP_WIDTH = SSD_WIDTH // SSD_GROUPS
SSD_CONV_DIM = SSD_WIDTH + 2 * SSD_GROUPS * SSD_STATE
CONV_W = 4
RET_WIDTH = D_MIX // 4
RET_HEADS = 4
RET_DH = RET_WIDTH // RET_HEADS
ROPE_BASE = 10000.0
CHUNK = 128
MEM_LEN = 256
CA_HEADS = 4
CA_DH = D_MODEL // CA_HEADS
D_FF = 11 * D_MODEL // 4
N_EXPERTS = 8
D_EXPERT = D_FF // 2
EPS = 1e-6

LANES = 128
BF16_ROWS = 16

OFF_QML = 0
OFF_KML = OFF_QML + ML_WIDTH
OFF_VML = OFF_KML + ML_WIDTH
OFF_OML = OFF_VML + ML_WIDTH
OFF_Z = OFF_OML + ML_WIDTH
OFF_XBC = OFF_Z + SSD_WIDTH
OFF_QR = OFF_XBC + SSD_CONV_DIM
OFF_KR = OFF_QR + RET_WIDTH
OFF_VR = OFF_KR + RET_WIDTH
OFF_GR = OFF_VR + RET_WIDTH
OFF_SMALL = OFF_GR + RET_WIDTH
D_PROJ = OFF_SMALL + LANES
LANE_I = 0
LANE_F = ML_HEADS
LANE_DT = 2 * ML_HEADS

VMEM_LIMIT = 56 * 1024 * 1024


def _cparams(sem):
    return pltpu.CompilerParams(dimension_semantics=sem, vmem_limit_bytes=VMEM_LIMIT)


def _const_spec(shape):
    nd = len(shape)
    return pl.BlockSpec(shape, lambda *_: (0,) * nd, pipeline_mode=pl.Buffered(1))


def _rms(x, g):
    return x * lax.rsqrt(jnp.mean(x * x, axis=-1, keepdims=True) + EPS) * g


def _sigmoid(x):
    return 1.0 / (1.0 + jnp.exp(-x))


def _silu(x):
    return x * _sigmoid(x)


def _dot(a, b):
    return jnp.dot(a, b, preferred_element_type=F32)


def _dot_nt(a, b):
    return lax.dot_general(a, b, (((1,), (1,)), ((), ())), preferred_element_type=F32)


def _dot_exact(a, b):
    return jnp.dot(a, b, preferred_element_type=F32, precision=HIGHEST)


def _norm_matmul_kernel(*refs, norm, residual):
    x_ref, g_ref, w_ref = refs[:3]
    o_ref = refs[-1]
    x = x_ref[...].astype(F32)
    if norm:
        x = _rms(x, g_ref[...])
    acc = _dot(x.astype(BF16), w_ref[...])
    if residual:
        acc = acc + refs[3][...]
    o_ref[...] = acc


def norm_matmul(x, g, w, residual=None, *, norm=True, tm=None, tn=None):
    m, k = x.shape
    n = w.shape[1]
    tm = tm or min(m, 512)
    tn = tn or min(n, 1024)
    in_specs = [pl.BlockSpec((tm, k), lambda j, i: (i, 0)),
                pl.BlockSpec((1, k), lambda j, i: (0, 0)),
                pl.BlockSpec((k, tn), lambda j, i: (0, j))]
    args = [x, g, w]
    if residual is not None:
        in_specs.append(pl.BlockSpec((tm, tn), lambda j, i: (i, j)))
        args.append(residual)
    return pl.pallas_call(
        functools.partial(_norm_matmul_kernel, norm=norm, residual=residual is not None),
        grid=(pl.cdiv(n, tn), m // tm),
        in_specs=in_specs,
        out_specs=pl.BlockSpec((tm, tn), lambda j, i: (i, j)),
        out_shape=jax.ShapeDtypeStruct((m, n), F32),
        compiler_params=_cparams(("parallel", "parallel")),
        name="norm_matmul",
    )(*args)


def _kv_proj_kernel(x_ref, g_ref, wk_ref, wv_ref, k_ref, v_ref):
    xn = _rms(x_ref[...], g_ref[...]).astype(BF16)
    k_ref[...] = _dot(xn, wk_ref[...])
    v_ref[...] = _dot(xn, wv_ref[...])


def kv_proj(mem2d, g, wk, wv, *, tm=512):
    m, d = mem2d.shape
    depth = wk.shape[0]
    out = jax.ShapeDtypeStruct((depth, m, d), F32)
    return pl.pallas_call(
        _kv_proj_kernel,
        grid=(depth, m // tm),
        in_specs=[pl.BlockSpec((tm, d), lambda l, i: (i, 0)),
                  pl.BlockSpec((None, 1, d), lambda l, i: (l, 0, 0)),
                  pl.BlockSpec((None, d, d), lambda l, i: (l, 0, 0)),
                  pl.BlockSpec((None, d, d), lambda l, i: (l, 0, 0))],
        out_specs=[pl.BlockSpec((None, tm, d), lambda l, i: (l, i, 0)),
                   pl.BlockSpec((None, tm, d), lambda l, i: (l, i, 0))],
        out_shape=[out, out],
        compiler_params=_cparams(("parallel", "parallel")),
        name="kv_proj",
    )(mem2d, g, wk, wv)


def _ca_prompt_kernel(x_ref, g_ref, wq_ref, wo_ref, k_ref, v_ref, o_ref):
    x = x_ref[...]
    xn = _rms(x, g_ref[...]).astype(BF16)
    q = _dot(xn, wq_ref[...])
    kb = k_ref[...].astype(BF16)
    vb = v_ref[...].astype(BF16)
    outs = []
    for h in range(CA_HEADS):
        sl = slice(h * CA_DH, (h + 1) * CA_DH)
        s = _dot_nt(q[:, sl].astype(BF16), kb[:, sl]) * (CA_DH ** -0.5)
        p = jnp.exp(s - jnp.max(s, axis=-1, keepdims=True))
        p = p / jnp.sum(p, axis=-1, keepdims=True)
        outs.append(_dot(p.astype(BF16), vb[:, sl]).astype(BF16))
    o = jnp.concatenate(outs, axis=-1)
    o_ref[...] = x + _dot(o, wo_ref[...])


def ca_prompt(x, g, wq, wo, mem_k, mem_v, layer, *, tm=512):
    b, length, d = x.shape
    mlen = mem_k.shape[2]
    return pl.pallas_call(
        _ca_prompt_kernel,
        grid=(b, length // tm),
        in_specs=[pl.BlockSpec((None, tm, d), lambda i, j: (i, j, 0)),
                  _const_spec((1, d)), _const_spec((d, d)), _const_spec((d, d)),
                  pl.BlockSpec((None, None, mlen, d), lambda i, j: (layer, i, 0, 0)),
                  pl.BlockSpec((None, None, mlen, d), lambda i, j: (layer, i, 0, 0))],
        out_specs=pl.BlockSpec((None, tm, d), lambda i, j: (i, j, 0)),
        out_shape=jax.ShapeDtypeStruct(x.shape, F32),
        compiler_params=_cparams(("parallel", "parallel")),
        name="ca_prompt",
    )(x, g, wq, wo, mem_k, mem_v)


def _ca_sample_kernel(q_ref, k_ref, v_ref, o_ref, *, bt):
    for j in range(bt):
        kq = k_ref[j] * q_ref[j]
        s = jnp.sum(kq, axis=-1, keepdims=True) * (CA_DH ** -0.5)
        p = jnp.exp(s - jnp.max(s, axis=0, keepdims=True))
        p = p / jnp.sum(p, axis=0, keepdims=True)
        o_ref[j] = jnp.sum(p * v_ref[j], axis=0)


def ca_sample(q, mem_k, mem_v, layer, *, bt=8):
    b = q.shape[0]
    blk = (None, bt) + mem_k.shape[2:]
    return pl.pallas_call(
        functools.partial(_ca_sample_kernel, bt=bt),
        grid=(b // bt,),
        in_specs=[pl.BlockSpec((bt,) + q.shape[1:], lambda i: (i, 0, 0)),
                  pl.BlockSpec(blk, lambda i: (layer, i, 0, 0, 0)),
                  pl.BlockSpec(blk, lambda i: (layer, i, 0, 0, 0))],
        out_specs=pl.BlockSpec((bt,) + q.shape[1:], lambda i: (i, 0, 0)),
        out_shape=jax.ShapeDtypeStruct(q.shape, F32),
        compiler_params=_cparams(("parallel",)),
        name="ca_sample",
    )(q, mem_k, mem_v)


def _ffn_kernel(x_ref, g_ref, wg_ref, wu_ref, wd_ref, o_ref, a_ref, *, fchunk):
    x = x_ref[...]
    xn = _rms(x, g_ref[...]).astype(BF16)
    dff = wg_ref.shape[1]
    for f0 in range(0, dff, fchunk):
        gate = _dot(xn, wg_ref[:, f0:f0 + fchunk])
        up = _dot(xn, wu_ref[:, f0:f0 + fchunk])
        a_ref[:, f0:f0 + fchunk] = (_silu(gate) * up).astype(BF16)
    o_ref[...] = x + _dot(a_ref[...], wd_ref[...])


def ffn_dense(x, g, wg, wu, wd, *, tm=512):
    m, d = x.shape
    dff = wg.shape[1]
    return pl.pallas_call(
        functools.partial(_ffn_kernel, fchunk=dff // 2),
        grid=(m // tm,),
        in_specs=[pl.BlockSpec((tm, d), lambda i: (i, 0)),
                  _const_spec((1, d)), _const_spec((d, dff)), _const_spec((d, dff)), _const_spec((dff, d))],
        out_specs=pl.BlockSpec((tm, d), lambda i: (i, 0)),
        out_shape=jax.ShapeDtypeStruct((m, d), F32),
        scratch_shapes=[pltpu.VMEM((tm, dff), BF16)],
        compiler_params=_cparams(("parallel",)),
        name="ffn_dense",
    )(x, g, wg, wu, wd)


def _moe_kernel(x_ref, g_ref, wrt_ref, wg_ref, wu_ref, wd_ref, gf_ref, o_ref,
                xn_ref, gate_t_ref, pos_t_ref, gate_c_ref, pos_c_ref, *, tm, sub, final_norm):
    e = pl.program_id(1)
    ne = pl.num_programs(1)
    epad = gate_t_ref.shape[0]

    @pl.when(e == 0)
    def _route():
        x = x_ref[...]
        xn = _rms(x, g_ref[...])
        xn_ref[...] = xn.astype(BF16)
        x_hi = xn.astype(BF16)
        x_lo = (xn - x_hi.astype(F32)).astype(BF16)
        w = wrt_ref[...]
        w_hi = w.astype(BF16)
        w_lo = (w - w_hi.astype(F32)).astype(BF16)
        logits = _dot_nt(w_hi, x_hi) + (_dot_nt(w_hi, x_lo) + _dot_nt(w_lo, x_hi))
        row = lax.broadcasted_iota(jnp.int32, (epad, tm), 0)
        logits = jnp.where(row < N_EXPERTS, logits, -jnp.inf)
        m1 = jnp.max(logits, axis=0, keepdims=True)
        i1 = jnp.min(jnp.where(logits == m1, row, epad), axis=0, keepdims=True)
        rest = jnp.where(row == i1, -jnp.inf, logits)
        m2 = jnp.max(rest, axis=0, keepdims=True)
        i2 = jnp.min(jnp.where(rest == m2, row, epad), axis=0, keepdims=True)
        e2 = jnp.exp(m2 - m1)
        den = 1.0 + e2
        gate_t = jnp.where(row == i1, 1.0 / den, 0.0) + jnp.where(row == i2, e2 / den, 0.0)
        sel_t = jnp.where((row == i1) | (row == i2), 1.0, 0.0)
        r_i = lax.broadcasted_iota(jnp.int32, (tm, tm), 0)
        c_i = lax.broadcasted_iota(jnp.int32, (tm, tm), 1)
        upper = jnp.where(r_i < c_i, 1.0, 0.0).astype(BF16)
        pos_t = _dot(sel_t.astype(BF16), upper)
        pos_t = jnp.where(sel_t > 0, pos_t, -1.0)
        gate_t_ref[...] = gate_t
        pos_t_ref[...] = pos_t
        pad = jnp.zeros((LANES - epad, tm), F32)
        gate_c_ref[...] = jnp.concatenate([gate_t, pad], axis=0).T
        pos_c_ref[...] = jnp.concatenate([pos_t, pad - 1.0], axis=0).T
        o_ref[...] = x

    pos_row = pos_t_ref[pl.ds(e, 1), :]
    lane = lax.broadcasted_iota(jnp.int32, (tm, LANES), 1)
    pos_col = jnp.sum(jnp.where(lane == e, pos_c_ref[...], 0.0), axis=-1, keepdims=True)
    gate_col = jnp.sum(jnp.where(lane == e, gate_c_ref[...], 0.0), axis=-1, keepdims=True)
    count = jnp.sum(jnp.where(pos_row >= 0, 1.0, 0.0)).astype(jnp.int32)
    nsub = (count + (sub - 1)) // sub

    def sub_body(j, carry):
        base = (j * sub).astype(F32)
        slot_r = lax.broadcasted_iota(jnp.int32, (sub, tm), 0).astype(F32) + base
        gather = jnp.where(pos_row == slot_r, 1.0, 0.0).astype(BF16)
        slot_c = lax.broadcasted_iota(jnp.int32, (tm, sub), 1).astype(F32) + base
        scatter = jnp.where(pos_col == slot_c, 1.0, 0.0).astype(BF16)
        xg = _dot(gather, xn_ref[...]).astype(BF16)
        act = (_silu(_dot(xg, wg_ref[...])) * _dot(xg, wu_ref[...])).astype(BF16)
        ye = _dot(act, wd_ref[...]).astype(BF16)
        o_ref[...] += gate_col * _dot(scatter, ye)
        return carry

    lax.fori_loop(0, nsub, sub_body, 0)

    if final_norm:
        @pl.when(e == ne - 1)
        def _final():
            o_ref[...] = _rms(o_ref[...], gf_ref[...])


def moe_ffn(x, g, w_router_t, wg, wu, wd, g_final, *, tm, sub, final_norm):
    m, d = x.shape
    ne, _, de = wg.shape
    epad = w_router_t.shape[0]
    return pl.pallas_call(
        functools.partial(_moe_kernel, tm=tm, sub=sub, final_norm=final_norm),
        grid=(m // tm, ne),
        in_specs=[pl.BlockSpec((tm, d), lambda i, e: (i, 0)),
                  pl.BlockSpec((1, d), lambda i, e: (0, 0)),
                  pl.BlockSpec((epad, d), lambda i, e: (0, 0)),
                  pl.BlockSpec((None, d, de), lambda i, e: (e, 0, 0)),
                  pl.BlockSpec((None, d, de), lambda i, e: (e, 0, 0)),
                  pl.BlockSpec((None, de, d), lambda i, e: (e, 0, 0)),
                  pl.BlockSpec((1, d), lambda i, e: (0, 0))],
        out_specs=pl.BlockSpec((tm, d), lambda i, e: (i, 0)),
        out_shape=jax.ShapeDtypeStruct((m, d), F32),
        scratch_shapes=[pltpu.VMEM((tm, d), BF16),
                        pltpu.VMEM((epad, tm), F32), pltpu.VMEM((epad, tm), F32),
                        pltpu.VMEM((tm, LANES), F32), pltpu.VMEM((tm, LANES), F32)],
        compiler_params=_cparams(("parallel", "arbitrary")),
        name="moe_ffn",
    )(x, g, w_router_t, wg, wu, wd, g_final)


def _retention_constants(c):
    log_g = np.log1p(-np.exp2(-5.0 - np.arange(RET_HEADS, dtype=np.float32))).astype(np.float32)
    j = np.arange(c, dtype=np.float32)
    rel = j[:, None] - j[None, :]
    w_intra = np.where(rel >= 0, np.exp(rel[None] * log_g[:, None, None]), 0.0).astype(np.float32)
    w_inter = np.exp((j[None, :] + 1.0) * log_g[:, None]).astype(np.float32)
    w_end = np.exp((c - 1.0 - j[None, :]) * log_g[:, None]).astype(np.float32)
    decay = np.exp(c * log_g).astype(np.float32)
    cols = np.zeros((c, LANES), np.float32)
    cols[:, :RET_HEADS] = w_inter.T
    cols[:, RET_HEADS:2 * RET_HEADS] = w_end.T
    return w_intra, cols, [float(d) for d in decay]


def _rotary_tables(pos):
    half = RET_DH // 2
    inv = ROPE_BASE ** (-jnp.arange(half, dtype=F32) / half)
    ang = pos[:, None] * inv[None, :]
    cos, sin = jnp.cos(ang), jnp.sin(ang)
    return jnp.concatenate([cos, cos], axis=-1), jnp.concatenate([-sin, sin], axis=-1)


def _head_expand_matrix():
    e = np.zeros((LANES, SSD_WIDTH), np.float32)
    for h in range(SSD_HEADS):
        e[LANE_DT + h, h * SSD_HEADDIM:(h + 1) * SSD_HEADDIM] = 1.0
    return e


def _softplus_parts(v):
    sp = jnp.log1p(jnp.exp(-jnp.abs(v)))
    return jnp.minimum(v, 0.0) - sp, jnp.maximum(v, 0.0) + sp


def _mixer_prompt_kernel(x_ref, g_ref, win_ref, wout_ref, bias_ref, alog_ref, convw_ref, convb_ref,
                         mlg_ref, ssdd_ref, ssdg_ref, retg_ref, cos_ref, sin_ref, retw_ref, retcol_ref,
                         expand_ref,
                         xo_ref, c_ref, n_ref, m_ref, conv_ref, ssd_ref, ret_ref,
                         proj_s, y_s, u_s, *, tm, ret_decay):
    c = CHUNK
    ti = pl.program_id(1)

    @pl.when(ti == 0)
    def _init():
        c_ref[...] = jnp.zeros_like(c_ref)
        n_ref[...] = jnp.zeros_like(n_ref)
        m_ref[...] = jnp.zeros_like(m_ref)
        ssd_ref[...] = jnp.zeros_like(ssd_ref)
        ret_ref[...] = jnp.zeros_like(ret_ref)
        u_s[0:8, :] = jnp.zeros((8, SSD_CONV_DIM), F32)

    x = x_ref[...]
    xn = _rms(x, g_ref[...]).astype(BF16)
    for off in range(0, D_PROJ, 1024):
        wdt = min(1024, D_PROJ - off)
        proj_s[:, off:off + wdt] = _dot(xn, win_ref[:, off:off + wdt])

    r_i = lax.broadcasted_iota(jnp.int32, (c, c), 0)
    c_i = lax.broadcasted_iota(jnp.int32, (c, c), 1)
    causal = r_i >= c_i
    tril = jnp.where(causal, 1.0, 0.0)
    lane1 = lax.broadcasted_iota(jnp.int32, (1, LANES), 1)
    lane_c = lax.broadcasted_iota(jnp.int32, (c, LANES), 1)
    a_vec = -jnp.exp(alog_ref[...])
    expand = expand_ref[...]
    bias = bias_ref[...]

    def chunk_body(ci, carry):
        r0 = pl.multiple_of(ci * c, c)
        rows = pl.ds(r0, c)
        small = proj_s[rows, OFF_SMALL:OFF_SMALL + LANES] + bias
        logf, dt = _softplus_parts(small)
        is_f = (lane_c >= LANE_F) & (lane_c < LANE_DT)
        is_dt = (lane_c >= LANE_DT) & (lane_c < LANE_DT + SSD_HEADS)
        incr = jnp.where(is_f, logf, jnp.where(is_dt, dt * a_vec, 0.0))
        cs = _dot_exact(tril, incr)
        cs_t = cs.T
        small_t = small.T

        m_vec = m_ref[...]
        m_out = m_vec
        for h in range(ML_HEADS):
            hs = slice(h * ML_DH, (h + 1) * ML_DH)
            b_col = cs[:, LANE_F + h:LANE_F + h + 1]
            b_row = cs_t[LANE_F + h:LANE_F + h + 1, :]
            i_col = small[:, LANE_I + h:LANE_I + h + 1]
            i_row = small_t[LANE_I + h:LANE_I + h + 1, :]
            m_prev = m_vec[:, h:h + 1]
            lw = jnp.where(causal, b_col + (i_row - b_row), -jnp.inf)
            log_a = b_col + m_prev
            m_t = jnp.maximum(log_a, jnp.max(lw, axis=-1, keepdims=True))
            w_intra = jnp.exp(lw - m_t)
            w_inter = jnp.exp(log_a - m_t)
            q = proj_s[rows, OFF_QML + h * ML_DH:OFF_QML + (h + 1) * ML_DH]
            k = proj_s[rows, OFF_KML + h * ML_DH:OFF_KML + (h + 1) * ML_DH] * (ML_DH ** -0.5)
            v = proj_s[rows, OFF_VML + h * ML_DH:OFF_VML + (h + 1) * ML_DH]
            og = proj_s[rows, OFF_OML + h * ML_DH:OFF_OML + (h + 1) * ML_DH]
            qb, kb, vb = q.astype(BF16), k.astype(BF16), v.astype(BF16)
            s = w_intra * _dot_nt(qb, kb)
            cm = c_ref[h]
            n_row = n_ref[h:h + 1, :]
            num = _dot(s.astype(BF16), vb) + w_inter * _dot_nt(qb, cm.astype(BF16))
            den = jnp.sum(s, axis=-1, keepdims=True) + w_inter * jnp.sum(q * n_row, axis=-1, keepdims=True)
            den = jnp.maximum(jnp.abs(den), jnp.exp(-m_t))
            hh = num / den
            m_new = m_t[c - 1:c, :]
            decay = jnp.exp(log_a[c - 1:c, :] - m_new)
            w_end = jnp.exp(b_col[c - 1:c, :] - b_col + i_col - m_new)
            c_ref[h] = decay * cm + _dot((v * w_end).T.astype(BF16), kb)
            n_ref[h:h + 1, :] = decay * n_row + jnp.sum(k * w_end, axis=0, keepdims=True)
            m_out = jnp.where(lane1 == h, m_new, m_out)
            hn = hh * lax.rsqrt(jnp.mean(hh * hh, axis=-1, keepdims=True) + EPS)
            y_s[rows, hs] = (hn * mlg_ref[:, hs] * _sigmoid(og)).astype(BF16)
        m_ref[...] = m_out

        u = proj_s[rows, OFF_XBC:OFF_XBC + SSD_CONV_DIM]
        u_s[8:8 + c, :] = u
        conv = convb_ref[...] + convw_ref[3:4, :] * u
        for jj in range(CONV_W - 1):
            conv = conv + convw_ref[jj:jj + 1, :] * u_s[5 + jj:5 + jj + c, :]
        u_s[0:8, :] = u_s[c:c + 8, :]
        xc = _silu(conv)
        xs = xc[:, :SSD_WIDTH]
        dt_exp = _dot_exact(jnp.where(is_dt, dt, 0.0), expand)
        acum_exp = _dot_exact(jnp.where(is_dt, cs, 0.0), expand)
        xdt = xs * dt_exp
        w_end_exp = jnp.exp(acum_exp[c - 1:c, :] - acum_exp)
        xw = xdt * w_end_exp
        ea_exp = jnp.exp(acum_exp)
        lane_lo = lane_c < SSD_HEADDIM
        hpg = SSD_HEADS // SSD_GROUPS
        y_cols = []
        for g in range(SSD_GROUPS):
            gw = slice(g * SSD_GROUP_WIDTH, (g + 1) * SSD_GROUP_WIDTH)
            bg = xc[:, SSD_WIDTH + g * SSD_STATE:SSD_WIDTH + (g + 1) * SSD_STATE].astype(BF16)
            cg = xc[:, SSD_WIDTH + (SSD_GROUPS + g) * SSD_STATE:
                    SSD_WIDTH + (SSD_GROUPS + g + 1) * SSD_STATE].astype(BF16)
            cb = _dot_nt(cg, bg)
            st = ssd_ref[gw, :]
            inter = _dot_nt(cg, st.astype(BF16)) * ea_exp[:, gw]
            for pr in range(hpg // 2):
                h0 = g * hpg + 2 * pr
                xp = xdt[:, h0 * SSD_HEADDIM:(h0 + 2) * SSD_HEADDIM].astype(BF16)
                pair = []
                for hidx in (h0, h0 + 1):
                    a_col = cs[:, LANE_DT + hidx:LANE_DT + hidx + 1]
                    a_row = cs_t[LANE_DT + hidx:LANE_DT + hidx + 1, :]
                    dec = jnp.exp(jnp.where(causal, a_col - a_row, -jnp.inf))
                    pair.append(_dot((dec * cb).astype(BF16), xp))
                y_cols.append(jnp.where(lane_lo, pair[0], pair[1]) + inter[:, pr * LANES:(pr + 1) * LANES])
            upd = _dot(xw[:, gw].T.astype(BF16), bg)
            for r in range(hpg):
                hidx = g * hpg + r
                scal = jnp.exp(cs[c - 1:c, LANE_DT + hidx:LANE_DT + hidx + 1])
                hr = slice(hidx * SSD_HEADDIM, (hidx + 1) * SSD_HEADDIM)
                ssd_ref[hr, :] = scal * ssd_ref[hr, :] + upd[r * SSD_HEADDIM:(r + 1) * SSD_HEADDIM, :]
        ys = jnp.concatenate(y_cols, axis=-1) + ssdd_ref[...] * xs
        yz = ys * _silu(proj_s[rows, OFF_Z:OFF_Z + SSD_WIDTH])
        for g in range(SSD_GROUPS):
            gw = slice(g * SSD_GROUP_WIDTH, (g + 1) * SSD_GROUP_WIDTH)
            seg = yz[:, gw]
            seg = seg * lax.rsqrt(jnp.mean(seg * seg, axis=-1, keepdims=True) + EPS) * ssdg_ref[:, gw]
            y_s[rows, ML_WIDTH + g * SSD_GROUP_WIDTH:ML_WIDTH + (g + 1) * SSD_GROUP_WIDTH] = seg.astype(BF16)

        cosb = cos_ref[rows, :]
        sinb = sin_ref[rows, :]
        retcol = retcol_ref[...]
        for h in range(RET_HEADS):
            hs = slice(h * RET_DH, (h + 1) * RET_DH)
            q = proj_s[rows, OFF_QR + h * RET_DH:OFF_QR + (h + 1) * RET_DH]
            k = proj_s[rows, OFF_KR + h * RET_DH:OFF_KR + (h + 1) * RET_DH]
            v = proj_s[rows, OFF_VR + h * RET_DH:OFF_VR + (h + 1) * RET_DH]
            gr = proj_s[rows, OFF_GR + h * RET_DH:OFF_GR + (h + 1) * RET_DH]
            qr = q * cosb + pltpu.roll(q, RET_DH // 2, 1) * sinb
            kr = (k * cosb + pltpu.roll(k, RET_DH // 2, 1) * sinb) * (RET_DH ** -0.5)
            qb, kb, vb = qr.astype(BF16), kr.astype(BF16), v.astype(BF16)
            att = _dot_nt(qb, kb) * retw_ref[h]
            sm = ret_ref[h]
            o = _dot(att.astype(BF16), vb) + _dot(qb, sm.astype(BF16)) * retcol[:, h:h + 1]
            kw = kr * retcol[:, RET_HEADS + h:RET_HEADS + h + 1]
            ret_ref[h] = ret_decay[h] * sm + _dot(kw.T.astype(BF16), vb)
            on = o * lax.rsqrt(jnp.mean(o * o, axis=-1, keepdims=True) + EPS)
            y_s[rows, ML_WIDTH + SSD_WIDTH + h * RET_DH:ML_WIDTH + SSD_WIDTH + (h + 1) * RET_DH] = (
                on * retg_ref[:, hs] * _silu(gr)).astype(BF16)
        return carry

    lax.fori_loop(0, tm // c, chunk_body, 0)
    xo_ref[...] = x + _dot(y_s[...], wout_ref[...])
    conv_ref[...] = u_s[5:8, :]


def mixer_prompt(x, lw, consts, *, tm):
    b, length, d = x.shape
    c = CHUNK
    row = lambda w: _const_spec((1, w))
    state_spec = lambda *shape: pl.BlockSpec((None,) + shape, lambda i, j: (i,) + (0,) * len(shape))
    out_shapes = [jax.ShapeDtypeStruct(x.shape, F32),
                  jax.ShapeDtypeStruct((b, ML_HEADS, ML_DH, ML_DH), F32),
                  jax.ShapeDtypeStruct((b, ML_HEADS, ML_DH), F32),
                  jax.ShapeDtypeStruct((b, 1, LANES), F32),
                  jax.ShapeDtypeStruct((b, CONV_W - 1, SSD_CONV_DIM), F32),
                  jax.ShapeDtypeStruct((b, SSD_WIDTH, SSD_STATE), F32),
                  jax.ShapeDtypeStruct((b, RET_HEADS, RET_DH, RET_DH), F32)]
    return pl.pallas_call(
        functools.partial(_mixer_prompt_kernel, tm=tm, ret_decay=consts["ret_decay"]),
        grid=(b, length // tm),
        in_specs=[pl.BlockSpec((None, tm, d), lambda i, j: (i, j, 0)),
                  row(d), _const_spec((d, D_PROJ)), _const_spec((D_MIX, d)),
                  row(LANES), row(LANES), _const_spec((CONV_W, SSD_CONV_DIM)), row(SSD_CONV_DIM),
                  row(ML_WIDTH), row(SSD_WIDTH), row(SSD_WIDTH), row(RET_WIDTH),
                  pl.BlockSpec((tm, LANES), lambda i, j: (j, 0)),
                  pl.BlockSpec((tm, LANES), lambda i, j: (j, 0)),
                  _const_spec((RET_HEADS, c, c)), _const_spec((c, LANES)), _const_spec((LANES, SSD_WIDTH))],
        out_specs=[pl.BlockSpec((None, tm, d), lambda i, j: (i, j, 0)),
                   state_spec(ML_HEADS, ML_DH, ML_DH), state_spec(ML_HEADS, ML_DH), state_spec(1, LANES),
                   state_spec(CONV_W - 1, SSD_CONV_DIM), state_spec(SSD_WIDTH, SSD_STATE),
                   state_spec(RET_HEADS, RET_DH, RET_DH)],
        out_shape=out_shapes,
        scratch_shapes=[pltpu.VMEM((tm, D_PROJ), F32), pltpu.VMEM((tm, D_MIX), BF16),
                        pltpu.VMEM((c + 8, SSD_CONV_DIM), F32)],
        compiler_params=_cparams(("parallel", "arbitrary")),
        name="mixer_prompt",
    )(x, lw["norm_mix"], lw["w_in"], lw["w_out"], lw["bias"], lw["alog"], lw["conv_w"], lw["conv_b"],
      lw["ml_norm"], lw["ssd_d"], lw["ssd_norm"], lw["ret_norm"], consts["cos_p"], consts["sin_p"],
      consts["ret_w"], consts["ret_cols"], consts["expand"])


N_SAMPLE_STATES = 6


def _mixer_sample_kernel(*refs, bt, ret_gamma, n_alias):
    (proj_ref, small_ref, bias_ref, alog_ref, convw_ref, convb_ref, mlg_ref, ssdd_ref,
     ssdg_ref, retg_ref, cos_ref, sin_ref, expand_ref,
     c_in, n_in, m_in, conv_in, ssd_in, ret_in) = refs[:19]
    (y_ref, c_out, n_out, m_out, conv_out, ssd_out, ret_out,
     wi_s, wt_s, en_s, dtx_s, eax_s) = refs[19 + n_alias:]
    small = small_ref[...] + bias_ref[...]
    logf, dt = _softplus_parts(small)
    lane_b = lax.broadcasted_iota(jnp.int32, (bt, LANES), 1)
    is_dt = (lane_b >= LANE_DT) & (lane_b < LANE_DT + SSD_HEADS)
    log_a = pltpu.roll(logf, LANES - LANE_F, 1) + m_in[...]
    m_t = jnp.maximum(log_a, small)
    m_out[...] = m_t
    wi_s[...] = jnp.exp(small - m_t)
    wt_s[...] = jnp.exp(log_a - m_t)
    en_s[...] = jnp.exp(-m_t)
    a_vec = -jnp.exp(alog_ref[...])
    expand = expand_ref[...]
    dtx_s[...] = _dot_exact(jnp.where(is_dt, dt, 0.0), expand)
    eax_s[...] = jnp.exp(_dot_exact(jnp.where(is_dt, dt * a_vec, 0.0), expand))

    sub16 = lax.broadcasted_iota(jnp.int32, (BF16_ROWS, LANES), 0)
    row_hi = lax.broadcasted_iota(jnp.int32, (LANES, LANES), 0) < SSD_HEADDIM

    def rows3(a, b, c):
        return jnp.where(sub16 == 0, a, jnp.where(sub16 == 1, b, jnp.where(sub16 == 2, c, 0.0))).astype(BF16)

    def split(v):
        hi = v.astype(BF16).astype(F32)
        return hi, v - hi

    def outer(a, b):
        a_hi, a_lo = split(a)
        b_hi, b_lo = split(b)
        return lax.dot_general(rows3(a_hi, a_hi, a_lo), rows3(b_hi, b_lo, b_hi), (((0,), (0,)), ((), ())),
                               preferred_element_type=F32)

    def row1(a):
        return jnp.where(sub16 == 0, a, 0.0).astype(BF16)

    cosb = cos_ref[...]
    sinb = sin_ref[...]
    hpg = SSD_HEADS // SSD_GROUPS

    sub_l = lax.broadcasted_iota(jnp.int32, (bt, LANES), 0)
    sub_w = lax.broadcasted_iota(jnp.int32, (bt, SSD_WIDTH), 0)

    def body(j, carry):
        pick_l = lambda ref: jnp.sum(jnp.where(sub_l == j, ref[...], 0.0), axis=0, keepdims=True)
        pick_w = lambda ref: jnp.sum(jnp.where(sub_w == j, ref[...], 0.0), axis=0, keepdims=True)
        wi = pick_l(wi_s)
        wt = pick_l(wt_s)
        en = pick_l(en_s)
        for h in range(ML_HEADS):
            hs = slice(h * ML_DH, (h + 1) * ML_DH)
            q = proj_ref[j, :,OFF_QML + h * ML_DH:OFF_QML + (h + 1) * ML_DH]
            k = proj_ref[j, :,OFF_KML + h * ML_DH:OFF_KML + (h + 1) * ML_DH] * (ML_DH ** -0.5)
            v = proj_ref[j, :,OFF_VML + h * ML_DH:OFF_VML + (h + 1) * ML_DH]
            og = proj_ref[j, :,OFF_OML + h * ML_DH:OFF_OML + (h + 1) * ML_DH]
            w_in_h = wi[:, h:h + 1]
            w_tr_h = wt[:, h:h + 1]
            c_new = w_tr_h * c_in[j, h] + w_in_h * outer(v, k)
            n_new = w_tr_h * n_in[j, h:h + 1, :] + w_in_h * k
            c_out[j, h] = c_new
            n_out[j, h:h + 1, :] = n_new
            num = _dot_nt(row1(q), c_new.astype(BF16))[0:1, :]
            den = jnp.sum(n_new * q, axis=-1, keepdims=True)
            den = jnp.maximum(jnp.abs(den), en[:, h:h + 1])
            hh = num / den
            hn = hh * lax.rsqrt(jnp.mean(hh * hh, axis=-1, keepdims=True) + EPS)
            y_ref[j, :,hs] = hn * mlg_ref[:, hs] * _sigmoid(og)
        u = proj_ref[j, :,OFF_XBC:OFF_XBC + SSD_CONV_DIM]
        prev = conv_in[j]
        conv = convb_ref[...] + convw_ref[CONV_W - 1:CONV_W, :] * u
        for jj in range(CONV_W - 1):
            conv = conv + convw_ref[jj:jj + 1, :] * prev[jj:jj + 1, :]
        conv_out[j, 0:CONV_W - 2, :] = prev[1:CONV_W - 1, :]
        conv_out[j, CONV_W - 2:CONV_W - 1, :] = u
        xc = _silu(conv)
        xs = xc[:, :SSD_WIDTH]
        xdt = xs * pick_w(dtx_s)
        ea = pick_w(eax_s)
        y_cols = []
        for g in range(SSD_GROUPS):
            bg = xc[:, SSD_WIDTH + g * SSD_STATE:SSD_WIDTH + (g + 1) * SSD_STATE]
            cg = xc[:, SSD_WIDTH + (SSD_GROUPS + g) * SSD_STATE:SSD_WIDTH + (SSD_GROUPS + g + 1) * SSD_STATE]
            for pr in range(hpg // 2):
                h0 = g * hpg + 2 * pr
                pw = slice(h0 * SSD_HEADDIM, (h0 + 2) * SSD_HEADDIM)
                lo = h0 * SSD_HEADDIM
                decay = jnp.where(row_hi, ea[:, lo:lo + 1], ea[:, lo + SSD_HEADDIM:lo + SSD_HEADDIM + 1])
                st_new = decay * ssd_in[j, pw, :] + outer(xdt[:, pw], bg)
                ssd_out[j, pw, :] = st_new
                y_cols.append(_dot_nt(row1(cg), st_new.astype(BF16))[0:1, :])
        ys = jnp.concatenate(y_cols, axis=-1) + ssdd_ref[...] * xs
        yz = ys * _silu(proj_ref[j, :,OFF_Z:OFF_Z + SSD_WIDTH])
        for g in range(SSD_GROUPS):
            gw = slice(g * SSD_GROUP_WIDTH, (g + 1) * SSD_GROUP_WIDTH)
            seg = yz[:, gw]
            y_ref[j, :,ML_WIDTH + g * SSD_GROUP_WIDTH:ML_WIDTH + (g + 1) * SSD_GROUP_WIDTH] = (
                seg * lax.rsqrt(jnp.mean(seg * seg, axis=-1, keepdims=True) + EPS) * ssdg_ref[:, gw])
        for h in range(RET_HEADS):
            hs = slice(h * RET_DH, (h + 1) * RET_DH)
            q = proj_ref[j, :,OFF_QR + h * RET_DH:OFF_QR + (h + 1) * RET_DH]
            k = proj_ref[j, :,OFF_KR + h * RET_DH:OFF_KR + (h + 1) * RET_DH]
            v = proj_ref[j, :,OFF_VR + h * RET_DH:OFF_VR + (h + 1) * RET_DH]
            gr = proj_ref[j, :,OFF_GR + h * RET_DH:OFF_GR + (h + 1) * RET_DH]
            qr = q * cosb + pltpu.roll(q, RET_DH // 2, 1) * sinb
            kr = (k * cosb + pltpu.roll(k, RET_DH // 2, 1) * sinb) * (RET_DH ** -0.5)
            s_new = ret_gamma[h] * ret_in[j, h] + outer(kr, v)
            ret_out[j, h] = s_new
            o = _dot(row1(qr), s_new.astype(BF16))[0:1, :]
            on = o * lax.rsqrt(jnp.mean(o * o, axis=-1, keepdims=True) + EPS)
            y_ref[j, :,ML_WIDTH + SSD_WIDTH + h * RET_DH:ML_WIDTH + SSD_WIDTH + (h + 1) * RET_DH] = (
                on * retg_ref[:, hs] * _silu(gr))
        return carry

    lax.fori_loop(0, bt, body, 0)


def mixer_sample(proj, lw, consts, states, layer, prev_out=None, *, bt=8):
    b = proj.shape[0]
    row = lambda w: pl.BlockSpec((1, w), lambda i: (0, 0))
    blk = lambda *shape: pl.BlockSpec((bt,) + shape, lambda i: (i,) + (0,) * len(shape))
    lblk = lambda *shape: pl.BlockSpec((None, bt) + shape, lambda i: (layer, i) + (0,) * len(shape))
    state_specs = [lblk(ML_HEADS, ML_DH, ML_DH), lblk(ML_HEADS, ML_DH), lblk(LANES),
                   lblk(CONV_W - 1, SSD_CONV_DIM), lblk(SSD_WIDTH, SSD_STATE), lblk(RET_HEADS, RET_DH, RET_DH)]
    state_shapes = [jax.ShapeDtypeStruct(s.shape, F32) for s in states]
    proj3 = proj.reshape(b, 1, D_PROJ)
    small = proj[:, OFF_SMALL:OFF_SMALL + LANES]
    n_fixed = 13 + N_SAMPLE_STATES
    alias_args = list(prev_out) if prev_out is not None else []
    aliases = {n_fixed + k: 1 + k for k in range(len(alias_args))}
    outs = pl.pallas_call(
        functools.partial(_mixer_sample_kernel, bt=bt, ret_gamma=consts["ret_gamma"], n_alias=len(alias_args)),
        grid=(b // bt,),
        in_specs=[blk(1, D_PROJ), blk(LANES), row(LANES), row(LANES),
                  pl.BlockSpec((CONV_W, SSD_CONV_DIM), lambda i: (0, 0)), row(SSD_CONV_DIM),
                  row(ML_WIDTH), row(SSD_WIDTH), row(SSD_WIDTH), row(RET_WIDTH), row(LANES), row(LANES),
                  pl.BlockSpec((LANES, SSD_WIDTH), lambda i: (0, 0))] + state_specs
                 + [pl.BlockSpec(memory_space=pl.ANY)] * len(alias_args),
        out_specs=[blk(1, D_MIX)] + state_specs,
        out_shape=[jax.ShapeDtypeStruct((b, 1, D_MIX), F32)] + state_shapes,
        input_output_aliases=aliases,
        scratch_shapes=[pltpu.VMEM((bt, LANES), F32)] * 3 + [pltpu.VMEM((bt, SSD_WIDTH), F32)] * 2,
        compiler_params=_cparams(("parallel",)),
        name="mixer_sample",
    )(proj3, small, lw["bias"], lw["alog"], lw["conv_w"], lw["conv_b"], lw["ml_norm"], lw["ssd_d"],
      lw["ssd_norm"], lw["ret_norm"], consts["cos_s"], consts["sin_s"], consts["expand"],
      *states, *alias_args)
    return outs[0].reshape(b, D_MIX), list(outs[1:])


def _pack_in_proj(w):
    sizes = [ML_WIDTH] * 4 + [ML_HEADS, ML_HEADS, SSD_WIDTH, SSD_CONV_DIM, SSD_HEADS] + [RET_WIDTH] * 4
    offs = np.concatenate([[0], np.cumsum(sizes)])
    seg = lambda i: w[:, int(offs[i]):int(offs[i + 1])]
    small = jnp.concatenate([seg(4), seg(5), seg(8)], axis=1)
    small = jnp.pad(small, ((0, 0), (0, LANES - small.shape[1])))
    cols = [seg(0), seg(1), seg(2), seg(3), seg(6), seg(7), seg(9), seg(10), seg(11), seg(12), small]
    return jnp.concatenate(cols, axis=1).astype(BF16)


def _lane_row(parts):
    v = jnp.concatenate([p.astype(F32) for p in parts])
    return jnp.pad(v, (0, LANES - v.shape[0]))[None, :]


def kernel(x_prompt, x_sample, state_mlstm_c, state_mlstm_n, state_mlstm_m, state_ssd_conv, state_ssd, state_ret,
           cache_mem_k, cache_mem_v, mem_prompt,
           norm_mix, w_in, ml_i_bias, ml_f_bias, ml_norm, ssd_conv_w, ssd_conv_b, ssd_dt_bias, ssd_a_log, ssd_d,
           ssd_norm, ret_norm, w_out, norm_ca, norm_mem, w_ca_q, w_ca_k, w_ca_v, w_ca_o, norm_ffn,
           ffn_w_gate, ffn_w_up, ffn_w_down, moe_w_router, moe_w_gate, moe_w_up, moe_w_down, norm_final):
    bp, seq, d = x_prompt.shape
    bs = x_sample.shape[0]
    assert x_sample.shape[1] == 1 and seq % CHUNK == 0 and DEPTH % 2 == 0

    ret_w, ret_cols, ret_decay = _retention_constants(CHUNK)
    _, _, ret_gamma = _retention_constants(1)
    cos_p, sin_p = _rotary_tables(jnp.arange(seq, dtype=F32))
    cos_s, sin_s = _rotary_tables(PAST_LEN + jnp.arange(1, dtype=F32))
    consts = {"ret_w": jnp.asarray(ret_w), "ret_cols": jnp.asarray(ret_cols), "ret_decay": ret_decay,
              "ret_gamma": ret_gamma, "cos_p": cos_p, "sin_p": sin_p, "cos_s": cos_s, "sin_s": sin_s,
              "expand": jnp.asarray(_head_expand_matrix())}
    zeros12 = jnp.zeros((2 * ML_HEADS,), F32)
    ones_row = jnp.ones((1, D_MIX), F32)

    mem2d = mem_prompt.reshape(bp * MEM_LEN, d)
    mem_k_p, mem_v_p = kv_proj(mem2d, norm_mem[:, None, :], w_ca_k.astype(BF16), w_ca_v.astype(BF16))

    mem_k_p = mem_k_p.reshape(DEPTH, bp, MEM_LEN, d)
    mem_v_p = mem_v_p.reshape(DEPTH, bp, MEM_LEN, d)
    sample_states = (state_mlstm_c, state_mlstm_n,
                     jnp.pad(state_mlstm_m, ((0, 0), (0, 0), (0, LANES - ML_HEADS))), state_ssd_conv,
                     state_ssd.reshape(DEPTH, bs, SSD_WIDTH, SSD_STATE), state_ret)
    st_s = None

    xp = x_prompt
    xs = x_sample.reshape(bs, d)
    new_p = [[] for _ in range(6)]
    for l in range(DEPTH):
        lw = {"norm_mix": norm_mix[l][None, :], "w_in": _pack_in_proj(w_in[l]), "w_out": w_out[l].astype(BF16),
              "bias": _lane_row([ml_i_bias[l], ml_f_bias[l], ssd_dt_bias[l]]),
              "alog": _lane_row([zeros12, ssd_a_log[l]]),
              "conv_w": ssd_conv_w[l], "conv_b": ssd_conv_b[l][None, :], "ml_norm": ml_norm[l][None, :],
              "ssd_d": jnp.repeat(ssd_d[l], SSD_HEADDIM)[None, :], "ssd_norm": ssd_norm[l][None, :],
              "ret_norm": ret_norm[l][None, :]}
        wq, wo = w_ca_q[l].astype(BF16), w_ca_o[l].astype(BF16)
        g_ca, g_ffn = norm_ca[l][None, :], norm_ffn[l][None, :]
        last = l == DEPTH - 1
        j = l // 2
        if l % 2 == 0:
            ffn_w = (ffn_w_gate[j].astype(BF16), ffn_w_up[j].astype(BF16), ffn_w_down[j].astype(BF16))
        else:
            wrt = jnp.pad(moe_w_router[j].T, ((0, 2 * N_EXPERTS - N_EXPERTS), (0, 0)))
            ffn_w = (wrt, moe_w_gate[j].astype(BF16), moe_w_up[j].astype(BF16), moe_w_down[j].astype(BF16))

        def run_ffn(x2d, tm, sub):
            if l % 2 == 0:
                return ffn_dense(x2d, g_ffn, *ffn_w, tm=tm)
            return moe_ffn(x2d, g_ffn, *ffn_w, norm_final[None, :], tm=tm, sub=sub, final_norm=last)

        xp, c1, n1, m1, conv1, ssd1, ret1 = mixer_prompt(xp, lw, consts, tm=256)
        st_p = (c1, n1, m1[:, 0, :ML_HEADS], conv1,
                ssd1.reshape(bp, SSD_HEADS, SSD_HEADDIM, SSD_STATE), ret1)
        xp = ca_prompt(xp, g_ca, wq, wo, mem_k_p, mem_v_p, l)
        xp = run_ffn(xp.reshape(bp * seq, d), 1024 if l % 2 else 512, 256).reshape(bp, seq, d)

        proj = norm_matmul(xs, lw["norm_mix"], lw["w_in"], tn=D_PROJ // 3)
        y, st_s = mixer_sample(proj, lw, consts, sample_states, l, st_s)
        xs = norm_matmul(y, ones_row, lw["w_out"], residual=xs, norm=False)
        q = norm_matmul(xs, g_ca, wq)
        o = ca_sample(q.reshape(bs, CA_HEADS, CA_DH), cache_mem_k, cache_mem_v, l).reshape(bs, d)
        xs = norm_matmul(o, ones_row[:, :d], wo, residual=xs, norm=False)
        xs = run_ffn(xs, bs, bs)

        for lst, a in zip(new_p, st_p):
            lst.append(a)

    shape5 = (DEPTH, bp, MEM_LEN, CA_HEADS, CA_DH)
    c_s, n_s, m_s, conv_s, ssd_s, ret_s = st_s
    return (xp, xs.reshape(bs, 1, d),
            jnp.stack(new_p[0]), jnp.stack(new_p[1]), jnp.stack(new_p[2]), jnp.stack(new_p[3]),
            jnp.stack(new_p[4]), jnp.stack(new_p[5]), mem_k_p.reshape(shape5), mem_v_p.reshape(shape5),
            c_s, n_s, m_s[:, :, :ML_HEADS], conv_s,
            ssd_s.reshape(DEPTH, bs, SSD_HEADS, SSD_HEADDIM, SSD_STATE), ret_s)
```

```python
import functools
import itertools
import math

import numpy as np
import jax
import jax.numpy as jnp
from jax import lax
from jax.experimental import pallas as pl
from jax.experimental.pallas import tpu as pltpu

F32 = jnp.float32
BF16 = jnp.bfloat16
HIGHEST = lax.Precision.HIGHEST

D_MODEL = 1024
DEPTH = 2
PAST_LEN = 16384
D_MIX = 2 * D_MODEL
ML_WIDTH = 3 * D_MIX // 8
ML_HEADS = 6
ML_DH = ML_WIDTH // ML_HEADS
SSD_WIDTH = 3 * D_MIX // 8
SSD_HEADDIM = 64
SSD_HEADS = SSD_WIDTH // SSD_HEADDIM
SSD_STATE = 128
SSD_GROUPS = 2
SSD_GROUP_WIDTH = SSD_WIDTH // SSD_GROUPS
SSD_CONV_DIM = SSD_WIDTH + 2 * SSD_GROUPS * SSD_STATE
CONV_W = 4
RET_WIDTH = D_MIX // 4
RET_HEADS = 4
RET_DH = RET_WIDTH // RET_HEADS
ROPE_BASE = 10000.0
CHUNK = 128
MEM_LEN = 256
CA_HEADS = 4
CA_DH = D_MODEL // CA_HEADS
D_FF = 11 * D_MODEL // 4
N_EXPERTS = 8
TOP_K = 2
D_EXPERT = D_FF // 2
EPS = 1e-6

LANES = 128
BF16_ROWS = 16

OFF_QML = 0
OFF_KML = OFF_QML + ML_WIDTH
OFF_VML = OFF_KML + ML_WIDTH
OFF_OML = OFF_VML + ML_WIDTH
OFF_Z = OFF_OML + ML_WIDTH
OFF_XBC = OFF_Z + SSD_WIDTH
OFF_QR = OFF_XBC + SSD_CONV_DIM
OFF_KR = OFF_QR + RET_WIDTH
OFF_VR = OFF_KR + RET_WIDTH
OFF_GR = OFF_VR + RET_WIDTH
OFF_SMALL = OFF_GR + RET_WIDTH
D_PROJ = OFF_SMALL + LANES
LANE_I = 0
LANE_F = ML_HEADS
LANE_DT = 2 * ML_HEADS

VMEM_LIMIT = 56 * 1024 * 1024


def _cparams(sem):
    return pltpu.CompilerParams(dimension_semantics=sem, vmem_limit_bytes=VMEM_LIMIT)


def _const_spec(shape):
    nd = len(shape)
    return pl.BlockSpec(shape, lambda *_: (0,) * nd, pipeline_mode=pl.Buffered(1))


def _rms(x, g):
    return x * lax.rsqrt(jnp.mean(x * x, axis=-1, keepdims=True) + EPS) * g


def _sigmoid(x):
    return 1.0 / (1.0 + jnp.exp(-x))


def _silu(x):
    return x * _sigmoid(x)


def _dot(a, b):
    return jnp.dot(a, b, preferred_element_type=F32)


def _dot_nt(a, b):
    return lax.dot_general(a, b, (((1,), (1,)), ((), ())), preferred_element_type=F32)


def _dot_exact(a, b):
    return jnp.dot(a, b, preferred_element_type=F32, precision=HIGHEST)


def _split3(v):
    v1 = v.astype(BF16)
    r1 = v - v1.astype(F32)
    v2 = r1.astype(BF16)
    v3 = (r1 - v2.astype(F32)).astype(BF16)
    return v1, v2, v3


def _norm_matmul_kernel(*refs, norm, residual):
    x_ref, g_ref, w_ref = refs[:3]
    o_ref = refs[-1]
    x = x_ref[...].astype(F32)
    if norm:
        x = _rms(x, g_ref[...])
    acc = _dot(x.astype(BF16), w_ref[...])
    if residual:
        acc = acc + refs[3][...]
    o_ref[...] = acc


def norm_matmul(x, g, w, residual=None, *, norm=True, tm=None, tn=None):
    m, k = x.shape
    n = w.shape[1]
    tm = tm or min(m, 512)
    tn = tn or min(n, 1024)
    in_specs = [pl.BlockSpec((tm, k), lambda j, i: (i, 0)),
                pl.BlockSpec((1, k), lambda j, i: (0, 0)),
                pl.BlockSpec((k, tn), lambda j, i: (0, j))]
    args = [x, g, w]
    if residual is not None:
        in_specs.append(pl.BlockSpec((tm, tn), lambda j, i: (i, j)))
        args.append(residual)
    return pl.pallas_call(
        functools.partial(_norm_matmul_kernel, norm=norm, residual=residual is not None),
        grid=(pl.cdiv(n, tn), m // tm),
        in_specs=in_specs,
        out_specs=pl.BlockSpec((tm, tn), lambda j, i: (i, j)),
        out_shape=jax.ShapeDtypeStruct((m, n), F32),
        compiler_params=_cparams(("parallel", "parallel")),
        name="norm_matmul",
    )(*args)


def _kv_proj_kernel(x_ref, g_ref, wk_ref, wv_ref, k_ref, v_ref):
    xn = _rms(x_ref[...], g_ref[...]).astype(BF16)
    k_ref[...] = _dot(xn, wk_ref[...])
    v_ref[...] = _dot(xn, wv_ref[...])


def kv_proj(mem2d, g, wk, wv, *, tm=512):
    m, d = mem2d.shape
    depth = wk.shape[0]
    out = jax.ShapeDtypeStruct((depth, m, d), F32)
    return pl.pallas_call(
        _kv_proj_kernel,
        grid=(depth, m // tm),
        in_specs=[pl.BlockSpec((tm, d), lambda l, i: (i, 0)),
                  pl.BlockSpec((None, 1, d), lambda l, i: (l, 0, 0)),
                  pl.BlockSpec((None, d, d), lambda l, i: (l, 0, 0)),
                  pl.BlockSpec((None, d, d), lambda l, i: (l, 0, 0))],
        out_specs=[pl.BlockSpec((None, tm, d), lambda l, i: (l, i, 0)),
                   pl.BlockSpec((None, tm, d), lambda l, i: (l, i, 0))],
        out_shape=[out, out],
        compiler_params=_cparams(("parallel", "parallel")),
        name="kv_proj",
    )(mem2d, g, wk, wv)


def _ca_prompt_kernel(x_ref, g_ref, wq_ref, wo_ref, k_ref, v_ref, o_ref):
    x = x_ref[...]
    xn = _rms(x, g_ref[...]).astype(BF16)
    q = _dot(xn, wq_ref[...])
    kb = k_ref[...].astype(BF16)
    vb = v_ref[...].astype(BF16)
    outs = []
    for h in range(CA_HEADS):
        sl = slice(h * CA_DH, (h + 1) * CA_DH)
        s = _dot_nt(q[:, sl].astype(BF16), kb[:, sl]) * (CA_DH ** -0.5)
        p = jnp.exp(s - jnp.max(s, axis=-1, keepdims=True))
        p = p / jnp.sum(p, axis=-1, keepdims=True)
        outs.append(_dot(p.astype(BF16), vb[:, sl]).astype(BF16))
    o = jnp.concatenate(outs, axis=-1)
    o_ref[...] = x + _dot(o, wo_ref[...])


def ca_prompt(x, g, wq, wo, mem_k, mem_v, layer, *, tm=512):
    b, length, d = x.shape
    mlen = mem_k.shape[2]
    return pl.pallas_call(
        _ca_prompt_kernel,
        grid=(b, length // tm),
        in_specs=[pl.BlockSpec((None, tm, d), lambda i, j: (i, j, 0)),
                  _const_spec((1, d)), _const_spec((d, d)), _const_spec((d, d)),
                  pl.BlockSpec((None, None, mlen, d), lambda i, j: (layer, i, 0, 0)),
                  pl.BlockSpec((None, None, mlen, d), lambda i, j: (layer, i, 0, 0))],
        out_specs=pl.BlockSpec((None, tm, d), lambda i, j: (i, j, 0)),
        out_shape=jax.ShapeDtypeStruct(x.shape, F32),
        compiler_params=_cparams(("parallel", "parallel")),
        name="ca_prompt",
    )(x, g, wq, wo, mem_k, mem_v)


def _ca_sample_kernel(q_ref, k_ref, v_ref, o_ref, *, bt):
    for j in range(bt):
        kq = k_ref[j] * q_ref[j]
        s = jnp.sum(kq, axis=-1, keepdims=True) * (CA_DH ** -0.5)
        p = jnp.exp(s - jnp.max(s, axis=0, keepdims=True))
        p = p / jnp.sum(p, axis=0, keepdims=True)
        o_ref[j] = jnp.sum(p * v_ref[j], axis=0)


def ca_sample(q, mem_k, mem_v, layer, *, bt=8):
    b = q.shape[0]
    blk = (None, bt) + mem_k.shape[2:]
    return pl.pallas_call(
        functools.partial(_ca_sample_kernel, bt=bt),
        grid=(b // bt,),
        in_specs=[pl.BlockSpec((bt,) + q.shape[1:], lambda i: (i, 0, 0)),
                  pl.BlockSpec(blk, lambda i: (layer, i, 0, 0, 0)),
                  pl.BlockSpec(blk, lambda i: (layer, i, 0, 0, 0))],
        out_specs=pl.BlockSpec((bt,) + q.shape[1:], lambda i: (i, 0, 0)),
        out_shape=jax.ShapeDtypeStruct(q.shape, F32),
        compiler_params=_cparams(("parallel",)),
        name="ca_sample",
    )(q, mem_k, mem_v)


def _ffn_kernel(x_ref, g_ref, wg_ref, wu_ref, wd_ref, o_ref, a_ref, *, fchunk):
    x = x_ref[...]
    xn = _rms(x, g_ref[...]).astype(BF16)
    dff = wg_ref.shape[1]
    for f0 in range(0, dff, fchunk):
        gate = _dot(xn, wg_ref[:, f0:f0 + fchunk])
        up = _dot(xn, wu_ref[:, f0:f0 + fchunk])
        a_ref[:, f0:f0 + fchunk] = (_silu(gate) * up).astype(BF16)
    o_ref[...] = x + _dot(a_ref[...], wd_ref[...])


def ffn_dense(x, g, wg, wu, wd, *, tm=512):
    m, d = x.shape
    dff = wg.shape[1]
    return pl.pallas_call(
        functools.partial(_ffn_kernel, fchunk=dff // 2),
        grid=(m // tm,),
        in_specs=[pl.BlockSpec((tm, d), lambda i: (i, 0)),
                  _const_spec((1, d)), _const_spec((d, dff)), _const_spec((d, dff)), _const_spec((dff, d))],
        out_specs=pl.BlockSpec((tm, d), lambda i: (i, 0)),
        out_shape=jax.ShapeDtypeStruct((m, d), F32),
        scratch_shapes=[pltpu.VMEM((tm, dff), BF16)],
        compiler_params=_cparams(("parallel",)),
        name="ffn_dense",
    )(x, g, wg, wu, wd)


def _moe_kernel(x_ref, g_ref, wrt_ref, wg_ref, wu_ref, wd_ref, gf_ref, o_ref,
                xn_ref, gate_t_ref, pos_t_ref, *, tm, sub, final_norm):
    e = pl.program_id(1)
    ne = pl.num_programs(1)
    epad = gate_t_ref.shape[0]

    @pl.when(e == 0)
    def _route():
        x = x_ref[...]
        xn = _rms(x, g_ref[...])
        xn_ref[...] = xn.astype(BF16)
        x_hi = xn.astype(BF16)
        x_lo = (xn - x_hi.astype(F32)).astype(BF16)
        w = wrt_ref[...]
        w_hi = w.astype(BF16)
        w_lo = (w - w_hi.astype(F32)).astype(BF16)
        logits = _dot_nt(w_hi, x_hi) + (_dot_nt(w_hi, x_lo) + _dot_nt(w_lo, x_hi))
        row = lax.broadcasted_iota(jnp.int32, (epad, tm), 0)
        logits = jnp.where(row < N_EXPERTS, logits, -jnp.inf)
        m1 = jnp.max(logits, axis=0, keepdims=True)
        i1 = jnp.min(jnp.where(logits == m1, row, epad), axis=0, keepdims=True)
        rest = jnp.where(row == i1, -jnp.inf, logits)
        m2 = jnp.max(rest, axis=0, keepdims=True)
        i2 = jnp.min(jnp.where(rest == m2, row, epad), axis=0, keepdims=True)
        e2 = jnp.exp(m2 - m1)
        den = 1.0 + e2
        gate_t = jnp.where(row == i1, 1.0 / den, 0.0) + jnp.where(row == i2, e2 / den, 0.0)
        sel_t = jnp.where((row == i1) | (row == i2), 1.0, 0.0)
        r_i = lax.broadcasted_iota(jnp.int32, (tm, tm), 0)
        c_i = lax.broadcasted_iota(jnp.int32, (tm, tm), 1)
        upper = jnp.where(r_i < c_i, 1.0, 0.0).astype(BF16)
        pos_t = _dot(sel_t.astype(BF16), upper)
        pos_t = jnp.where(sel_t > 0, pos_t, -1.0)
        gate_t_ref[...] = gate_t
        pos_t_ref[...] = pos_t
        o_ref[...] = x

    pos_row = pos_t_ref[pl.ds(e, 1), :]
    gate_row = gate_t_ref[pl.ds(e, 1), :]
    count = jnp.sum(jnp.where(pos_row >= 0, 1.0, 0.0)).astype(jnp.int32)
    nsub = (count + (sub - 1)) // sub

    def sub_body(j, carry):
        base = (j * sub).astype(F32)
        slot = lax.broadcasted_iota(jnp.int32, (sub, tm), 0).astype(F32) + base
        hit = pos_row == slot
        gather = jnp.where(hit, 1.0, 0.0).astype(BF16)
        gate = jnp.sum(jnp.where(hit, gate_row, 0.0), axis=-1, keepdims=True)
        xg = _dot(gather, xn_ref[...]).astype(BF16)
        act = (_silu(_dot(xg, wg_ref[...])) * _dot(xg, wu_ref[...])).astype(BF16)
        ye = (gate * _dot(act, wd_ref[...])).astype(BF16)
        o_ref[...] += lax.dot_general(gather, ye, (((0,), (0,)), ((), ())), preferred_element_type=F32)
        return carry

    lax.fori_loop(0, nsub, sub_body, 0)

    if final_norm:
        @pl.when(e == ne - 1)
        def _final():
            o_ref[...] = _rms(o_ref[...], gf_ref[...])


def _moe_sub_rows(tm):
    mean = tm * TOP_K // N_EXPERTS
    return min(tm, -(-(mean + mean // 8) // BF16_ROWS) * BF16_ROWS)


def moe_ffn(x, g, w_router_t, wg, wu, wd, g_final, *, tm, sub, final_norm):
    m, d = x.shape
    ne, _, de = wg.shape
    epad = w_router_t.shape[0]
    return pl.pallas_call(
        functools.partial(_moe_kernel, tm=tm, sub=sub, final_norm=final_norm),
        grid=(m // tm, ne),
        in_specs=[pl.BlockSpec((tm, d), lambda i, e: (i, 0)),
                  pl.BlockSpec((1, d), lambda i, e: (0, 0)),
                  pl.BlockSpec((epad, d), lambda i, e: (0, 0)),
                  pl.BlockSpec((None, d, de), lambda i, e: (e, 0, 0)),
                  pl.BlockSpec((None, d, de), lambda i, e: (e, 0, 0)),
                  pl.BlockSpec((None, de, d), lambda i, e: (e, 0, 0)),
                  pl.BlockSpec((1, d), lambda i, e: (0, 0))],
        out_specs=pl.BlockSpec((tm, d), lambda i, e: (i, 0)),
        out_shape=jax.ShapeDtypeStruct((m, d), F32),
        scratch_shapes=[pltpu.VMEM((tm, d), BF16),
                        pltpu.VMEM((epad, tm), F32), pltpu.VMEM((epad, tm), F32)],
        compiler_params=_cparams(("parallel", "arbitrary")),
        name="moe_ffn",
    )(x, g, w_router_t, wg, wu, wd, g_final)


def _retention_constants(c):
    log_g = np.log1p(-np.exp2(-5.0 - np.arange(RET_HEADS, dtype=np.float32))).astype(np.float32)
    j = np.arange(c, dtype=np.float32)
    rel = j[:, None] - j[None, :]
    w_intra = np.where(rel >= 0, np.exp(rel[None] * log_g[:, None, None]), 0.0).astype(np.float32)
    w_inter = np.exp((j[None, :] + 1.0) * log_g[:, None]).astype(np.float32)
    w_end = np.exp((c - 1.0 - j[None, :]) * log_g[:, None]).astype(np.float32)
    decay = np.exp(c * log_g).astype(np.float32)
    cols = np.zeros((c, LANES), np.float32)
    cols[:, :RET_HEADS] = w_inter.T
    cols[:, RET_HEADS:2 * RET_HEADS] = w_end.T
    return w_intra, cols, [float(d) for d in decay]


def _rotary_tables(pos):
    half = RET_DH // 2
    inv = ROPE_BASE ** (-jnp.arange(half, dtype=F32) / half)
    ang = pos[:, None] * inv[None, :]
    cos, sin = jnp.cos(ang), jnp.sin(ang)
    return jnp.concatenate([cos, cos], axis=-1), jnp.concatenate([-sin, sin], axis=-1)


def _head_expand_matrix():
    e = np.zeros((LANES, SSD_WIDTH), np.float32)
    for h in range(SSD_HEADS):
        e[LANE_DT + h, h * SSD_HEADDIM:(h + 1) * SSD_HEADDIM] = 1.0
    return e


def _softplus_parts(v):
    sp = jnp.log1p(jnp.exp(-jnp.abs(v)))
    return jnp.minimum(v, 0.0) - sp, jnp.maximum(v, 0.0) + sp


def _mixer_prompt_kernel(x_ref, xnext_ref, g_ref, win_ref, wout_ref, bias_ref, alog_ref, convw_ref, convb_ref,
                         mlg_ref, ssdd_ref, ssdg_ref, retg_ref, cos_ref, sin_ref, retw_ref, retcol_ref,
                         expand_ref,
                         xo_ref, c_ref, n_ref, m_ref, conv_ref, ssd_ref, ret_ref,
                         proj_s, y_s, u_s, *, tm, tiles_per_seq, ret_decay):
    c = CHUNK
    i = pl.program_id(0)

    @pl.when(i % tiles_per_seq == 0)
    def _init():
        c_ref[...] = jnp.zeros_like(c_ref)
        n_ref[...] = jnp.zeros_like(n_ref)
        m_ref[...] = jnp.zeros_like(m_ref)
        ssd_ref[...] = jnp.zeros_like(ssd_ref)
        ret_ref[...] = jnp.zeros_like(ret_ref)
        u_s[0:8, :] = jnp.zeros((8, SSD_CONV_DIM), F32)

    seg_w = 512

    def project(src_ref, slot):
        xn = _rms(src_ref[...], g_ref[...]).astype(BF16)
        for off in range(0, D_PROJ, seg_w):
            wdt = min(seg_w, D_PROJ - off)
            proj_s[slot, :, off:off + wdt] = _dot(xn, win_ref[:, off:off + wdt])
            yield

    @pl.when(i == 0)
    def _first():
        for _ in project(x_ref, 0):
            pass

    proj_cur = proj_s.at[i % 2]
    x = x_ref[...]

    r_i = lax.broadcasted_iota(jnp.int32, (c, c), 0)
    c_i = lax.broadcasted_iota(jnp.int32, (c, c), 1)
    causal = r_i >= c_i
    tril = jnp.where(causal, 1.0, 0.0).astype(BF16)
    lane1 = lax.broadcasted_iota(jnp.int32, (1, LANES), 1)
    lane_c = lax.broadcasted_iota(jnp.int32, (c, LANES), 1)
    a_vec = -jnp.exp(alog_ref[...])
    expand3 = expand_ref[...]
    bias = bias_ref[...]

    def cumsum_rows(v):
        p = _dot(tril, jnp.concatenate(_split3(v), axis=-1))
        return p[:, :LANES] + p[:, LANES:2 * LANES] + p[:, 2 * LANES:]

    def expand_heads(v):
        return _dot(jnp.concatenate(_split3(v), axis=-1), expand3)

    def chunk_body(ci):
        rows = slice(ci * c, (ci + 1) * c)
        small = proj_cur[rows, OFF_SMALL:OFF_SMALL + LANES] + bias
        logf, dt = _softplus_parts(small)
        is_f = (lane_c >= LANE_F) & (lane_c < LANE_DT)
        is_dt = (lane_c >= LANE_DT) & (lane_c < LANE_DT + SSD_HEADS)
        incr = jnp.where(is_f, logf, jnp.where(is_dt, dt * a_vec, 0.0))
        cs = cumsum_rows(incr)
        cs_t = cs.T
        small_t = small.T

        m_vec = m_ref[...]
        m_out = m_vec
        for h in range(ML_HEADS):
            hs = slice(h * ML_DH, (h + 1) * ML_DH)
            b_col = cs[:, LANE_F + h:LANE_F + h + 1]
            b_row = cs_t[LANE_F + h:LANE_F + h + 1, :]
            i_col = small[:, LANE_I + h:LANE_I + h + 1]
            i_row = small_t[LANE_I + h:LANE_I + h + 1, :]
            m_prev = m_vec[:, h:h + 1]
            lw = jnp.where(causal, b_col + (i_row - b_row), -jnp.inf)
            log_a = b_col + m_prev
            m_t = jnp.maximum(log_a, jnp.max(lw, axis=-1, keepdims=True))
            w_intra = jnp.exp(lw - m_t)
            w_inter = jnp.exp(log_a - m_t)
            q = proj_cur[rows, OFF_QML + h * ML_DH:OFF_QML + (h + 1) * ML_DH]
            k = proj_cur[rows, OFF_KML + h * ML_DH:OFF_KML + (h + 1) * ML_DH] * (ML_DH ** -0.5)
            v = proj_cur[rows, OFF_VML + h * ML_DH:OFF_VML + (h + 1) * ML_DH]
            og = proj_cur[rows, OFF_OML + h * ML_DH:OFF_OML + (h + 1) * ML_DH]
            qb, kb, vb = q.astype(BF16), k.astype(BF16), v.astype(BF16)
            s = w_intra * _dot_nt(qb, kb)
            cm = c_ref[h]
            n_row = n_ref[h:h + 1, :]
            num = _dot(s.astype(BF16), vb) + w_inter * _dot_nt(qb, cm.astype(BF16))
            den = jnp.sum(s, axis=-1, keepdims=True) + w_inter * jnp.sum(q * n_row, axis=-1, keepdims=True)
            den = jnp.maximum(jnp.abs(den), jnp.exp(-m_t))
            hh = num / den
            m_new = m_t[c - 1:c, :]
            decay = jnp.exp(log_a[c - 1:c, :] - m_new)
            w_end = jnp.exp(b_col[c - 1:c, :] - b_col + i_col - m_new)
            c_ref[h] = decay * cm + _dot((v * w_end).T.astype(BF16), kb)
            n_ref[h:h + 1, :] = decay * n_row + jnp.sum(k * w_end, axis=0, keepdims=True)
            m_out = jnp.where(lane1 == h, m_new, m_out)
            hn = hh * lax.rsqrt(jnp.mean(hh * hh, axis=-1, keepdims=True) + EPS)
            y_s[rows, hs] = (hn * mlg_ref[:, hs] * _sigmoid(og)).astype(BF16)
            yield
        m_ref[...] = m_out

        u = proj_cur[rows, OFF_XBC:OFF_XBC + SSD_CONV_DIM]
        u_s[8:8 + c, :] = u
        conv = convb_ref[...] + convw_ref[3:4, :] * u
        for jj in range(CONV_W - 1):
            conv = conv + convw_ref[jj:jj + 1, :] * u_s[5 + jj:5 + jj + c, :]
        u_s[0:8, :] = u_s[c:c + 8, :]
        xc = _silu(conv)
        xs = xc[:, :SSD_WIDTH]
        dt_exp = expand_heads(jnp.where(is_dt, dt, 0.0))
        acum_exp = expand_heads(jnp.where(is_dt, cs, 0.0))
        xdt = xs * dt_exp
        w_end_exp = jnp.exp(acum_exp[c - 1:c, :] - acum_exp)
        xw = xdt * w_end_exp
        ea_exp = jnp.exp(acum_exp)
        lane_lo = lane_c < SSD_HEADDIM
        hpg = SSD_HEADS // SSD_GROUPS
        y_cols = []
        yield
        for g in range(SSD_GROUPS):
            gw = slice(g * SSD_GROUP_WIDTH, (g + 1) * SSD_GROUP_WIDTH)
            bg = xc[:, SSD_WIDTH + g * SSD_STATE:SSD_WIDTH + (g + 1) * SSD_STATE].astype(BF16)
            cg = xc[:, SSD_WIDTH + (SSD_GROUPS + g) * SSD_STATE:
                    SSD_WIDTH + (SSD_GROUPS + g + 1) * SSD_STATE].astype(BF16)
            cb = _dot_nt(cg, bg)
            st = ssd_ref[gw, :]
            inter = _dot_nt(cg, st.astype(BF16)) * ea_exp[:, gw]
            for pr in range(hpg // 2):
                h0 = g * hpg + 2 * pr
                xp = xdt[:, h0 * SSD_HEADDIM:(h0 + 2) * SSD_HEADDIM].astype(BF16)
                pair = []
                for hidx in (h0, h0 + 1):
                    a_col = cs[:, LANE_DT + hidx:LANE_DT + hidx + 1]
                    a_row = cs_t[LANE_DT + hidx:LANE_DT + hidx + 1, :]
                    dec = jnp.exp(jnp.where(causal, a_col - a_row, -jnp.inf))
                    pair.append(_dot((dec * cb).astype(BF16), xp))
                y_cols.append(jnp.where(lane_lo, pair[0], pair[1]) + inter[:, pr * LANES:(pr + 1) * LANES])
                yield
            upd = _dot(xw[:, gw].T.astype(BF16), bg)
            for r in range(hpg):
                hidx = g * hpg + r
                scal = jnp.exp(cs[c - 1:c, LANE_DT + hidx:LANE_DT + hidx + 1])
                hr = slice(hidx * SSD_HEADDIM, (hidx + 1) * SSD_HEADDIM)
                ssd_ref[hr, :] = scal * ssd_ref[hr, :] + upd[r * SSD_HEADDIM:(r + 1) * SSD_HEADDIM, :]
            yield
        ys = jnp.concatenate(y_cols, axis=-1) + ssdd_ref[...] * xs
        yz = ys * _silu(proj_cur[rows, OFF_Z:OFF_Z + SSD_WIDTH])
        for g in range(SSD_GROUPS):
            gw = slice(g * SSD_GROUP_WIDTH, (g + 1) * SSD_GROUP_WIDTH)
            seg = yz[:, gw]
            seg = seg * lax.rsqrt(jnp.mean(seg * seg, axis=-1, keepdims=True) + EPS) * ssdg_ref[:, gw]
            y_s[rows, ML_WIDTH + g * SSD_GROUP_WIDTH:ML_WIDTH + (g + 1) * SSD_GROUP_WIDTH] = seg.astype(BF16)
        yield

        cosb = cos_ref[rows, :]
        sinb = sin_ref[rows, :]
        retcol = retcol_ref[...]
        for h in range(RET_HEADS):
            hs = slice(h * RET_DH, (h + 1) * RET_DH)
            q = proj_cur[rows, OFF_QR + h * RET_DH:OFF_QR + (h + 1) * RET_DH]
            k = proj_cur[rows, OFF_KR + h * RET_DH:OFF_KR + (h + 1) * RET_DH]
            v = proj_cur[rows, OFF_VR + h * RET_DH:OFF_VR + (h + 1) * RET_DH]
            gr = proj_cur[rows, OFF_GR + h * RET_DH:OFF_GR + (h + 1) * RET_DH]
            qr = q * cosb + pltpu.roll(q, RET_DH // 2, 1) * sinb
            kr = (k * cosb + pltpu.roll(k, RET_DH // 2, 1) * sinb) * (RET_DH ** -0.5)
            qb, kb, vb = qr.astype(BF16), kr.astype(BF16), v.astype(BF16)
            att = _dot_nt(qb, kb) * retw_ref[h]
            sm = ret_ref[h]
            o = _dot(att.astype(BF16), vb) + _dot(qb, sm.astype(BF16)) * retcol[:, h:h + 1]
            kw = kr * retcol[:, RET_HEADS + h:RET_HEADS + h + 1]
            ret_ref[h] = ret_decay[h] * sm + _dot(kw.T.astype(BF16), vb)
            on = o * lax.rsqrt(jnp.mean(o * o, axis=-1, keepdims=True) + EPS)
            y_s[rows, ML_WIDTH + SSD_WIDTH + h * RET_DH:ML_WIDTH + SSD_WIDTH + (h + 1) * RET_DH] = (
                on * retg_ref[:, hs] * _silu(gr)).astype(BF16)
            yield

    stages_per_chunk = ML_HEADS + 1 + SSD_GROUPS * (SSD_HEADS // SSD_GROUPS // 2 + 1) + 1 + RET_HEADS
    n_segments = -(-D_PROJ // seg_w)
    every = max(1, (tm // c) * stages_per_chunk // n_segments)
    segments = project(xnext_ref, (i + 1) % 2)
    for k, _ in enumerate(itertools.chain.from_iterable(chunk_body(ci) for ci in range(tm // c))):
        if k % every == 0:
            next(segments, None)
    for _ in segments:
        pass
    xo_ref[...] = x + _dot(y_s[...], wout_ref[...])
    conv_ref[...] = u_s[5:8, :]


def mixer_prompt(x, lw, consts, *, tm):
    b, length, d = x.shape
    c = CHUNK
    tps = length // tm
    n_tiles = b * tps
    x2d = x.reshape(b * length, d)
    row = lambda w: _const_spec((1, w))
    state_spec = lambda *shape: pl.BlockSpec((None,) + shape, lambda i: (i // tps,) + (0,) * len(shape))
    out_shapes = [jax.ShapeDtypeStruct(x2d.shape, F32),
                  jax.ShapeDtypeStruct((b, ML_HEADS, ML_DH, ML_DH), F32),
                  jax.ShapeDtypeStruct((b, ML_HEADS, ML_DH), F32),
                  jax.ShapeDtypeStruct((b, 1, LANES), F32),
                  jax.ShapeDtypeStruct((b, CONV_W - 1, SSD_CONV_DIM), F32),
                  jax.ShapeDtypeStruct((b, SSD_WIDTH, SSD_STATE), F32),
                  jax.ShapeDtypeStruct((b, RET_HEADS, RET_DH, RET_DH), F32)]
    outs = pl.pallas_call(
        functools.partial(_mixer_prompt_kernel, tm=tm, tiles_per_seq=tps, ret_decay=consts["ret_decay"]),
        grid=(n_tiles,),
        in_specs=[pl.BlockSpec((tm, d), lambda i: (i, 0)),
                  pl.BlockSpec((tm, d), lambda i: (jnp.minimum(i + 1, n_tiles - 1), 0)),
                  row(d), _const_spec((d, D_PROJ)), _const_spec((D_MIX, d)),
                  row(LANES), row(LANES), _const_spec((CONV_W, SSD_CONV_DIM)), row(SSD_CONV_DIM),
                  row(ML_WIDTH), row(SSD_WIDTH), row(SSD_WIDTH), row(RET_WIDTH),
                  pl.BlockSpec((tm, LANES), lambda i: (i % tps, 0)),
                  pl.BlockSpec((tm, LANES), lambda i: (i % tps, 0)),
                  _const_spec((RET_HEADS, c, c)), _const_spec((c, LANES)),
                  _const_spec((3 * LANES, SSD_WIDTH))],
        out_specs=[pl.BlockSpec((tm, d), lambda i: (i, 0)),
                   state_spec(ML_HEADS, ML_DH, ML_DH), state_spec(ML_HEADS, ML_DH), state_spec(1, LANES),
                   state_spec(CONV_W - 1, SSD_CONV_DIM), state_spec(SSD_WIDTH, SSD_STATE),
                   state_spec(RET_HEADS, RET_DH, RET_DH)],
        out_shape=out_shapes,
        scratch_shapes=[pltpu.VMEM((2, tm, D_PROJ), F32), pltpu.VMEM((tm, D_MIX), BF16),
                        pltpu.VMEM((c + 8, SSD_CONV_DIM), F32)],
        compiler_params=_cparams(("arbitrary",)),
        name="mixer_prompt",
    )(x2d, x2d, lw["norm_mix"], lw["w_in"], lw["w_out"], lw["bias"], lw["alog"], lw["conv_w"], lw["conv_b"],
      lw["ml_norm"], lw["ssd_d"], lw["ssd_norm"], lw["ret_norm"], consts["cos_p"], consts["sin_p"],
      consts["ret_w"], consts["ret_cols"], consts["expand3"])
    return [outs[0].reshape(b, length, d)] + list(outs[1:])


N_SAMPLE_STATES = 6


def _mixer_sample_kernel(*refs, bt, ret_gamma, n_alias):
    (proj_ref, small_ref, bias_ref, alog_ref, convw_ref, convb_ref, mlg_ref, ssdd_ref,
     ssdg_ref, retg_ref, cos_ref, sin_ref, expand_ref,
     c_in, n_in, m_in, conv_in, ssd_in, ret_in) = refs[:19]
    (y_ref, c_out, n_out, m_out, conv_out, ssd_out, ret_out,
     wi_s, wt_s, en_s, dtx_s, eax_s) = refs[19 + n_alias:]
    small = small_ref[...] + bias_ref[...]
    logf, dt = _softplus_parts(small)
    lane_b = lax.broadcasted_iota(jnp.int32, (bt, LANES), 1)
    is_dt = (lane_b >= LANE_DT) & (lane_b < LANE_DT + SSD_HEADS)
    log_a = pltpu.roll(logf, LANES - LANE_F, 1) + m_in[...]
    m_t = jnp.maximum(log_a, small)
    m_out[...] = m_t
    wi_s[...] = jnp.exp(small - m_t)
    wt_s[...] = jnp.exp(log_a - m_t)
    en_s[...] = jnp.exp(-m_t)
    a_vec = -jnp.exp(alog_ref[...])
    expand = expand_ref[...]
    dtx_s[...] = _dot_exact(jnp.where(is_dt, dt, 0.0), expand)
    eax_s[...] = jnp.exp(_dot_exact(jnp.where(is_dt, dt * a_vec, 0.0), expand))

    sub16 = lax.broadcasted_iota(jnp.int32, (BF16_ROWS, LANES), 0)
    row_hi = lax.broadcasted_iota(jnp.int32, (LANES, LANES), 0) < SSD_HEADDIM

    def rows3(a, b, c):
        return jnp.where(sub16 == 0, a, jnp.where(sub16 == 1, b, jnp.where(sub16 == 2, c, 0.0))).astype(BF16)

    def split(v):
        hi = v.astype(BF16).astype(F32)
        return hi, v - hi

    def outer(a, b):
        a_hi, a_lo = split(a)
        b_hi, b_lo = split(b)
        return lax.dot_general(rows3(a_hi, a_hi, a_lo), rows3(b_hi, b_lo, b_hi), (((0,), (0,)), ((), ())),
                               preferred_element_type=F32)

    def row1(a):
        return jnp.where(sub16 == 0, a, 0.0).astype(BF16)

    cosb = cos_ref[...]
    sinb = sin_ref[...]
    hpg = SSD_HEADS // SSD_GROUPS

    sub_l = lax.broadcasted_iota(jnp.int32, (bt, LANES), 0)
    sub_w = lax.broadcasted_iota(jnp.int32, (bt, SSD_WIDTH), 0)

    def body(j, carry):
        pick_l = lambda ref: jnp.sum(jnp.where(sub_l == j, ref[...], 0.0), axis=0, keepdims=True)
        pick_w = lambda ref: jnp.sum(jnp.where(sub_w == j, ref[...], 0.0), axis=0, keepdims=True)
        wi = pick_l(wi_s)
        wt = pick_l(wt_s)
        en = pick_l(en_s)
        for h in range(ML_HEADS):
            hs = slice(h * ML_DH, (h + 1) * ML_DH)
            q = proj_ref[j, :,OFF_QML + h * ML_DH:OFF_QML + (h + 1) * ML_DH]
            k = proj_ref[j, :,OFF_KML + h * ML_DH:OFF_KML + (h + 1) * ML_DH] * (ML_DH ** -0.5)
            v = proj_ref[j, :,OFF_VML + h * ML_DH:OFF_VML + (h + 1) * ML_DH]
            og = proj_ref[j, :,OFF_OML + h * ML_DH:OFF_OML + (h + 1) * ML_DH]
            w_in_h = wi[:, h:h + 1]
            w_tr_h = wt[:, h:h + 1]
            c_new = w_tr_h * c_in[j, h] + w_in_h * outer(v, k)
            n_new = w_tr_h * n_in[j, h:h + 1, :] + w_in_h * k
            c_out[j, h] = c_new
            n_out[j, h:h + 1, :] = n_new
            num = _dot_nt(row1(q), c_new.astype(BF16))[0:1, :]
            den = jnp.sum(n_new * q, axis=-1, keepdims=True)
            den = jnp.maximum(jnp.abs(den), en[:, h:h + 1])
            hh = num / den
            hn = hh * lax.rsqrt(jnp.mean(hh * hh, axis=-1, keepdims=True) + EPS)
            y_ref[j, :,hs] = hn * mlg_ref[:, hs] * _sigmoid(og)
        u = proj_ref[j, :,OFF_XBC:OFF_XBC + SSD_CONV_DIM]
        prev = conv_in[j]
        conv = convb_ref[...] + convw_ref[CONV_W - 1:CONV_W, :] * u
        for jj in range(CONV_W - 1):
            conv = conv + convw_ref[jj:jj + 1, :] * prev[jj:jj + 1, :]
        conv_out[j, 0:CONV_W - 2, :] = prev[1:CONV_W - 1, :]
        conv_out[j, CONV_W - 2:CONV_W - 1, :] = u
        xc = _silu(conv)
        xs = xc[:, :SSD_WIDTH]
        xdt = xs * pick_w(dtx_s)
        ea = pick_w(eax_s)
        y_cols = []
        for g in range(SSD_GROUPS):
            bg = xc[:, SSD_WIDTH + g * SSD_STATE:SSD_WIDTH + (g + 1) * SSD_STATE]
            cg = xc[:, SSD_WIDTH + (SSD_GROUPS + g) * SSD_STATE:SSD_WIDTH + (SSD_GROUPS + g + 1) * SSD_STATE]
            for pr in range(hpg // 2):
                h0 = g * hpg + 2 * pr
                pw = slice(h0 * SSD_HEADDIM, (h0 + 2) * SSD_HEADDIM)
                lo = h0 * SSD_HEADDIM
                decay = jnp.where(row_hi, ea[:, lo:lo + 1], ea[:, lo + SSD_HEADDIM:lo + SSD_HEADDIM + 1])
                st_new = decay * ssd_in[j, pw, :] + outer(xdt[:, pw], bg)
                ssd_out[j, pw, :] = st_new
                y_cols.append(_dot_nt(row1(cg), st_new.astype(BF16))[0:1, :])
        ys = jnp.concatenate(y_cols, axis=-1) + ssdd_ref[...] * xs
        yz = ys * _silu(proj_ref[j, :,OFF_Z:OFF_Z + SSD_WIDTH])
        for g in range(SSD_GROUPS):
            gw = slice(g * SSD_GROUP_WIDTH, (g + 1) * SSD_GROUP_WIDTH)
            seg = yz[:, gw]
            y_ref[j, :,ML_WIDTH + g * SSD_GROUP_WIDTH:ML_WIDTH + (g + 1) * SSD_GROUP_WIDTH] = (
                seg * lax.rsqrt(jnp.mean(seg * seg, axis=-1, keepdims=True) + EPS) * ssdg_ref[:, gw])
        for h in range(RET_HEADS):
            hs = slice(h * RET_DH, (h + 1) * RET_DH)
            q = proj_ref[j, :,OFF_QR + h * RET_DH:OFF_QR + (h + 1) * RET_DH]
            k = proj_ref[j, :,OFF_KR + h * RET_DH:OFF_KR + (h + 1) * RET_DH]
            v = proj_ref[j, :,OFF_VR + h * RET_DH:OFF_VR + (h + 1) * RET_DH]
            gr = proj_ref[j, :,OFF_GR + h * RET_DH:OFF_GR + (h + 1) * RET_DH]
            qr = q * cosb + pltpu.roll(q, RET_DH // 2, 1) * sinb
            kr = (k * cosb + pltpu.roll(k, RET_DH // 2, 1) * sinb) * (RET_DH ** -0.5)
            s_new = ret_gamma[h] * ret_in[j, h] + outer(kr, v)
            ret_out[j, h] = s_new
            o = _dot(row1(qr), s_new.astype(BF16))[0:1, :]
            on = o * lax.rsqrt(jnp.mean(o * o, axis=-1, keepdims=True) + EPS)
            y_ref[j, :,ML_WIDTH + SSD_WIDTH + h * RET_DH:ML_WIDTH + SSD_WIDTH + (h + 1) * RET_DH] = (
                on * retg_ref[:, hs] * _silu(gr))
        return carry

    lax.fori_loop(0, bt, body, 0)


def mixer_sample(proj, lw, consts, states, layer, prev_out=None, *, bt=8):
    b = proj.shape[0]
    row = lambda w: pl.BlockSpec((1, w), lambda i: (0, 0))
    blk = lambda *shape: pl.BlockSpec((bt,) + shape, lambda i: (i,) + (0,) * len(shape))
    lblk = lambda *shape: pl.BlockSpec((None, bt) + shape, lambda i: (layer, i) + (0,) * len(shape))
    state_specs = [lblk(ML_HEADS, ML_DH, ML_DH), lblk(ML_HEADS, ML_DH), lblk(LANES),
                   lblk(CONV_W - 1, SSD_CONV_DIM), lblk(SSD_WIDTH, SSD_STATE), lblk(RET_HEADS, RET_DH, RET_DH)]
    state_shapes = [jax.ShapeDtypeStruct(s.shape, F32) for s in states]
    proj3 = proj.reshape(b, 1, D_PROJ)
    small = proj[:, OFF_SMALL:OFF_SMALL + LANES]
    n_fixed = 13 + N_SAMPLE_STATES
    alias_args = list(prev_out) if prev_out is not None else []
    aliases = {n_fixed + k: 1 + k for k in range(len(alias_args))}
    outs = pl.pallas_call(
        functools.partial(_mixer_sample_kernel, bt=bt, ret_gamma=consts["ret_gamma"], n_alias=len(alias_args)),
        grid=(b // bt,),
        in_specs=[blk(1, D_PROJ), blk(LANES), row(LANES), row(LANES),
                  pl.BlockSpec((CONV_W, SSD_CONV_DIM), lambda i: (0, 0)), row(SSD_CONV_DIM),
                  row(ML_WIDTH), row(SSD_WIDTH), row(SSD_WIDTH), row(RET_WIDTH), row(LANES), row(LANES),
                  pl.BlockSpec((LANES, SSD_WIDTH), lambda i: (0, 0))] + state_specs
                 + [pl.BlockSpec(memory_space=pl.ANY)] * len(alias_args),
        out_specs=[blk(1, D_MIX)] + state_specs,
        out_shape=[jax.ShapeDtypeStruct((b, 1, D_MIX), F32)] + state_shapes,
        input_output_aliases=aliases,
        scratch_shapes=[pltpu.VMEM((bt, LANES), F32)] * 3 + [pltpu.VMEM((bt, SSD_WIDTH), F32)] * 2,
        compiler_params=_cparams(("parallel",)),
        name="mixer_sample",
    )(proj3, small, lw["bias"], lw["alog"], lw["conv_w"], lw["conv_b"], lw["ml_norm"], lw["ssd_d"],
      lw["ssd_norm"], lw["ret_norm"], consts["cos_s"], consts["sin_s"], consts["expand"],
      *states, *alias_args)
    return outs[0].reshape(b, D_MIX), list(outs[1:])


def _pack_in_proj(w):
    sizes = [ML_WIDTH] * 4 + [ML_HEADS, ML_HEADS, SSD_WIDTH, SSD_CONV_DIM, SSD_HEADS] + [RET_WIDTH] * 4
    offs = np.concatenate([[0], np.cumsum(sizes)])
    seg = lambda i: w[:, int(offs[i]):int(offs[i + 1])]
    small = jnp.concatenate([seg(4), seg(5), seg(8)], axis=1)
    small = jnp.pad(small, ((0, 0), (0, LANES - small.shape[1])))
    cols = [seg(0), seg(1), seg(2), seg(3), seg(6), seg(7), seg(9), seg(10), seg(11), seg(12), small]
    return jnp.concatenate(cols, axis=1).astype(BF16)


def _lane_row(parts):
    v = jnp.concatenate([p.astype(F32) for p in parts])
    return jnp.pad(v, (0, LANES - v.shape[0]))[None, :]


def kernel(x_prompt, x_sample, state_mlstm_c, state_mlstm_n, state_mlstm_m, state_ssd_conv, state_ssd, state_ret,
           cache_mem_k, cache_mem_v, mem_prompt,
           norm_mix, w_in, ml_i_bias, ml_f_bias, ml_norm, ssd_conv_w, ssd_conv_b, ssd_dt_bias, ssd_a_log, ssd_d,
           ssd_norm, ret_norm, w_out, norm_ca, norm_mem, w_ca_q, w_ca_k, w_ca_v, w_ca_o, norm_ffn,
           ffn_w_gate, ffn_w_up, ffn_w_down, moe_w_router, moe_w_gate, moe_w_up, moe_w_down, norm_final):
    bp, seq, d = x_prompt.shape
    bs = x_sample.shape[0]
    assert x_sample.shape[1] == 1 and seq % CHUNK == 0 and DEPTH % 2 == 0

    ret_w, ret_cols, ret_decay = _retention_constants(CHUNK)
    _, _, ret_gamma = _retention_constants(1)
    cos_p, sin_p = _rotary_tables(jnp.arange(seq, dtype=F32))
    cos_s, sin_s = _rotary_tables(PAST_LEN + jnp.arange(1, dtype=F32))
    consts = {"ret_w": jnp.asarray(ret_w), "ret_cols": jnp.asarray(ret_cols), "ret_decay": ret_decay,
              "ret_gamma": ret_gamma, "cos_p": cos_p, "sin_p": sin_p, "cos_s": cos_s, "sin_s": sin_s,
              "expand": jnp.asarray(_head_expand_matrix()),
              "expand3": jnp.asarray(np.tile(_head_expand_matrix(), (3, 1))).astype(BF16)}
    zeros12 = jnp.zeros((2 * ML_HEADS,), F32)
    ones_row = jnp.ones((1, D_MIX), F32)

    mem2d = mem_prompt.reshape(bp * MEM_LEN, d)
    mem_k_p, mem_v_p = kv_proj(mem2d, norm_mem[:, None, :], w_ca_k.astype(BF16), w_ca_v.astype(BF16))

    mem_k_p = mem_k_p.reshape(DEPTH, bp, MEM_LEN, d)
    mem_v_p = mem_v_p.reshape(DEPTH, bp, MEM_LEN, d)
    sample_states = (state_mlstm_c, state_mlstm_n,
                     jnp.pad(state_mlstm_m, ((0, 0), (0, 0), (0, LANES - ML_HEADS))), state_ssd_conv,
                     state_ssd.reshape(DEPTH, bs, SSD_WIDTH, SSD_STATE), state_ret)
    st_s = None

    xp = x_prompt
    xs = x_sample.reshape(bs, d)
    new_p = [[] for _ in range(6)]
    for l in range(DEPTH):
        lw = {"norm_mix": norm_mix[l][None, :], "w_in": _pack_in_proj(w_in[l]), "w_out": w_out[l].astype(BF16),
              "bias": _lane_row([ml_i_bias[l], ml_f_bias[l], ssd_dt_bias[l]]),
              "alog": _lane_row([zeros12, ssd_a_log[l]]),
              "conv_w": ssd_conv_w[l], "conv_b": ssd_conv_b[l][None, :], "ml_norm": ml_norm[l][None, :],
              "ssd_d": jnp.repeat(ssd_d[l], SSD_HEADDIM)[None, :], "ssd_norm": ssd_norm[l][None, :],
              "ret_norm": ret_norm[l][None, :]}
        wq, wo = w_ca_q[l].astype(BF16), w_ca_o[l].astype(BF16)
        g_ca, g_ffn = norm_ca[l][None, :], norm_ffn[l][None, :]
        last = l == DEPTH - 1
        j = l // 2
        if l % 2 == 0:
            ffn_w = (ffn_w_gate[j].astype(BF16), ffn_w_up[j].astype(BF16), ffn_w_down[j].astype(BF16))
        else:
            wrt = jnp.pad(moe_w_router[j].T, ((0, 2 * N_EXPERTS - N_EXPERTS), (0, 0)))
            ffn_w = (wrt, moe_w_gate[j].astype(BF16), moe_w_up[j].astype(BF16), moe_w_down[j].astype(BF16))

        def run_ffn(x2d, tm):
            if l % 2 == 0:
                return ffn_dense(x2d, g_ffn, *ffn_w, tm=tm)
            return moe_ffn(x2d, g_ffn, *ffn_w, norm_final[None, :], tm=tm, sub=_moe_sub_rows(tm), final_norm=last)

        xp, c1, n1, m1, conv1, ssd1, ret1 = mixer_prompt(xp, lw, consts, tm=256)
        st_p = (c1, n1, m1[:, 0, :ML_HEADS], conv1,
                ssd1.reshape(bp, SSD_HEADS, SSD_HEADDIM, SSD_STATE), ret1)
        xp = ca_prompt(xp, g_ca, wq, wo, mem_k_p, mem_v_p, l)
        xp = run_ffn(xp.reshape(bp * seq, d), 1024 if l % 2 else 512).reshape(bp, seq, d)

        proj = norm_matmul(xs, lw["norm_mix"], lw["w_in"], tn=D_PROJ // 3)
        y, st_s = mixer_sample(proj, lw, consts, sample_states, l, st_s)
        xs = norm_matmul(y, ones_row, lw["w_out"], residual=xs, norm=False)
        q = norm_matmul(xs, g_ca, wq)
        o = ca_sample(q.reshape(bs, CA_HEADS, CA_DH), cache_mem_k, cache_mem_v, l).reshape(bs, d)
        xs = norm_matmul(o, ones_row[:, :d], wo, residual=xs, norm=False)
        xs = run_ffn(xs, bs)

        for lst, a in zip(new_p, st_p):
            lst.append(a)

    shape5 = (DEPTH, bp, MEM_LEN, CA_HEADS, CA_DH)
    c_s, n_s, m_s, conv_s, ssd_s, ret_s = st_s
    return (xp, xs.reshape(bs, 1, d),
            jnp.stack(new_p[0]), jnp.stack(new_p[1]), jnp.stack(new_p[2]), jnp.stack(new_p[3]),
            jnp.stack(new_p[4]), jnp.stack(new_p[5]), mem_k_p.reshape(shape5), mem_v_p.reshape(shape5),
            c_s, n_s, m_s[:, :, :ML_HEADS], conv_s,
            ssd_s.reshape(DEPTH, bs, SSD_HEADS, SSD_HEADDIM, SSD_STATE), ret_s)
```

```python
import functools
import itertools
import math

import numpy as np
import jax
import jax.numpy as jnp
from jax import lax
from jax.experimental import pallas as pl
from jax.experimental.pallas import tpu as pltpu

F32 = jnp.float32
BF16 = jnp.bfloat16
HIGHEST = lax.Precision.HIGHEST

D_MODEL = 1024
DEPTH = 2
PAST_LEN = 16384
D_MIX = 2 * D_MODEL
ML_WIDTH = 3 * D_MIX // 8
ML_HEADS = 6
ML_DH = ML_WIDTH // ML_HEADS
SSD_WIDTH = 3 * D_MIX // 8
SSD_HEADDIM = 64
SSD_HEADS = SSD_WIDTH // SSD_HEADDIM
SSD_STATE = 128
SSD_GROUPS = 2
SSD_GROUP_WIDTH = SSD_WIDTH // SSD_GROUPS
SSD_CONV_DIM = SSD_WIDTH + 2 * SSD_GROUPS * SSD_STATE
CONV_W = 4
RET_WIDTH = D_MIX // 4
RET_HEADS = 4
RET_DH = RET_WIDTH // RET_HEADS
ROPE_BASE = 10000.0
CHUNK = 128
MEM_LEN = 256
CA_HEADS = 4
CA_DH = D_MODEL // CA_HEADS
D_FF = 11 * D_MODEL // 4
N_EXPERTS = 8
TOP_K = 2
D_EXPERT = D_FF // 2
EPS = 1e-6

LANES = 128
BF16_ROWS = 16
MXU_ROWS = 128

OFF_QML = 0
OFF_KML = OFF_QML + ML_WIDTH
OFF_VML = OFF_KML + ML_WIDTH
OFF_OML = OFF_VML + ML_WIDTH
OFF_Z = OFF_OML + ML_WIDTH
OFF_XBC = OFF_Z + SSD_WIDTH
OFF_QR = OFF_XBC + SSD_CONV_DIM
OFF_KR = OFF_QR + RET_WIDTH
OFF_VR = OFF_KR + RET_WIDTH
OFF_GR = OFF_VR + RET_WIDTH
OFF_SMALL = OFF_GR + RET_WIDTH
D_PROJ = OFF_SMALL + LANES
LANE_I = 0
LANE_F = ML_HEADS
LANE_DT = 2 * ML_HEADS

VMEM_LIMIT = 56 * 1024 * 1024


def _cparams(sem):
    return pltpu.CompilerParams(dimension_semantics=sem, vmem_limit_bytes=VMEM_LIMIT)


def _const_spec(shape):
    nd = len(shape)
    return pl.BlockSpec(shape, lambda *_: (0,) * nd, pipeline_mode=pl.Buffered(1))


def _rms(x, g):
    return x * lax.rsqrt(jnp.mean(x * x, axis=-1, keepdims=True) + EPS) * g


def _sigmoid(x):
    return 1.0 / (1.0 + jnp.exp(-x))


def _silu(x):
    return x * _sigmoid(x)


def _dot(a, b):
    return jnp.dot(a, b, preferred_element_type=F32)


def _dot_nt(a, b):
    return lax.dot_general(a, b, (((1,), (1,)), ((), ())), preferred_element_type=F32)


def _dot_exact(a, b):
    return jnp.dot(a, b, preferred_element_type=F32, precision=HIGHEST)


def _split3(v):
    v1 = v.astype(BF16)
    r1 = v - v1.astype(F32)
    v2 = r1.astype(BF16)
    v3 = (r1 - v2.astype(F32)).astype(BF16)
    return v1, v2, v3


def _norm_matmul_kernel(*refs, norm, residual):
    x_ref, g_ref, w_ref = refs[:3]
    o_ref = refs[-1]
    x = x_ref[...].astype(F32)
    if norm:
        x = _rms(x, g_ref[...])
    acc = _dot(x.astype(BF16), w_ref[...])
    if residual:
        acc = acc + refs[3][...]
    o_ref[...] = acc


def norm_matmul(x, g, w, residual=None, *, norm=True, tm=None, tn=None):
    m, k = x.shape
    n = w.shape[1]
    tm = tm or min(m, 512)
    tn = tn or min(n, 1024)
    in_specs = [pl.BlockSpec((tm, k), lambda j, i: (i, 0)),
                pl.BlockSpec((1, k), lambda j, i: (0, 0)),
                pl.BlockSpec((k, tn), lambda j, i: (0, j))]
    args = [x, g, w]
    if residual is not None:
        in_specs.append(pl.BlockSpec((tm, tn), lambda j, i: (i, j)))
        args.append(residual)
    return pl.pallas_call(
        functools.partial(_norm_matmul_kernel, norm=norm, residual=residual is not None),
        grid=(pl.cdiv(n, tn), m // tm),
        in_specs=in_specs,
        out_specs=pl.BlockSpec((tm, tn), lambda j, i: (i, j)),
        out_shape=jax.ShapeDtypeStruct((m, n), F32),
        compiler_params=_cparams(("parallel", "parallel")),
        name="norm_matmul",
    )(*args)


def _kv_proj_kernel(x_ref, g_ref, wk_ref, wv_ref, k_ref, v_ref):
    xn = _rms(x_ref[...], g_ref[...]).astype(BF16)
    k = _dot(xn, wk_ref[...])
    v = _dot(xn, wv_ref[...])
    for h in range(CA_HEADS):
        k_ref[:, h, :] = k[:, h * CA_DH:(h + 1) * CA_DH]
        v_ref[:, h, :] = v[:, h * CA_DH:(h + 1) * CA_DH]


def kv_proj(mem2d, g, wk, wv, *, tm=512):
    m, d = mem2d.shape
    depth = wk.shape[0]
    out = jax.ShapeDtypeStruct((depth, m, CA_HEADS, CA_DH), F32)
    return pl.pallas_call(
        _kv_proj_kernel,
        grid=(depth, m // tm),
        in_specs=[pl.BlockSpec((tm, d), lambda l, i: (i, 0)),
                  pl.BlockSpec((None, 1, d), lambda l, i: (l, 0, 0)),
                  pl.BlockSpec((None, d, d), lambda l, i: (l, 0, 0)),
                  pl.BlockSpec((None, d, d), lambda l, i: (l, 0, 0))],
        out_specs=[pl.BlockSpec((None, tm, CA_HEADS, CA_DH), lambda l, i: (l, i, 0, 0)),
                   pl.BlockSpec((None, tm, CA_HEADS, CA_DH), lambda l, i: (l, i, 0, 0))],
        out_shape=[out, out],
        compiler_params=_cparams(("parallel", "parallel")),
        name="kv_proj",
    )(mem2d, g, wk, wv)


def _ca_prompt_kernel(x_ref, g_ref, wq_ref, wo_ref, k_ref, v_ref, o_ref):
    x = x_ref[...]
    xn = _rms(x, g_ref[...]).astype(BF16)
    q = _dot(xn, wq_ref[...])
    outs = []
    for h in range(CA_HEADS):
        sl = slice(h * CA_DH, (h + 1) * CA_DH)
        s = _dot_nt(q[:, sl].astype(BF16), k_ref[:, h, :].astype(BF16)) * (CA_DH ** -0.5)
        p = jnp.exp(s - jnp.max(s, axis=-1, keepdims=True))
        p = p / jnp.sum(p, axis=-1, keepdims=True)
        outs.append(_dot(p.astype(BF16), v_ref[:, h, :].astype(BF16)).astype(BF16))
    o = jnp.concatenate(outs, axis=-1)
    o_ref[...] = x + _dot(o, wo_ref[...])


def ca_prompt(x, g, wq, wo, mem_k, mem_v, layer, *, tm=512):
    b, length, d = x.shape
    kv_block = (None, None) + mem_k.shape[2:]
    return pl.pallas_call(
        _ca_prompt_kernel,
        grid=(b, length // tm),
        in_specs=[pl.BlockSpec((None, tm, d), lambda i, j: (i, j, 0)),
                  _const_spec((1, d)), _const_spec((d, d)), _const_spec((d, d)),
                  pl.BlockSpec(kv_block, lambda i, j: (layer, i, 0, 0, 0)),
                  pl.BlockSpec(kv_block, lambda i, j: (layer, i, 0, 0, 0))],
        out_specs=pl.BlockSpec((None, tm, d), lambda i, j: (i, j, 0)),
        out_shape=jax.ShapeDtypeStruct(x.shape, F32),
        compiler_params=_cparams(("parallel", "parallel")),
        name="ca_prompt",
    )(x, g, wq, wo, mem_k, mem_v)


def _ca_sample_kernel(q_ref, k_ref, v_ref, o_ref, *, bt):
    for j in range(bt):
        kq = k_ref[j] * q_ref[j]
        s = jnp.sum(kq, axis=-1, keepdims=True) * (CA_DH ** -0.5)
        p = jnp.exp(s - jnp.max(s, axis=0, keepdims=True))
        p = p / jnp.sum(p, axis=0, keepdims=True)
        o_ref[j] = jnp.sum(p * v_ref[j], axis=0)


def ca_sample(q, mem_k, mem_v, layer, *, bt=8):
    b = q.shape[0]
    blk = (None, bt) + mem_k.shape[2:]
    return pl.pallas_call(
        functools.partial(_ca_sample_kernel, bt=bt),
        grid=(b // bt,),
        in_specs=[pl.BlockSpec((bt,) + q.shape[1:], lambda i: (i, 0, 0)),
                  pl.BlockSpec(blk, lambda i: (layer, i, 0, 0, 0)),
                  pl.BlockSpec(blk, lambda i: (layer, i, 0, 0, 0))],
        out_specs=pl.BlockSpec((bt,) + q.shape[1:], lambda i: (i, 0, 0)),
        out_shape=jax.ShapeDtypeStruct(q.shape, F32),
        compiler_params=_cparams(("parallel",)),
        name="ca_sample",
    )(q, mem_k, mem_v)


def _ffn_kernel(x_ref, g_ref, wg_ref, wu_ref, wd_ref, o_ref, a_ref, *, fchunk):
    x = x_ref[...]
    xn = _rms(x, g_ref[...]).astype(BF16)
    dff = wg_ref.shape[1]
    for f0 in range(0, dff, fchunk):
        gate = _dot(xn, wg_ref[:, f0:f0 + fchunk])
        up = _dot(xn, wu_ref[:, f0:f0 + fchunk])
        a_ref[:, f0:f0 + fchunk] = (_silu(gate) * up).astype(BF16)
    o_ref[...] = x + _dot(a_ref[...], wd_ref[...])


def ffn_dense(x, g, wg, wu, wd, *, tm=512):
    m, d = x.shape
    dff = wg.shape[1]
    return pl.pallas_call(
        functools.partial(_ffn_kernel, fchunk=dff // 2),
        grid=(m // tm,),
        in_specs=[pl.BlockSpec((tm, d), lambda i: (i, 0)),
                  _const_spec((1, d)), _const_spec((d, dff)), _const_spec((d, dff)), _const_spec((dff, d))],
        out_specs=pl.BlockSpec((tm, d), lambda i: (i, 0)),
        out_shape=jax.ShapeDtypeStruct((m, d), F32),
        scratch_shapes=[pltpu.VMEM((tm, dff), BF16)],
        compiler_params=_cparams(("parallel",)),
        name="ffn_dense",
    )(x, g, wg, wu, wd)


def _moe_kernel(x_ref, g_ref, wrt_ref, wg_ref, wu_ref, wd_ref, gf_ref, o_ref,
                xn_ref, gate_t_ref, pos_t_ref, *, tm, sub, final_norm):
    e = pl.program_id(1)
    ne = pl.num_programs(1)
    epad = gate_t_ref.shape[0]

    @pl.when(e == 0)
    def _route():
        x = x_ref[...]
        xn = _rms(x, g_ref[...])
        xn_ref[...] = xn.astype(BF16)
        x_hi = xn.astype(BF16)
        x_lo = (xn - x_hi.astype(F32)).astype(BF16)
        w = wrt_ref[...]
        w_hi = w.astype(BF16)
        w_lo = (w - w_hi.astype(F32)).astype(BF16)
        logits = _dot_nt(w_hi, x_hi) + (_dot_nt(w_hi, x_lo) + _dot_nt(w_lo, x_hi))
        row = lax.broadcasted_iota(jnp.int32, (epad, tm), 0)
        logits = jnp.where(row < N_EXPERTS, logits, -jnp.inf)
        m1 = jnp.max(logits, axis=0, keepdims=True)
        i1 = jnp.min(jnp.where(logits == m1, row, epad), axis=0, keepdims=True)
        rest = jnp.where(row == i1, -jnp.inf, logits)
        m2 = jnp.max(rest, axis=0, keepdims=True)
        i2 = jnp.min(jnp.where(rest == m2, row, epad), axis=0, keepdims=True)
        e2 = jnp.exp(m2 - m1)
        den = 1.0 + e2
        gate_t = jnp.where(row == i1, 1.0 / den, 0.0) + jnp.where(row == i2, e2 / den, 0.0)
        sel_t = jnp.where((row == i1) | (row == i2), 1.0, 0.0)
        r_i = lax.broadcasted_iota(jnp.int32, (tm, tm), 0)
        c_i = lax.broadcasted_iota(jnp.int32, (tm, tm), 1)
        upper = jnp.where(r_i < c_i, 1.0, 0.0).astype(BF16)
        pos_t = _dot(sel_t.astype(BF16), upper)
        pos_t = jnp.where(sel_t > 0, pos_t, -1.0)
        gate_t_ref[...] = gate_t
        pos_t_ref[...] = pos_t
        o_ref[...] = x

    pos_row = pos_t_ref[pl.ds(e, 1), :]
    gate_row = gate_t_ref[pl.ds(e, 1), :]
    count = jnp.sum(jnp.where(pos_row >= 0, 1.0, 0.0)).astype(jnp.int32)
    half = sub // 2
    n_full = (count + (half - 1)) // sub
    done = n_full * sub

    def expert_rows(base, rows):
        slot = lax.broadcasted_iota(jnp.int32, (rows, tm), 0).astype(F32) + base.astype(F32)
        hit = pos_row == slot
        gather = jnp.where(hit, 1.0, 0.0).astype(BF16)
        gate = jnp.sum(jnp.where(hit, gate_row, 0.0), axis=-1, keepdims=True)
        xg = _dot(gather, xn_ref[...]).astype(BF16)
        act = (_silu(_dot(xg, wg_ref[...])) * _dot(xg, wu_ref[...])).astype(BF16)
        ye = (gate * _dot(act, wd_ref[...])).astype(BF16)
        o_ref[...] += lax.dot_general(gather, ye, (((0,), (0,)), ((), ())), preferred_element_type=F32)

    def full_body(j, carry):
        expert_rows(j * sub, sub)
        return carry

    lax.fori_loop(0, n_full, full_body, 0)

    @pl.when(count > done)
    def _remainder():
        expert_rows(done, half)

    if final_norm:
        @pl.when(e == ne - 1)
        def _final():
            o_ref[...] = _rms(o_ref[...], gf_ref[...])


def _moe_sub_rows(tm):
    mean = tm * TOP_K // N_EXPERTS
    return min(tm, max(MXU_ROWS, -(-mean // MXU_ROWS) * MXU_ROWS))


def moe_ffn(x, g, w_router_t, wg, wu, wd, g_final, *, tm, sub, final_norm):
    m, d = x.shape
    ne, _, de = wg.shape
    epad = w_router_t.shape[0]
    return pl.pallas_call(
        functools.partial(_moe_kernel, tm=tm, sub=sub, final_norm=final_norm),
        grid=(m // tm, ne),
        in_specs=[pl.BlockSpec((tm, d), lambda i, e: (i, 0)),
                  pl.BlockSpec((1, d), lambda i, e: (0, 0)),
                  pl.BlockSpec((epad, d), lambda i, e: (0, 0)),
                  pl.BlockSpec((None, d, de), lambda i, e: (e, 0, 0)),
                  pl.BlockSpec((None, d, de), lambda i, e: (e, 0, 0)),
                  pl.BlockSpec((None, de, d), lambda i, e: (e, 0, 0)),
                  pl.BlockSpec((1, d), lambda i, e: (0, 0))],
        out_specs=pl.BlockSpec((tm, d), lambda i, e: (i, 0)),
        out_shape=jax.ShapeDtypeStruct((m, d), F32),
        scratch_shapes=[pltpu.VMEM((tm, d), BF16),
                        pltpu.VMEM((epad, tm), F32), pltpu.VMEM((epad, tm), F32)],
        compiler_params=_cparams(("parallel", "arbitrary")),
        name="moe_ffn",
    )(x, g, w_router_t, wg, wu, wd, g_final)


def _retention_constants(c):
    log_g = np.log1p(-np.exp2(-5.0 - np.arange(RET_HEADS, dtype=np.float32))).astype(np.float32)
    j = np.arange(c, dtype=np.float32)
    rel = j[:, None] - j[None, :]
    w_intra = np.where(rel >= 0, np.exp(rel[None] * log_g[:, None, None]), 0.0).astype(np.float32)
    w_inter = np.exp((j[None, :] + 1.0) * log_g[:, None]).astype(np.float32)
    w_end = np.exp((c - 1.0 - j[None, :]) * log_g[:, None]).astype(np.float32)
    decay = np.exp(c * log_g).astype(np.float32)
    cols = np.zeros((c, LANES), np.float32)
    cols[:, :RET_HEADS] = w_inter.T
    cols[:, RET_HEADS:2 * RET_HEADS] = w_end.T
    return w_intra, cols, [float(d) for d in decay]


def _rotary_tables(pos):
    half = RET_DH // 2
    inv = ROPE_BASE ** (-jnp.arange(half, dtype=F32) / half)
    ang = pos[:, None] * inv[None, :]
    cos, sin = jnp.cos(ang), jnp.sin(ang)
    return jnp.concatenate([cos, cos], axis=-1), jnp.concatenate([-sin, sin], axis=-1)


def _head_expand_matrix():
    e = np.zeros((LANES, SSD_WIDTH), np.float32)
    for h in range(SSD_HEADS):
        e[LANE_DT + h, h * SSD_HEADDIM:(h + 1) * SSD_HEADDIM] = 1.0
    return e


def _softplus_parts(v):
    sp = jnp.log1p(jnp.exp(-jnp.abs(v)))
    return jnp.minimum(v, 0.0) - sp, jnp.maximum(v, 0.0) + sp


def _mixer_prompt_kernel(x_ref, xnext_ref, g_ref, win_ref, wout_ref, bias_ref, alog_ref, convw_ref, convb_ref,
                         mlg_ref, ssdd_ref, ssdg_ref, retg_ref, cos_ref, sin_ref, retw_ref, retcol_ref,
                         expand_ref,
                         xo_ref, c_ref, n_ref, m_ref, conv_ref, ssd_ref, ret_ref,
                         proj_s, y_s, u_s, *, tm, tiles_per_seq, ret_decay):
    c = CHUNK
    i = pl.program_id(0)

    @pl.when(i % tiles_per_seq == 0)
    def _init():
        c_ref[...] = jnp.zeros_like(c_ref)
        n_ref[...] = jnp.zeros_like(n_ref)
        m_ref[...] = jnp.zeros_like(m_ref)
        ssd_ref[...] = jnp.zeros_like(ssd_ref)
        ret_ref[...] = jnp.zeros_like(ret_ref)
        u_s[0:8, :] = jnp.zeros((8, SSD_CONV_DIM), F32)

    seg_w = 512

    def project(src_ref, slot):
        xn = _rms(src_ref[...], g_ref[...]).astype(BF16)
        for off in range(0, D_PROJ, seg_w):
            wdt = min(seg_w, D_PROJ - off)
            proj_s[slot, :, off:off + wdt] = _dot(xn, win_ref[:, off:off + wdt])
            yield

    @pl.when(i == 0)
    def _first():
        for _ in project(x_ref, 0):
            pass

    proj_cur = proj_s.at[i % 2]
    x = x_ref[...]

    r_i = lax.broadcasted_iota(jnp.int32, (c, c), 0)
    c_i = lax.broadcasted_iota(jnp.int32, (c, c), 1)
    causal = r_i >= c_i
    tril = jnp.where(causal, 1.0, 0.0).astype(BF16)
    lane1 = lax.broadcasted_iota(jnp.int32, (1, LANES), 1)
    lane_c = lax.broadcasted_iota(jnp.int32, (c, LANES), 1)
    a_vec = -jnp.exp(alog_ref[...])
    expand3 = expand_ref[...]
    bias = bias_ref[...]

    def cumsum_rows(v):
        p = _dot(tril, jnp.concatenate(_split3(v), axis=-1))
        return p[:, :LANES] + p[:, LANES:2 * LANES] + p[:, 2 * LANES:]

    def expand_heads(v):
        return _dot(jnp.concatenate(_split3(v), axis=-1), expand3)

    def chunk_body(ci):
        rows = slice(ci * c, (ci + 1) * c)
        small = proj_cur[rows, OFF_SMALL:OFF_SMALL + LANES] + bias
        logf, dt = _softplus_parts(small)
        is_f = (lane_c >= LANE_F) & (lane_c < LANE_DT)
        is_dt = (lane_c >= LANE_DT) & (lane_c < LANE_DT + SSD_HEADS)
        incr = jnp.where(is_f, logf, jnp.where(is_dt, dt * a_vec, 0.0))
        cs = cumsum_rows(incr)
        cs_t = cs.T
        small_t = small.T

        m_vec = m_ref[...]
        m_out = m_vec
        for h in range(ML_HEADS):
            hs = slice(h * ML_DH, (h + 1) * ML_DH)
            b_col = cs[:, LANE_F + h:LANE_F + h + 1]
            b_row = cs_t[LANE_F + h:LANE_F + h + 1, :]
            i_col = small[:, LANE_I + h:LANE_I + h + 1]
            i_row = small_t[LANE_I + h:LANE_I + h + 1, :]
            m_prev = m_vec[:, h:h + 1]
            lw = jnp.where(causal, b_col + (i_row - b_row), -jnp.inf)
            log_a = b_col + m_prev
            m_t = jnp.maximum(log_a, jnp.max(lw, axis=-1, keepdims=True))
            w_intra = jnp.exp(lw - m_t)
            w_inter = jnp.exp(log_a - m_t)
            q = proj_cur[rows, OFF_QML + h * ML_DH:OFF_QML + (h + 1) * ML_DH]
            k = proj_cur[rows, OFF_KML + h * ML_DH:OFF_KML + (h + 1) * ML_DH] * (ML_DH ** -0.5)
            v = proj_cur[rows, OFF_VML + h * ML_DH:OFF_VML + (h + 1) * ML_DH]
            og = proj_cur[rows, OFF_OML + h * ML_DH:OFF_OML + (h + 1) * ML_DH]
            qb, kb, vb = q.astype(BF16), k.astype(BF16), v.astype(BF16)
            s = w_intra * _dot_nt(qb, kb)
            cm = c_ref[h]
            n_row = n_ref[h:h + 1, :]
            num = _dot(s.astype(BF16), vb) + w_inter * _dot_nt(qb, cm.astype(BF16))
            den = jnp.sum(s, axis=-1, keepdims=True) + w_inter * jnp.sum(q * n_row, axis=-1, keepdims=True)
            den = jnp.maximum(jnp.abs(den), jnp.exp(-m_t))
            hh = num / den
            m_new = m_t[c - 1:c, :]
            decay = jnp.exp(log_a[c - 1:c, :] - m_new)
            w_end = jnp.exp(b_col[c - 1:c, :] - b_col + i_col - m_new)
            c_ref[h] = decay * cm + _dot((v * w_end).T.astype(BF16), kb)
            n_ref[h:h + 1, :] = decay * n_row + jnp.sum(k * w_end, axis=0, keepdims=True)
            m_out = jnp.where(lane1 == h, m_new, m_out)
            hn = hh * lax.rsqrt(jnp.mean(hh * hh, axis=-1, keepdims=True) + EPS)
            y_s[rows, hs] = (hn * mlg_ref[:, hs] * _sigmoid(og)).astype(BF16)
            yield
        m_ref[...] = m_out

        u = proj_cur[rows, OFF_XBC:OFF_XBC + SSD_CONV_DIM]
        u_s[8:8 + c, :] = u
        conv = convb_ref[...] + convw_ref[3:4, :] * u
        for jj in range(CONV_W - 1):
            conv = conv + convw_ref[jj:jj + 1, :] * u_s[5 + jj:5 + jj + c, :]
        u_s[0:8, :] = u_s[c:c + 8, :]
        xc = _silu(conv)
        xs = xc[:, :SSD_WIDTH]
        dt_exp = expand_heads(jnp.where(is_dt, dt, 0.0))
        acum_exp = expand_heads(jnp.where(is_dt, cs, 0.0))
        xdt = xs * dt_exp
        w_end_exp = jnp.exp(acum_exp[c - 1:c, :] - acum_exp)
        xw = xdt * w_end_exp
        ea_exp = jnp.exp(acum_exp)
        lane_lo = lane_c < SSD_HEADDIM
        hpg = SSD_HEADS // SSD_GROUPS
        y_cols = []
        yield
        for g in range(SSD_GROUPS):
            gw = slice(g * SSD_GROUP_WIDTH, (g + 1) * SSD_GROUP_WIDTH)
            bg = xc[:, SSD_WIDTH + g * SSD_STATE:SSD_WIDTH + (g + 1) * SSD_STATE].astype(BF16)
            cg = xc[:, SSD_WIDTH + (SSD_GROUPS + g) * SSD_STATE:
                    SSD_WIDTH + (SSD_GROUPS + g + 1) * SSD_STATE].astype(BF16)
            cb = _dot_nt(cg, bg)
            st = ssd_ref[gw, :]
            inter = _dot_nt(cg, st.astype(BF16)) * ea_exp[:, gw]
            for pr in range(hpg // 2):
                h0 = g * hpg + 2 * pr
                xp = xdt[:, h0 * SSD_HEADDIM:(h0 + 2) * SSD_HEADDIM].astype(BF16)
                pair = []
                for hidx in (h0, h0 + 1):
                    a_col = cs[:, LANE_DT + hidx:LANE_DT + hidx + 1]
                    a_row = cs_t[LANE_DT + hidx:LANE_DT + hidx + 1, :]
                    dec = jnp.exp(jnp.where(causal, a_col - a_row, -jnp.inf))
                    pair.append(_dot((dec * cb).astype(BF16), xp))
                y_cols.append(jnp.where(lane_lo, pair[0], pair[1]) + inter[:, pr * LANES:(pr + 1) * LANES])
                yield
            upd = _dot(xw[:, gw].T.astype(BF16), bg)
            for r in range(hpg):
                hidx = g * hpg + r
                scal = jnp.exp(cs[c - 1:c, LANE_DT + hidx:LANE_DT + hidx + 1])
                hr = slice(hidx * SSD_HEADDIM, (hidx + 1) * SSD_HEADDIM)
                ssd_ref[hr, :] = scal * ssd_ref[hr, :] + upd[r * SSD_HEADDIM:(r + 1) * SSD_HEADDIM, :]
            yield
        ys = jnp.concatenate(y_cols, axis=-1) + ssdd_ref[...] * xs
        yz = ys * _silu(proj_cur[rows, OFF_Z:OFF_Z + SSD_WIDTH])
        for g in range(SSD_GROUPS):
            gw = slice(g * SSD_GROUP_WIDTH, (g + 1) * SSD_GROUP_WIDTH)
            seg = yz[:, gw]
            seg = seg * lax.rsqrt(jnp.mean(seg * seg, axis=-1, keepdims=True) + EPS) * ssdg_ref[:, gw]
            y_s[rows, ML_WIDTH + g * SSD_GROUP_WIDTH:ML_WIDTH + (g + 1) * SSD_GROUP_WIDTH] = seg.astype(BF16)
        yield

        cosb = cos_ref[rows, :]
        sinb = sin_ref[rows, :]
        retcol = retcol_ref[...]
        for h in range(RET_HEADS):
            hs = slice(h * RET_DH, (h + 1) * RET_DH)
            q = proj_cur[rows, OFF_QR + h * RET_DH:OFF_QR + (h + 1) * RET_DH]
            k = proj_cur[rows, OFF_KR + h * RET_DH:OFF_KR + (h + 1) * RET_DH]
            v = proj_cur[rows, OFF_VR + h * RET_DH:OFF_VR + (h + 1) * RET_DH]
            gr = proj_cur[rows, OFF_GR + h * RET_DH:OFF_GR + (h + 1) * RET_DH]
            qr = q * cosb + pltpu.roll(q, RET_DH // 2, 1) * sinb
            kr = (k * cosb + pltpu.roll(k, RET_DH // 2, 1) * sinb) * (RET_DH ** -0.5)
            qb, kb, vb = qr.astype(BF16), kr.astype(BF16), v.astype(BF16)
            att = _dot_nt(qb, kb) * retw_ref[h]
            sm = ret_ref[h]
            o = _dot(att.astype(BF16), vb) + _dot(qb, sm.astype(BF16)) * retcol[:, h:h + 1]
            kw = kr * retcol[:, RET_HEADS + h:RET_HEADS + h + 1]
            ret_ref[h] = ret_decay[h] * sm + _dot(kw.T.astype(BF16), vb)
            on = o * lax.rsqrt(jnp.mean(o * o, axis=-1, keepdims=True) + EPS)
            y_s[rows, ML_WIDTH + SSD_WIDTH + h * RET_DH:ML_WIDTH + SSD_WIDTH + (h + 1) * RET_DH] = (
                on * retg_ref[:, hs] * _silu(gr)).astype(BF16)
            yield

    stages_per_chunk = ML_HEADS + 1 + SSD_GROUPS * (SSD_HEADS // SSD_GROUPS // 2 + 1) + 1 + RET_HEADS
    n_segments = -(-D_PROJ // seg_w)
    every = max(1, (tm // c) * stages_per_chunk // n_segments)
    segments = project(xnext_ref, (i + 1) % 2)
    for k, _ in enumerate(itertools.chain.from_iterable(chunk_body(ci) for ci in range(tm // c))):
        if k % every == 0:
            next(segments, None)
    for _ in segments:
        pass
    xo_ref[...] = x + _dot(y_s[...], wout_ref[...])
    conv_ref[...] = u_s[5:8, :]


def mixer_prompt(x, lw, consts, *, tm):
    b, length, d = x.shape
    c = CHUNK
    tps = length // tm
    n_tiles = b * tps
    x2d = x.reshape(b * length, d)
    row = lambda w: _const_spec((1, w))
    state_spec = lambda *shape: pl.BlockSpec((None,) + shape, lambda i: (i // tps,) + (0,) * len(shape))
    out_shapes = [jax.ShapeDtypeStruct(x2d.shape, F32),
                  jax.ShapeDtypeStruct((b, ML_HEADS, ML_DH, ML_DH), F32),
                  jax.ShapeDtypeStruct((b, ML_HEADS, ML_DH), F32),
                  jax.ShapeDtypeStruct((b, 1, LANES), F32),
                  jax.ShapeDtypeStruct((b, CONV_W - 1, SSD_CONV_DIM), F32),
                  jax.ShapeDtypeStruct((b, SSD_WIDTH, SSD_STATE), F32),
                  jax.ShapeDtypeStruct((b, RET_HEADS, RET_DH, RET_DH), F32)]
    outs = pl.pallas_call(
        functools.partial(_mixer_prompt_kernel, tm=tm, tiles_per_seq=tps, ret_decay=consts["ret_decay"]),
        grid=(n_tiles,),
        in_specs=[pl.BlockSpec((tm, d), lambda i: (i, 0)),
                  pl.BlockSpec((tm, d), lambda i: (jnp.minimum(i + 1, n_tiles - 1), 0)),
                  row(d), _const_spec((d, D_PROJ)), _const_spec((D_MIX, d)),
                  row(LANES), row(LANES), _const_spec((CONV_W, SSD_CONV_DIM)), row(SSD_CONV_DIM),
                  row(ML_WIDTH), row(SSD_WIDTH), row(SSD_WIDTH), row(RET_WIDTH),
                  pl.BlockSpec((tm, LANES), lambda i: (i % tps, 0)),
                  pl.BlockSpec((tm, LANES), lambda i: (i % tps, 0)),
                  _const_spec((RET_HEADS, c, c)), _const_spec((c, LANES)),
                  _const_spec((3 * LANES, SSD_WIDTH))],
        out_specs=[pl.BlockSpec((tm, d), lambda i: (i, 0)),
                   state_spec(ML_HEADS, ML_DH, ML_DH), state_spec(ML_HEADS, ML_DH), state_spec(1, LANES),
                   state_spec(CONV_W - 1, SSD_CONV_DIM), state_spec(SSD_WIDTH, SSD_STATE),
                   state_spec(RET_HEADS, RET_DH, RET_DH)],
        out_shape=out_shapes,
        scratch_shapes=[pltpu.VMEM((2, tm, D_PROJ), F32), pltpu.VMEM((tm, D_MIX), BF16),
                        pltpu.VMEM((c + 8, SSD_CONV_DIM), F32)],
        compiler_params=_cparams(("arbitrary",)),
        name="mixer_prompt",
    )(x2d, x2d, lw["norm_mix"], lw["w_in"], lw["w_out"], lw["bias"], lw["alog"], lw["conv_w"], lw["conv_b"],
      lw["ml_norm"], lw["ssd_d"], lw["ssd_norm"], lw["ret_norm"], consts["cos_p"], consts["sin_p"],
      consts["ret_w"], consts["ret_cols"], consts["expand3"])
    return [outs[0].reshape(b, length, d)] + list(outs[1:])


N_SAMPLE_STATES = 6


def _mixer_sample_kernel(*refs, bt, ret_gamma, n_alias):
    (proj_ref, small_ref, bias_ref, alog_ref, convw_ref, convb_ref, mlg_ref, ssdd_ref,
     ssdg_ref, retg_ref, cos_ref, sin_ref, expand_ref,
     c_in, n_in, m_in, conv_in, ssd_in, ret_in) = refs[:19]
    (y_ref, c_out, n_out, m_out, conv_out, ssd_out, ret_out,
     wi_s, wt_s, en_s, dtx_s, eax_s) = refs[19 + n_alias:]
    small = small_ref[...] + bias_ref[...]
    logf, dt = _softplus_parts(small)
    lane_b = lax.broadcasted_iota(jnp.int32, (bt, LANES), 1)
    is_dt = (lane_b >= LANE_DT) & (lane_b < LANE_DT + SSD_HEADS)
    log_a = pltpu.roll(logf, LANES - LANE_F, 1) + m_in[...]
    m_t = jnp.maximum(log_a, small)
    m_out[...] = m_t
    wi_s[...] = jnp.exp(small - m_t)
    wt_s[...] = jnp.exp(log_a - m_t)
    en_s[...] = jnp.exp(-m_t)
    a_vec = -jnp.exp(alog_ref[...])
    expand = expand_ref[...]
    dtx_s[...] = _dot_exact(jnp.where(is_dt, dt, 0.0), expand)
    eax_s[...] = jnp.exp(_dot_exact(jnp.where(is_dt, dt * a_vec, 0.0), expand))

    sub16 = lax.broadcasted_iota(jnp.int32, (BF16_ROWS, LANES), 0)
    row_hi = lax.broadcasted_iota(jnp.int32, (LANES, LANES), 0) < SSD_HEADDIM

    def rows3(a, b, c):
        return jnp.where(sub16 == 0, a, jnp.where(sub16 == 1, b, jnp.where(sub16 == 2, c, 0.0))).astype(BF16)

    def split(v):
        hi = v.astype(BF16).astype(F32)
        return hi, v - hi

    def outer(a, b):
        a_hi, a_lo = split(a)
        b_hi, b_lo = split(b)
        return lax.dot_general(rows3(a_hi, a_hi, a_lo), rows3(b_hi, b_lo, b_hi), (((0,), (0,)), ((), ())),
                               preferred_element_type=F32)

    def row1(a):
        return jnp.where(sub16 == 0, a, 0.0).astype(BF16)

    cosb = cos_ref[...]
    sinb = sin_ref[...]
    hpg = SSD_HEADS // SSD_GROUPS

    sub_l = lax.broadcasted_iota(jnp.int32, (bt, LANES), 0)
    sub_w = lax.broadcasted_iota(jnp.int32, (bt, SSD_WIDTH), 0)

    def body(j, carry):
        pick_l = lambda ref: jnp.sum(jnp.where(sub_l == j, ref[...], 0.0), axis=0, keepdims=True)
        pick_w = lambda ref: jnp.sum(jnp.where(sub_w == j, ref[...], 0.0), axis=0, keepdims=True)
        wi = pick_l(wi_s)
        wt = pick_l(wt_s)
        en = pick_l(en_s)
        for h in range(ML_HEADS):
            hs = slice(h * ML_DH, (h + 1) * ML_DH)
            q = proj_ref[j, :,OFF_QML + h * ML_DH:OFF_QML + (h + 1) * ML_DH]
            k = proj_ref[j, :,OFF_KML + h * ML_DH:OFF_KML + (h + 1) * ML_DH] * (ML_DH ** -0.5)
            v = proj_ref[j, :,OFF_VML + h * ML_DH:OFF_VML + (h + 1) * ML_DH]
            og = proj_ref[j, :,OFF_OML + h * ML_DH:OFF_OML + (h + 1) * ML_DH]
            w_in_h = wi[:, h:h + 1]
            w_tr_h = wt[:, h:h + 1]
            c_new = w_tr_h * c_in[j, h] + w_in_h * outer(v, k)
            n_new = w_tr_h * n_in[j, h:h + 1, :] + w_in_h * k
            c_out[j, h] = c_new
            n_out[j, h:h + 1, :] = n_new
            num = _dot_nt(row1(q), c_new.astype(BF16))[0:1, :]
            den = jnp.sum(n_new * q, axis=-1, keepdims=True)
            den = jnp.maximum(jnp.abs(den), en[:, h:h + 1])
            hh = num / den
            hn = hh * lax.rsqrt(jnp.mean(hh * hh, axis=-1, keepdims=True) + EPS)
            y_ref[j, :,hs] = hn * mlg_ref[:, hs] * _sigmoid(og)
        u = proj_ref[j, :,OFF_XBC:OFF_XBC + SSD_CONV_DIM]
        prev = conv_in[j]
        conv = convb_ref[...] + convw_ref[CONV_W - 1:CONV_W, :] * u
        for jj in range(CONV_W - 1):
            conv = conv + convw_ref[jj:jj + 1, :] * prev[jj:jj + 1, :]
        conv_out[j, 0:CONV_W - 2, :] = prev[1:CONV_W - 1, :]
        conv_out[j, CONV_W - 2:CONV_W - 1, :] = u
        xc = _silu(conv)
        xs = xc[:, :SSD_WIDTH]
        xdt = xs * pick_w(dtx_s)
        ea = pick_w(eax_s)
        y_cols = []
        for g in range(SSD_GROUPS):
            bg = xc[:, SSD_WIDTH + g * SSD_STATE:SSD_WIDTH + (g + 1) * SSD_STATE]
            cg = xc[:, SSD_WIDTH + (SSD_GROUPS + g) * SSD_STATE:SSD_WIDTH + (SSD_GROUPS + g + 1) * SSD_STATE]
            for pr in range(hpg // 2):
                h0 = g * hpg + 2 * pr
                pw = slice(h0 * SSD_HEADDIM, (h0 + 2) * SSD_HEADDIM)
                lo = h0 * SSD_HEADDIM
                decay = jnp.where(row_hi, ea[:, lo:lo + 1], ea[:, lo + SSD_HEADDIM:lo + SSD_HEADDIM + 1])
                st_new = decay * ssd_in[j, pw, :] + outer(xdt[:, pw], bg)
                ssd_out[j, pw, :] = st_new
                y_cols.append(_dot_nt(row1(cg), st_new.astype(BF16))[0:1, :])
        ys = jnp.concatenate(y_cols, axis=-1) + ssdd_ref[...] * xs
        yz = ys * _silu(proj_ref[j, :,OFF_Z:OFF_Z + SSD_WIDTH])
        for g in range(SSD_GROUPS):
            gw = slice(g * SSD_GROUP_WIDTH, (g + 1) * SSD_GROUP_WIDTH)
            seg = yz[:, gw]
            y_ref[j, :,ML_WIDTH + g * SSD_GROUP_WIDTH:ML_WIDTH + (g + 1) * SSD_GROUP_WIDTH] = (
                seg * lax.rsqrt(jnp.mean(seg * seg, axis=-1, keepdims=True) + EPS) * ssdg_ref[:, gw])
        for h in range(RET_HEADS):
            hs = slice(h * RET_DH, (h + 1) * RET_DH)
            q = proj_ref[j, :,OFF_QR + h * RET_DH:OFF_QR + (h + 1) * RET_DH]
            k = proj_ref[j, :,OFF_KR + h * RET_DH:OFF_KR + (h + 1) * RET_DH]
            v = proj_ref[j, :,OFF_VR + h * RET_DH:OFF_VR + (h + 1) * RET_DH]
            gr = proj_ref[j, :,OFF_GR + h * RET_DH:OFF_GR + (h + 1) * RET_DH]
            qr = q * cosb + pltpu.roll(q, RET_DH // 2, 1) * sinb
            kr = (k * cosb + pltpu.roll(k, RET_DH // 2, 1) * sinb) * (RET_DH ** -0.5)
            s_new = ret_gamma[h] * ret_in[j, h] + outer(kr, v)
            ret_out[j, h] = s_new
            o = _dot(row1(qr), s_new.astype(BF16))[0:1, :]
            on = o * lax.rsqrt(jnp.mean(o * o, axis=-1, keepdims=True) + EPS)
            y_ref[j, :,ML_WIDTH + SSD_WIDTH + h * RET_DH:ML_WIDTH + SSD_WIDTH + (h + 1) * RET_DH] = (
                on * retg_ref[:, hs] * _silu(gr))
        return carry

    lax.fori_loop(0, bt, body, 0)


def mixer_sample(proj, lw, consts, states, layer, prev_out=None, *, bt=8):
    b = proj.shape[0]
    row = lambda w: pl.BlockSpec((1, w), lambda i: (0, 0))
    blk = lambda *shape: pl.BlockSpec((bt,) + shape, lambda i: (i,) + (0,) * len(shape))
    lblk = lambda *shape: pl.BlockSpec((None, bt) + shape, lambda i: (layer, i) + (0,) * len(shape))
    state_specs = [lblk(ML_HEADS, ML_DH, ML_DH), lblk(ML_HEADS, ML_DH), lblk(LANES),
                   lblk(CONV_W - 1, SSD_CONV_DIM), lblk(SSD_WIDTH, SSD_STATE), lblk(RET_HEADS, RET_DH, RET_DH)]
    state_shapes = [jax.ShapeDtypeStruct(s.shape, F32) for s in states]
    proj3 = proj.reshape(b, 1, D_PROJ)
    small = proj[:, OFF_SMALL:OFF_SMALL + LANES]
    n_fixed = 13 + N_SAMPLE_STATES
    alias_args = list(prev_out) if prev_out is not None else []
    aliases = {n_fixed + k: 1 + k for k in range(len(alias_args))}
    outs = pl.pallas_call(
        functools.partial(_mixer_sample_kernel, bt=bt, ret_gamma=consts["ret_gamma"], n_alias=len(alias_args)),
        grid=(b // bt,),
        in_specs=[blk(1, D_PROJ), blk(LANES), row(LANES), row(LANES),
                  pl.BlockSpec((CONV_W, SSD_CONV_DIM), lambda i: (0, 0)), row(SSD_CONV_DIM),
                  row(ML_WIDTH), row(SSD_WIDTH), row(SSD_WIDTH), row(RET_WIDTH), row(LANES), row(LANES),
                  pl.BlockSpec((LANES, SSD_WIDTH), lambda i: (0, 0))] + state_specs
                 + [pl.BlockSpec(memory_space=pl.ANY)] * len(alias_args),
        out_specs=[blk(1, D_MIX)] + state_specs,
        out_shape=[jax.ShapeDtypeStruct((b, 1, D_MIX), F32)] + state_shapes,
        input_output_aliases=aliases,
        scratch_shapes=[pltpu.VMEM((bt, LANES), F32)] * 3 + [pltpu.VMEM((bt, SSD_WIDTH), F32)] * 2,
        compiler_params=_cparams(("parallel",)),
        name="mixer_sample",
    )(proj3, small, lw["bias"], lw["alog"], lw["conv_w"], lw["conv_b"], lw["ml_norm"], lw["ssd_d"],
      lw["ssd_norm"], lw["ret_norm"], consts["cos_s"], consts["sin_s"], consts["expand"],
      *states, *alias_args)
    return outs[0].reshape(b, D_MIX), list(outs[1:])


def _pack_in_proj(w):
    sizes = [ML_WIDTH] * 4 + [ML_HEADS, ML_HEADS, SSD_WIDTH, SSD_CONV_DIM, SSD_HEADS] + [RET_WIDTH] * 4
    offs = np.concatenate([[0], np.cumsum(sizes)])
    seg = lambda i: w[:, int(offs[i]):int(offs[i + 1])]
    small = jnp.concatenate([seg(4), seg(5), seg(8)], axis=1)
    small = jnp.pad(small, ((0, 0), (0, LANES - small.shape[1])))
    cols = [seg(0), seg(1), seg(2), seg(3), seg(6), seg(7), seg(9), seg(10), seg(11), seg(12), small]
    return jnp.concatenate(cols, axis=1).astype(BF16)


def _lane_row(parts):
    v = jnp.concatenate([p.astype(F32) for p in parts])
    return jnp.pad(v, (0, LANES - v.shape[0]))[None, :]


def kernel(x_prompt, x_sample, state_mlstm_c, state_mlstm_n, state_mlstm_m, state_ssd_conv, state_ssd, state_ret,
           cache_mem_k, cache_mem_v, mem_prompt,
           norm_mix, w_in, ml_i_bias, ml_f_bias, ml_norm, ssd_conv_w, ssd_conv_b, ssd_dt_bias, ssd_a_log, ssd_d,
           ssd_norm, ret_norm, w_out, norm_ca, norm_mem, w_ca_q, w_ca_k, w_ca_v, w_ca_o, norm_ffn,
           ffn_w_gate, ffn_w_up, ffn_w_down, moe_w_router, moe_w_gate, moe_w_up, moe_w_down, norm_final):
    bp, seq, d = x_prompt.shape
    bs = x_sample.shape[0]
    assert x_sample.shape[1] == 1 and seq % CHUNK == 0 and DEPTH % 2 == 0

    ret_w, ret_cols, ret_decay = _retention_constants(CHUNK)
    _, _, ret_gamma = _retention_constants(1)
    cos_p, sin_p = _rotary_tables(jnp.arange(seq, dtype=F32))
    cos_s, sin_s = _rotary_tables(PAST_LEN + jnp.arange(1, dtype=F32))
    consts = {"ret_w": jnp.asarray(ret_w), "ret_cols": jnp.asarray(ret_cols), "ret_decay": ret_decay,
              "ret_gamma": ret_gamma, "cos_p": cos_p, "sin_p": sin_p, "cos_s": cos_s, "sin_s": sin_s,
              "expand": jnp.asarray(_head_expand_matrix()),
              "expand3": jnp.asarray(np.tile(_head_expand_matrix(), (3, 1))).astype(BF16)}
    zeros12 = jnp.zeros((2 * ML_HEADS,), F32)
    ones_row = jnp.ones((1, D_MIX), F32)

    mem2d = mem_prompt.reshape(bp * MEM_LEN, d)
    mem_k_p, mem_v_p = kv_proj(mem2d, norm_mem[:, None, :], w_ca_k.astype(BF16), w_ca_v.astype(BF16))

    mem_k_p = mem_k_p.reshape(DEPTH, bp, MEM_LEN, CA_HEADS, CA_DH)
    mem_v_p = mem_v_p.reshape(DEPTH, bp, MEM_LEN, CA_HEADS, CA_DH)
    sample_states = (state_mlstm_c, state_mlstm_n,
                     jnp.pad(state_mlstm_m, ((0, 0), (0, 0), (0, LANES - ML_HEADS))), state_ssd_conv,
                     state_ssd.reshape(DEPTH, bs, SSD_WIDTH, SSD_STATE), state_ret)
    st_s = None

    xp = x_prompt
    xs = x_sample.reshape(bs, d)
    new_p = [[] for _ in range(6)]
    for l in range(DEPTH):
        lw = {"norm_mix": norm_mix[l][None, :], "w_in": _pack_in_proj(w_in[l]), "w_out": w_out[l].astype(BF16),
              "bias": _lane_row([ml_i_bias[l], ml_f_bias[l], ssd_dt_bias[l]]),
              "alog": _lane_row([zeros12, ssd_a_log[l]]),
              "conv_w": ssd_conv_w[l], "conv_b": ssd_conv_b[l][None, :], "ml_norm": ml_norm[l][None, :],
              "ssd_d": jnp.repeat(ssd_d[l], SSD_HEADDIM)[None, :], "ssd_norm": ssd_norm[l][None, :],
              "ret_norm": ret_norm[l][None, :]}
        wq, wo = w_ca_q[l].astype(BF16), w_ca_o[l].astype(BF16)
        g_ca, g_ffn = norm_ca[l][None, :], norm_ffn[l][None, :]
        last = l == DEPTH - 1
        j = l // 2
        if l % 2 == 0:
            ffn_w = (ffn_w_gate[j].astype(BF16), ffn_w_up[j].astype(BF16), ffn_w_down[j].astype(BF16))
        else:
            wrt = jnp.pad(moe_w_router[j].T, ((0, 2 * N_EXPERTS - N_EXPERTS), (0, 0)))
            ffn_w = (wrt, moe_w_gate[j].astype(BF16), moe_w_up[j].astype(BF16), moe_w_down[j].astype(BF16))

        def run_ffn(x2d, tm):
            if l % 2 == 0:
                return ffn_dense(x2d, g_ffn, *ffn_w, tm=tm)
            return moe_ffn(x2d, g_ffn, *ffn_w, norm_final[None, :], tm=tm, sub=_moe_sub_rows(tm), final_norm=last)

        xp, c1, n1, m1, conv1, ssd1, ret1 = mixer_prompt(xp, lw, consts, tm=256)
        st_p = (c1, n1, m1[:, 0, :ML_HEADS], conv1,
                ssd1.reshape(bp, SSD_HEADS, SSD_HEADDIM, SSD_STATE), ret1)
        xp = ca_prompt(xp, g_ca, wq, wo, mem_k_p, mem_v_p, l)
        xp = run_ffn(xp.reshape(bp * seq, d), 1024 if l % 2 else 512).reshape(bp, seq, d)

        proj = norm_matmul(xs, lw["norm_mix"], lw["w_in"], tn=D_PROJ // 3)
        y, st_s = mixer_sample(proj, lw, consts, sample_states, l, st_s)
        xs = norm_matmul(y, ones_row, lw["w_out"], residual=xs, norm=False)
        q = norm_matmul(xs, g_ca, wq)
        o = ca_sample(q.reshape(bs, CA_HEADS, CA_DH), cache_mem_k, cache_mem_v, l).reshape(bs, d)
        xs = norm_matmul(o, ones_row[:, :d], wo, residual=xs, norm=False)
        xs = run_ffn(xs, bs)

        for lst, a in zip(new_p, st_p):
            lst.append(a)

    c_s, n_s, m_s, conv_s, ssd_s, ret_s = st_s
    return (xp, xs.reshape(bs, 1, d),
            jnp.stack(new_p[0]), jnp.stack(new_p[1]), jnp.stack(new_p[2]), jnp.stack(new_p[3]),
            jnp.stack(new_p[4]), jnp.stack(new_p[5]), mem_k_p, mem_v_p,
            c_s, n_s, m_s[:, :, :ML_HEADS], conv_s,
            ssd_s.reshape(DEPTH, bs, SSD_HEADS, SSD_HEADDIM, SSD_STATE), ret_s)
```

```python
import functools
import itertools
import math

import numpy as np
import jax
import jax.numpy as jnp
from jax import lax
from jax.experimental import pallas as pl
from jax.experimental.pallas import tpu as pltpu

F32 = jnp.float32
BF16 = jnp.bfloat16
HIGHEST = lax.Precision.HIGHEST

D_MODEL = 1024
DEPTH = 2
PAST_LEN = 16384
D_MIX = 2 * D_MODEL
ML_WIDTH = 3 * D_MIX // 8
ML_HEADS = 6
ML_DH = ML_WIDTH // ML_HEADS
SSD_WIDTH = 3 * D_MIX // 8
SSD_HEADDIM = 64
SSD_HEADS = SSD_WIDTH // SSD_HEADDIM
SSD_STATE = 128
SSD_GROUPS = 2
SSD_GROUP_WIDTH = SSD_WIDTH // SSD_GROUPS
SSD_CONV_DIM = SSD_WIDTH + 2 * SSD_GROUPS * SSD_STATE
CONV_W = 4
RET_WIDTH = D_MIX // 4
RET_HEADS = 4
RET_DH = RET_WIDTH // RET_HEADS
ROPE_BASE = 10000.0
CHUNK = 128
MEM_LEN = 256
CA_HEADS = 4
CA_DH = D_MODEL // CA_HEADS
D_FF = 11 * D_MODEL // 4
N_EXPERTS = 8
TOP_K = 2
D_EXPERT = D_FF // 2
EPS = 1e-6

LANES = 128
BF16_ROWS = 16
MXU_ROWS = 128

OFF_QML = 0
OFF_KML = OFF_QML + ML_WIDTH
OFF_VML = OFF_KML + ML_WIDTH
OFF_OML = OFF_VML + ML_WIDTH
OFF_Z = OFF_OML + ML_WIDTH
OFF_XBC = OFF_Z + SSD_WIDTH
OFF_QR = OFF_XBC + SSD_CONV_DIM
OFF_KR = OFF_QR + RET_WIDTH
OFF_VR = OFF_KR + RET_WIDTH
OFF_GR = OFF_VR + RET_WIDTH
OFF_SMALL = OFF_GR + RET_WIDTH
D_PROJ = OFF_SMALL + LANES
LANE_I = 0
LANE_F = ML_HEADS
LANE_DT = 2 * ML_HEADS

VMEM_LIMIT = 60000 * 1024


def _cparams(sem):
    return pltpu.CompilerParams(dimension_semantics=sem, vmem_limit_bytes=VMEM_LIMIT)


def _const_spec(shape):
    nd = len(shape)
    return pl.BlockSpec(shape, lambda *_: (0,) * nd, pipeline_mode=pl.Buffered(1))


def _rms(x, g):
    return x * lax.rsqrt(jnp.mean(x * x, axis=-1, keepdims=True) + EPS) * g


def _sigmoid(x):
    return 1.0 / (1.0 + jnp.exp(-x))


def _silu(x):
    return x * _sigmoid(x)


def _dot(a, b):
    return jnp.dot(a, b, preferred_element_type=F32)


def _dot_nt(a, b):
    return lax.dot_general(a, b, (((1,), (1,)), ((), ())), preferred_element_type=F32)


def _dot_exact(a, b):
    return jnp.dot(a, b, preferred_element_type=F32, precision=HIGHEST)


def _split3(v):
    v1 = v.astype(BF16)
    r1 = v - v1.astype(F32)
    v2 = r1.astype(BF16)
    v3 = (r1 - v2.astype(F32)).astype(BF16)
    return v1, v2, v3


def _norm_matmul_kernel(*refs, norm, residual):
    x_ref, g_ref, w_ref = refs[:3]
    o_ref = refs[-1]
    x = x_ref[...].astype(F32)
    if norm:
        x = _rms(x, g_ref[...])
    acc = _dot(x.astype(BF16), w_ref[...])
    if residual:
        acc = acc + refs[3][...]
    o_ref[...] = acc


def norm_matmul(x, g, w, residual=None, *, norm=True, tm=None, tn=None):
    m, k = x.shape
    n = w.shape[1]
    tm = tm or min(m, 512)
    tn = tn or min(n, 1024)
    in_specs = [pl.BlockSpec((tm, k), lambda j, i: (i, 0)),
                pl.BlockSpec((1, k), lambda j, i: (0, 0)),
                pl.BlockSpec((k, tn), lambda j, i: (0, j))]
    args = [x, g, w]
    if residual is not None:
        in_specs.append(pl.BlockSpec((tm, tn), lambda j, i: (i, j)))
        args.append(residual)
    return pl.pallas_call(
        functools.partial(_norm_matmul_kernel, norm=norm, residual=residual is not None),
        grid=(pl.cdiv(n, tn), m // tm),
        in_specs=in_specs,
        out_specs=pl.BlockSpec((tm, tn), lambda j, i: (i, j)),
        out_shape=jax.ShapeDtypeStruct((m, n), F32),
        compiler_params=_cparams(("parallel", "parallel")),
        name="norm_matmul",
    )(*args)


def _kv_proj_kernel(x_ref, g_ref, wk_ref, wv_ref, k_ref, v_ref):
    xn = _rms(x_ref[...], g_ref[...]).astype(BF16)
    k_ref[...] = _dot(xn, wk_ref[...])
    v_ref[...] = _dot(xn, wv_ref[...])


def kv_proj(mem2d, g, wk, wv, *, tm=512):
    m, d = mem2d.shape
    depth = wk.shape[0]
    out = jax.ShapeDtypeStruct((depth, m, d), F32)
    return pl.pallas_call(
        _kv_proj_kernel,
        grid=(depth, m // tm),
        in_specs=[pl.BlockSpec((tm, d), lambda l, i: (i, 0)),
                  pl.BlockSpec((None, 1, d), lambda l, i: (l, 0, 0)),
                  pl.BlockSpec((None, d, d), lambda l, i: (l, 0, 0)),
                  pl.BlockSpec((None, d, d), lambda l, i: (l, 0, 0))],
        out_specs=[pl.BlockSpec((None, tm, d), lambda l, i: (l, i, 0)),
                   pl.BlockSpec((None, tm, d), lambda l, i: (l, i, 0))],
        out_shape=[out, out],
        compiler_params=_cparams(("parallel", "parallel")),
        name="kv_proj",
    )(mem2d, g, wk, wv)


def _ca_prompt_kernel(x_ref, g_ref, wq_ref, wo_ref, k_ref, v_ref, o_ref):
    x = x_ref[...]
    xn = _rms(x, g_ref[...]).astype(BF16)
    q = _dot(xn, wq_ref[...])
    kb = k_ref[...].astype(BF16)
    vb = v_ref[...].astype(BF16)
    outs = []
    for h in range(CA_HEADS):
        sl = slice(h * CA_DH, (h + 1) * CA_DH)
        s = _dot_nt(q[:, sl].astype(BF16), kb[:, sl]) * (CA_DH ** -0.5)
        p = jnp.exp(s - jnp.max(s, axis=-1, keepdims=True))
        p = p / jnp.sum(p, axis=-1, keepdims=True)
        outs.append(_dot(p.astype(BF16), vb[:, sl]).astype(BF16))
    o = jnp.concatenate(outs, axis=-1)
    o_ref[...] = x + _dot(o, wo_ref[...])


def ca_prompt(x, g, wq, wo, mem_k, mem_v, layer, *, tm=512):
    b, length, d = x.shape
    mlen = mem_k.shape[2]
    return pl.pallas_call(
        _ca_prompt_kernel,
        grid=(b, length // tm),
        in_specs=[pl.BlockSpec((None, tm, d), lambda i, j: (i, j, 0)),
                  _const_spec((1, d)), _const_spec((d, d)), _const_spec((d, d)),
                  pl.BlockSpec((None, None, mlen, d), lambda i, j: (layer, i, 0, 0)),
                  pl.BlockSpec((None, None, mlen, d), lambda i, j: (layer, i, 0, 0))],
        out_specs=pl.BlockSpec((None, tm, d), lambda i, j: (i, j, 0)),
        out_shape=jax.ShapeDtypeStruct(x.shape, F32),
        compiler_params=_cparams(("parallel", "parallel")),
        name="ca_prompt",
    )(x, g, wq, wo, mem_k, mem_v)


def _ca_sample_kernel(q_ref, k_ref, v_ref, o_ref, *, bt):
    for j in range(bt):
        kq = k_ref[j] * q_ref[j]
        s = jnp.sum(kq, axis=-1, keepdims=True) * (CA_DH ** -0.5)
        p = jnp.exp(s - jnp.max(s, axis=0, keepdims=True))
        p = p / jnp.sum(p, axis=0, keepdims=True)
        o_ref[j] = jnp.sum(p * v_ref[j], axis=0)


def ca_sample(q, mem_k, mem_v, layer, *, bt=8):
    b = q.shape[0]
    blk = (None, bt) + mem_k.shape[2:]
    return pl.pallas_call(
        functools.partial(_ca_sample_kernel, bt=bt),
        grid=(b // bt,),
        in_specs=[pl.BlockSpec((bt,) + q.shape[1:], lambda i: (i, 0, 0)),
                  pl.BlockSpec(blk, lambda i: (layer, i, 0, 0, 0)),
                  pl.BlockSpec(blk, lambda i: (layer, i, 0, 0, 0))],
        out_specs=pl.BlockSpec((bt,) + q.shape[1:], lambda i: (i, 0, 0)),
        out_shape=jax.ShapeDtypeStruct(q.shape, F32),
        compiler_params=_cparams(("parallel",)),
        name="ca_sample",
    )(q, mem_k, mem_v)


def _ffn_kernel(x_ref, g_ref, wg_ref, wu_ref, wd_ref, o_ref, a_ref, *, fchunk):
    x = x_ref[...]
    xn = _rms(x, g_ref[...]).astype(BF16)
    dff = wg_ref.shape[1]
    for f0 in range(0, dff, fchunk):
        gate = _dot(xn, wg_ref[:, f0:f0 + fchunk])
        up = _dot(xn, wu_ref[:, f0:f0 + fchunk])
        a_ref[:, f0:f0 + fchunk] = (_silu(gate) * up).astype(BF16)
    o_ref[...] = x + _dot(a_ref[...], wd_ref[...])


def ffn_dense(x, g, wg, wu, wd, *, tm=512):
    m, d = x.shape
    dff = wg.shape[1]
    return pl.pallas_call(
        functools.partial(_ffn_kernel, fchunk=dff // 2),
        grid=(m // tm,),
        in_specs=[pl.BlockSpec((tm, d), lambda i: (i, 0)),
                  _const_spec((1, d)), _const_spec((d, dff)), _const_spec((d, dff)), _const_spec((dff, d))],
        out_specs=pl.BlockSpec((tm, d), lambda i: (i, 0)),
        out_shape=jax.ShapeDtypeStruct((m, d), F32),
        scratch_shapes=[pltpu.VMEM((tm, dff), BF16)],
        compiler_params=_cparams(("parallel",)),
        name="ffn_dense",
    )(x, g, wg, wu, wd)


def _moe_kernel(x_ref, g_ref, wrt_ref, wg_ref, wu_ref, wd_ref, gf_ref, o_ref,
                xn_ref, gate_t_ref, pos_t_ref, *, tm, sub, final_norm):
    e = pl.program_id(1)
    ne = pl.num_programs(1)
    epad = gate_t_ref.shape[0]

    @pl.when(e == 0)
    def _route():
        x = x_ref[...]
        xn = _rms(x, g_ref[...])
        xn_ref[...] = xn.astype(BF16)
        x_hi = xn.astype(BF16)
        x_lo = (xn - x_hi.astype(F32)).astype(BF16)
        w = wrt_ref[...]
        w_hi = w.astype(BF16)
        w_lo = (w - w_hi.astype(F32)).astype(BF16)
        logits = _dot_nt(w_hi, x_hi) + (_dot_nt(w_hi, x_lo) + _dot_nt(w_lo, x_hi))
        row = lax.broadcasted_iota(jnp.int32, (epad, tm), 0)
        logits = jnp.where(row < N_EXPERTS, logits, -jnp.inf)
        m1 = jnp.max(logits, axis=0, keepdims=True)
        i1 = jnp.min(jnp.where(logits == m1, row, epad), axis=0, keepdims=True)
        rest = jnp.where(row == i1, -jnp.inf, logits)
        m2 = jnp.max(rest, axis=0, keepdims=True)
        i2 = jnp.min(jnp.where(rest == m2, row, epad), axis=0, keepdims=True)
        e2 = jnp.exp(m2 - m1)
        den = 1.0 + e2
        gate_t = jnp.where(row == i1, 1.0 / den, 0.0) + jnp.where(row == i2, e2 / den, 0.0)
        sel_t = jnp.where((row == i1) | (row == i2), 1.0, 0.0)
        r_i = lax.broadcasted_iota(jnp.int32, (tm, tm), 0)
        c_i = lax.broadcasted_iota(jnp.int32, (tm, tm), 1)
        upper = jnp.where(r_i < c_i, 1.0, 0.0).astype(BF16)
        pos_t = _dot(sel_t.astype(BF16), upper)
        pos_t = jnp.where(sel_t > 0, pos_t, -1.0)
        gate_t_ref[...] = gate_t
        pos_t_ref[...] = pos_t
        o_ref[...] = x

    pos_row = pos_t_ref[pl.ds(e, 1), :]
    gate_row = gate_t_ref[pl.ds(e, 1), :]
    count = jnp.sum(jnp.where(pos_row >= 0, 1.0, 0.0)).astype(jnp.int32)
    half = sub // 2
    n_full = (count + (half - 1)) // sub
    done = n_full * sub

    def expert_rows(base, rows):
        slot = lax.broadcasted_iota(jnp.int32, (rows, tm), 0).astype(F32) + base.astype(F32)
        hit = pos_row == slot
        gather = jnp.where(hit, 1.0, 0.0).astype(BF16)
        gate = jnp.sum(jnp.where(hit, gate_row, 0.0), axis=-1, keepdims=True)
        xg = _dot(gather, xn_ref[...]).astype(BF16)
        act = (_silu(_dot(xg, wg_ref[...])) * _dot(xg, wu_ref[...])).astype(BF16)
        ye = (gate * _dot(act, wd_ref[...])).astype(BF16)
        o_ref[...] += lax.dot_general(gather, ye, (((0,), (0,)), ((), ())), preferred_element_type=F32)

    def full_body(j, carry):
        expert_rows(j * sub, sub)
        return carry

    lax.fori_loop(0, n_full, full_body, 0)

    @pl.when(count > done)
    def _remainder():
        expert_rows(done, half)

    if final_norm:
        @pl.when(e == ne - 1)
        def _final():
            o_ref[...] = _rms(o_ref[...], gf_ref[...])


def _moe_sub_rows(tm):
    mean = tm * TOP_K // N_EXPERTS
    return min(tm, max(MXU_ROWS, -(-mean // MXU_ROWS) * MXU_ROWS))


def moe_ffn(x, g, w_router_t, wg, wu, wd, g_final, *, tm, sub, final_norm):
    m, d = x.shape
    ne, _, de = wg.shape
    epad = w_router_t.shape[0]
    return pl.pallas_call(
        functools.partial(_moe_kernel, tm=tm, sub=sub, final_norm=final_norm),
        grid=(m // tm, ne),
        in_specs=[pl.BlockSpec((tm, d), lambda i, e: (i, 0)),
                  pl.BlockSpec((1, d), lambda i, e: (0, 0)),
                  pl.BlockSpec((epad, d), lambda i, e: (0, 0)),
                  pl.BlockSpec((None, d, de), lambda i, e: (e, 0, 0)),
                  pl.BlockSpec((None, d, de), lambda i, e: (e, 0, 0)),
                  pl.BlockSpec((None, de, d), lambda i, e: (e, 0, 0)),
                  pl.BlockSpec((1, d), lambda i, e: (0, 0))],
        out_specs=pl.BlockSpec((tm, d), lambda i, e: (i, 0)),
        out_shape=jax.ShapeDtypeStruct((m, d), F32),
        scratch_shapes=[pltpu.VMEM((tm, d), BF16),
                        pltpu.VMEM((epad, tm), F32), pltpu.VMEM((epad, tm), F32)],
        compiler_params=_cparams(("parallel", "arbitrary")),
        name="moe_ffn",
    )(x, g, w_router_t, wg, wu, wd, g_final)


def _retention_constants(c):
    log_g = np.log1p(-np.exp2(-5.0 - np.arange(RET_HEADS, dtype=np.float32))).astype(np.float32)
    j = np.arange(c, dtype=np.float32)
    rel = j[:, None] - j[None, :]
    w_intra = np.where(rel >= 0, np.exp(rel[None] * log_g[:, None, None]), 0.0).astype(np.float32)
    w_inter = np.exp((j[None, :] + 1.0) * log_g[:, None]).astype(np.float32)
    w_end = np.exp((c - 1.0 - j[None, :]) * log_g[:, None]).astype(np.float32)
    decay = np.exp(c * log_g).astype(np.float32)
    cols = np.zeros((c, LANES), np.float32)
    cols[:, :RET_HEADS] = w_inter.T
    cols[:, RET_HEADS:2 * RET_HEADS] = w_end.T
    return w_intra, cols, [float(d) for d in decay]


def _rotary_tables(pos):
    half = RET_DH // 2
    inv = ROPE_BASE ** (-jnp.arange(half, dtype=F32) / half)
    ang = pos[:, None] * inv[None, :]
    cos, sin = jnp.cos(ang), jnp.sin(ang)
    return jnp.concatenate([cos, cos], axis=-1), jnp.concatenate([-sin, sin], axis=-1)


def _head_expand_matrix():
    e = np.zeros((LANES, SSD_WIDTH), np.float32)
    for h in range(SSD_HEADS):
        e[LANE_DT + h, h * SSD_HEADDIM:(h + 1) * SSD_HEADDIM] = 1.0
    return e


def _softplus_parts(v):
    sp = jnp.log1p(jnp.exp(-jnp.abs(v)))
    return jnp.minimum(v, 0.0) - sp, jnp.maximum(v, 0.0) + sp


def _mixer_prompt_kernel(x_ref, xnext_ref, g_ref, win_ref, wout_ref, bias_ref, alog_ref, convw_ref, convb_ref,
                         mlg_ref, ssdd_ref, ssdg_ref, retg_ref, cos_ref, sin_ref, retw_ref, retcol_ref,
                         expand_ref,
                         xo_ref, c_ref, n_ref, m_ref, conv_ref, ssd_ref, ret_ref,
                         proj_s, cs_s, cst_s, smt_s, dtx_s, acx_s, y_s, u_s, *, tm, tiles_per_seq, ret_decay):
    c = CHUNK
    i = pl.program_id(0)

    @pl.when(i % tiles_per_seq == 0)
    def _init():
        c_ref[...] = jnp.zeros_like(c_ref)
        n_ref[...] = jnp.zeros_like(n_ref)
        m_ref[...] = jnp.zeros_like(m_ref)
        ssd_ref[...] = jnp.zeros_like(ssd_ref)
        ret_ref[...] = jnp.zeros_like(ret_ref)
        u_s[0:8, :] = jnp.zeros((8, SSD_CONV_DIM), F32)

    seg_w = 512
    r_i = lax.broadcasted_iota(jnp.int32, (c, c), 0)
    c_i = lax.broadcasted_iota(jnp.int32, (c, c), 1)
    causal = r_i >= c_i
    tril = jnp.where(causal, 1.0, 0.0).astype(BF16)
    lane1 = lax.broadcasted_iota(jnp.int32, (1, LANES), 1)
    lane_c = lax.broadcasted_iota(jnp.int32, (c, LANES), 1)
    a_vec = -jnp.exp(alog_ref[...])
    expand3 = expand_ref[...]
    bias = bias_ref[...]

    def cumsum_rows(v):
        p = _dot(tril, jnp.concatenate(_split3(v), axis=-1))
        return p[:, :LANES] + p[:, LANES:2 * LANES] + p[:, 2 * LANES:]

    def expand_heads(v):
        return _dot(jnp.concatenate(_split3(v), axis=-1), expand3)

    seg_offs = list(range(0, D_PROJ, seg_w))
    assert seg_offs[-1] == OFF_SMALL

    def prepare(src_ref, slot):
        xn = _rms(src_ref[...], g_ref[...]).astype(BF16)
        for off in seg_offs[-1:] + seg_offs[:-1]:
            wdt = min(seg_w, D_PROJ - off)
            proj_s[slot, :, off:off + wdt] = _dot(xn, win_ref[:, off:off + wdt])
            yield
            if off != OFF_SMALL:
                continue
            is_f = (lane_c >= LANE_F) & (lane_c < LANE_DT)
            is_dt = (lane_c >= LANE_DT) & (lane_c < LANE_DT + SSD_HEADS)
            for ci in range(tm // c):
                rows = slice(ci * c, (ci + 1) * c)
                small = proj_s[slot, rows, OFF_SMALL:OFF_SMALL + LANES] + bias
                logf, dt = _softplus_parts(small)
                cs = cumsum_rows(jnp.where(is_f, logf, jnp.where(is_dt, dt * a_vec, 0.0)))
                cs_s[slot, rows, :] = cs
                cst_s[slot, :, rows] = cs.T
                smt_s[slot, :, rows] = small.T
                dtx_s[slot, rows, :] = expand_heads(jnp.where(is_dt, dt, 0.0))
                acx_s[slot, rows, :] = expand_heads(jnp.where(is_dt, cs, 0.0))
                yield

    @pl.when(i == 0)
    def _first():
        for _ in prepare(x_ref, 0):
            pass

    cur = i % 2
    proj_cur = proj_s.at[cur]
    x = x_ref[...]

    def chunk_body(ci):
        rows = slice(ci * c, (ci + 1) * c)
        small = proj_cur[rows, OFF_SMALL:OFF_SMALL + LANES] + bias
        cs = cs_s[cur, rows, :]
        cs_t = cst_s[cur, :, rows]
        small_t = smt_s[cur, :, rows]

        m_vec = m_ref[...]
        m_out = m_vec
        for h in range(ML_HEADS):
            hs = slice(h * ML_DH, (h + 1) * ML_DH)
            b_col = cs[:, LANE_F + h:LANE_F + h + 1]
            b_row = cs_t[LANE_F + h:LANE_F + h + 1, :]
            i_col = small[:, LANE_I + h:LANE_I + h + 1]
            i_row = small_t[LANE_I + h:LANE_I + h + 1, :]
            m_prev = m_vec[:, h:h + 1]
            lw = jnp.where(causal, b_col + (i_row - b_row), -jnp.inf)
            log_a = b_col + m_prev
            m_t = jnp.maximum(log_a, jnp.max(lw, axis=-1, keepdims=True))
            w_intra = jnp.exp(lw - m_t)
            w_inter = jnp.exp(log_a - m_t)
            q = proj_cur[rows, OFF_QML + h * ML_DH:OFF_QML + (h + 1) * ML_DH]
            k = proj_cur[rows, OFF_KML + h * ML_DH:OFF_KML + (h + 1) * ML_DH] * (ML_DH ** -0.5)
            v = proj_cur[rows, OFF_VML + h * ML_DH:OFF_VML + (h + 1) * ML_DH]
            og = proj_cur[rows, OFF_OML + h * ML_DH:OFF_OML + (h + 1) * ML_DH]
            qb, kb, vb = q.astype(BF16), k.astype(BF16), v.astype(BF16)
            s = w_intra * _dot_nt(qb, kb)
            cm = c_ref[h]
            n_row = n_ref[h:h + 1, :]
            num = _dot(s.astype(BF16), vb) + w_inter * _dot_nt(qb, cm.astype(BF16))
            den = jnp.sum(s, axis=-1, keepdims=True) + w_inter * jnp.sum(q * n_row, axis=-1, keepdims=True)
            den = jnp.maximum(jnp.abs(den), jnp.exp(-m_t))
            hh = num / den
            m_new = m_t[c - 1:c, :]
            decay = jnp.exp(log_a[c - 1:c, :] - m_new)
            w_end = jnp.exp(b_col[c - 1:c, :] - b_col + i_col - m_new)
            c_ref[h] = decay * cm + _dot((v * w_end).T.astype(BF16), kb)
            n_ref[h:h + 1, :] = decay * n_row + jnp.sum(k * w_end, axis=0, keepdims=True)
            m_out = jnp.where(lane1 == h, m_new, m_out)
            hn = hh * lax.rsqrt(jnp.mean(hh * hh, axis=-1, keepdims=True) + EPS)
            y_s[rows, hs] = (hn * mlg_ref[:, hs] * _sigmoid(og)).astype(BF16)
            yield
        m_ref[...] = m_out

        u = proj_cur[rows, OFF_XBC:OFF_XBC + SSD_CONV_DIM]
        u_s[8:8 + c, :] = u
        conv = convb_ref[...] + convw_ref[3:4, :] * u
        for jj in range(CONV_W - 1):
            conv = conv + convw_ref[jj:jj + 1, :] * u_s[5 + jj:5 + jj + c, :]
        u_s[0:8, :] = u_s[c:c + 8, :]
        xc = _silu(conv)
        xs = xc[:, :SSD_WIDTH]
        dt_exp = dtx_s[cur, rows, :]
        acum_exp = acx_s[cur, rows, :]
        xdt = xs * dt_exp
        w_end_exp = jnp.exp(acum_exp[c - 1:c, :] - acum_exp)
        xw = xdt * w_end_exp
        ea_exp = jnp.exp(acum_exp)
        lane_lo = lane_c < SSD_HEADDIM
        hpg = SSD_HEADS // SSD_GROUPS
        y_cols = []
        yield
        for g in range(SSD_GROUPS):
            gw = slice(g * SSD_GROUP_WIDTH, (g + 1) * SSD_GROUP_WIDTH)
            bg = xc[:, SSD_WIDTH + g * SSD_STATE:SSD_WIDTH + (g + 1) * SSD_STATE].astype(BF16)
            cg = xc[:, SSD_WIDTH + (SSD_GROUPS + g) * SSD_STATE:
                    SSD_WIDTH + (SSD_GROUPS + g + 1) * SSD_STATE].astype(BF16)
            cb = _dot_nt(cg, bg)
            st = ssd_ref[gw, :]
            inter = _dot_nt(cg, st.astype(BF16)) * ea_exp[:, gw]
            for pr in range(hpg // 2):
                h0 = g * hpg + 2 * pr
                xp = xdt[:, h0 * SSD_HEADDIM:(h0 + 2) * SSD_HEADDIM].astype(BF16)
                pair = []
                for hidx in (h0, h0 + 1):
                    a_col = cs[:, LANE_DT + hidx:LANE_DT + hidx + 1]
                    a_row = cs_t[LANE_DT + hidx:LANE_DT + hidx + 1, :]
                    dec = jnp.exp(jnp.where(causal, a_col - a_row, -jnp.inf))
                    pair.append(_dot((dec * cb).astype(BF16), xp))
                y_cols.append(jnp.where(lane_lo, pair[0], pair[1]) + inter[:, pr * LANES:(pr + 1) * LANES])
                yield
            upd = _dot(xw[:, gw].T.astype(BF16), bg)
            for r in range(hpg):
                hidx = g * hpg + r
                scal = jnp.exp(cs[c - 1:c, LANE_DT + hidx:LANE_DT + hidx + 1])
                hr = slice(hidx * SSD_HEADDIM, (hidx + 1) * SSD_HEADDIM)
                ssd_ref[hr, :] = scal * ssd_ref[hr, :] + upd[r * SSD_HEADDIM:(r + 1) * SSD_HEADDIM, :]
            yield
        ys = jnp.concatenate(y_cols, axis=-1) + ssdd_ref[...] * xs
        yz = ys * _silu(proj_cur[rows, OFF_Z:OFF_Z + SSD_WIDTH])
        for g in range(SSD_GROUPS):
            gw = slice(g * SSD_GROUP_WIDTH, (g + 1) * SSD_GROUP_WIDTH)
            seg = yz[:, gw]
            seg = seg * lax.rsqrt(jnp.mean(seg * seg, axis=-1, keepdims=True) + EPS) * ssdg_ref[:, gw]
            y_s[rows, ML_WIDTH + g * SSD_GROUP_WIDTH:ML_WIDTH + (g + 1) * SSD_GROUP_WIDTH] = seg.astype(BF16)
        yield

        cosb = cos_ref[rows, :]
        sinb = sin_ref[rows, :]
        retcol = retcol_ref[...]
        for h in range(RET_HEADS):
            hs = slice(h * RET_DH, (h + 1) * RET_DH)
            q = proj_cur[rows, OFF_QR + h * RET_DH:OFF_QR + (h + 1) * RET_DH]
            k = proj_cur[rows, OFF_KR + h * RET_DH:OFF_KR + (h + 1) * RET_DH]
            v = proj_cur[rows, OFF_VR + h * RET_DH:OFF_VR + (h + 1) * RET_DH]
            gr = proj_cur[rows, OFF_GR + h * RET_DH:OFF_GR + (h + 1) * RET_DH]
            qr = q * cosb + pltpu.roll(q, RET_DH // 2, 1) * sinb
            kr = (k * cosb + pltpu.roll(k, RET_DH // 2, 1) * sinb) * (RET_DH ** -0.5)
            qb, kb, vb = qr.astype(BF16), kr.astype(BF16), v.astype(BF16)
            att = _dot_nt(qb, kb) * retw_ref[h]
            sm = ret_ref[h]
            o = _dot(att.astype(BF16), vb) + _dot(qb, sm.astype(BF16)) * retcol[:, h:h + 1]
            kw = kr * retcol[:, RET_HEADS + h:RET_HEADS + h + 1]
            ret_ref[h] = ret_decay[h] * sm + _dot(kw.T.astype(BF16), vb)
            on = o * lax.rsqrt(jnp.mean(o * o, axis=-1, keepdims=True) + EPS)
            y_s[rows, ML_WIDTH + SSD_WIDTH + h * RET_DH:ML_WIDTH + SSD_WIDTH + (h + 1) * RET_DH] = (
                on * retg_ref[:, hs] * _silu(gr)).astype(BF16)
            yield

    stages_per_chunk = ML_HEADS + 1 + SSD_GROUPS * (SSD_HEADS // SSD_GROUPS // 2 + 1) + 1 + RET_HEADS
    n_segments = len(seg_offs) + tm // c
    every = max(1, (tm // c) * stages_per_chunk // n_segments)
    segments = prepare(xnext_ref, (i + 1) % 2)
    for k, _ in enumerate(itertools.chain.from_iterable(chunk_body(ci) for ci in range(tm // c))):
        if k % every == 0:
            next(segments, None)
    for _ in segments:
        pass
    xo_ref[...] = x + _dot(y_s[...], wout_ref[...])
    conv_ref[...] = u_s[5:8, :]


def mixer_prompt(x, lw, consts, *, tm):
    b, length, d = x.shape
    c = CHUNK
    tps = length // tm
    n_tiles = b * tps
    x2d = x.reshape(b * length, d)
    row = lambda w: _const_spec((1, w))
    state_spec = lambda *shape: pl.BlockSpec((None,) + shape, lambda i: (i // tps,) + (0,) * len(shape))
    out_shapes = [jax.ShapeDtypeStruct(x2d.shape, F32),
                  jax.ShapeDtypeStruct((b, ML_HEADS, ML_DH, ML_DH), F32),
                  jax.ShapeDtypeStruct((b, ML_HEADS, ML_DH), F32),
                  jax.ShapeDtypeStruct((b, 1, LANES), F32),
                  jax.ShapeDtypeStruct((b, CONV_W - 1, SSD_CONV_DIM), F32),
                  jax.ShapeDtypeStruct((b, SSD_WIDTH, SSD_STATE), F32),
                  jax.ShapeDtypeStruct((b, RET_HEADS, RET_DH, RET_DH), F32)]
    outs = pl.pallas_call(
        functools.partial(_mixer_prompt_kernel, tm=tm, tiles_per_seq=tps, ret_decay=consts["ret_decay"]),
        grid=(n_tiles,),
        in_specs=[pl.BlockSpec((tm, d), lambda i: (i, 0)),
                  pl.BlockSpec((tm, d), lambda i: (jnp.minimum(i + 1, n_tiles - 1), 0)),
                  row(d), _const_spec((d, D_PROJ)), _const_spec((D_MIX, d)),
                  row(LANES), row(LANES), _const_spec((CONV_W, SSD_CONV_DIM)), row(SSD_CONV_DIM),
                  row(ML_WIDTH), row(SSD_WIDTH), row(SSD_WIDTH), row(RET_WIDTH),
                  pl.BlockSpec((tm, LANES), lambda i: (i % tps, 0)),
                  pl.BlockSpec((tm, LANES), lambda i: (i % tps, 0)),
                  _const_spec((RET_HEADS, c, c)), _const_spec((c, LANES)),
                  _const_spec((3 * LANES, SSD_WIDTH))],
        out_specs=[pl.BlockSpec((tm, d), lambda i: (i, 0)),
                   state_spec(ML_HEADS, ML_DH, ML_DH), state_spec(ML_HEADS, ML_DH), state_spec(1, LANES),
                   state_spec(CONV_W - 1, SSD_CONV_DIM), state_spec(SSD_WIDTH, SSD_STATE),
                   state_spec(RET_HEADS, RET_DH, RET_DH)],
        out_shape=out_shapes,
        scratch_shapes=[pltpu.VMEM((2, tm, D_PROJ), F32),
                        pltpu.VMEM((2, tm, LANES), F32), pltpu.VMEM((2, LANES, tm), F32),
                        pltpu.VMEM((2, LANES, tm), F32), pltpu.VMEM((2, tm, SSD_WIDTH), F32),
                        pltpu.VMEM((2, tm, SSD_WIDTH), F32), pltpu.VMEM((tm, D_MIX), BF16),
                        pltpu.VMEM((c + 8, SSD_CONV_DIM), F32)],
        compiler_params=_cparams(("arbitrary",)),
        name="mixer_prompt",
    )(x2d, x2d, lw["norm_mix"], lw["w_in"], lw["w_out"], lw["bias"], lw["alog"], lw["conv_w"], lw["conv_b"],
      lw["ml_norm"], lw["ssd_d"], lw["ssd_norm"], lw["ret_norm"], consts["cos_p"], consts["sin_p"],
      consts["ret_w"], consts["ret_cols"], consts["expand3"])
    return [outs[0].reshape(b, length, d)] + list(outs[1:])


N_SAMPLE_STATES = 6


def _mixer_sample_kernel(*refs, bt, ret_gamma, n_alias):
    (proj_ref, small_ref, bias_ref, alog_ref, convw_ref, convb_ref, mlg_ref, ssdd_ref,
     ssdg_ref, retg_ref, cos_ref, sin_ref, expand_ref,
     c_in, n_in, m_in, conv_in, ssd_in, ret_in) = refs[:19]
    (y_ref, c_out, n_out, m_out, conv_out, ssd_out, ret_out,
     wi_s, wt_s, en_s, dtx_s, eax_s) = refs[19 + n_alias:]
    small = small_ref[...] + bias_ref[...]
    logf, dt = _softplus_parts(small)
    lane_b = lax.broadcasted_iota(jnp.int32, (bt, LANES), 1)
    is_dt = (lane_b >= LANE_DT) & (lane_b < LANE_DT + SSD_HEADS)
    log_a = pltpu.roll(logf, LANES - LANE_F, 1) + m_in[...]
    m_t = jnp.maximum(log_a, small)
    m_out[...] = m_t
    wi_s[...] = jnp.exp(small - m_t)
    wt_s[...] = jnp.exp(log_a - m_t)
    en_s[...] = jnp.exp(-m_t)
    a_vec = -jnp.exp(alog_ref[...])
    expand = expand_ref[...]
    dtx_s[...] = _dot_exact(jnp.where(is_dt, dt, 0.0), expand)
    eax_s[...] = jnp.exp(_dot_exact(jnp.where(is_dt, dt * a_vec, 0.0), expand))

    sub16 = lax.broadcasted_iota(jnp.int32, (BF16_ROWS, LANES), 0)
    row_hi = lax.broadcasted_iota(jnp.int32, (LANES, LANES), 0) < SSD_HEADDIM

    def rows3(a, b, c):
        return jnp.where(sub16 == 0, a, jnp.where(sub16 == 1, b, jnp.where(sub16 == 2, c, 0.0))).astype(BF16)

    def split(v):
        hi = v.astype(BF16).astype(F32)
        return hi, v - hi

    def outer(a, b):
        a_hi, a_lo = split(a)
        b_hi, b_lo = split(b)
        return lax.dot_general(rows3(a_hi, a_hi, a_lo), rows3(b_hi, b_lo, b_hi), (((0,), (0,)), ((), ())),
                               preferred_element_type=F32)

    def row1(a):
        return jnp.where(sub16 == 0, a, 0.0).astype(BF16)

    cosb = cos_ref[...]
    sinb = sin_ref[...]
    hpg = SSD_HEADS // SSD_GROUPS

    sub_l = lax.broadcasted_iota(jnp.int32, (bt, LANES), 0)
    sub_w = lax.broadcasted_iota(jnp.int32, (bt, SSD_WIDTH), 0)

    def body(j, carry):
        pick_l = lambda ref: jnp.sum(jnp.where(sub_l == j, ref[...], 0.0), axis=0, keepdims=True)
        pick_w = lambda ref: jnp.sum(jnp.where(sub_w == j, ref[...], 0.0), axis=0, keepdims=True)
        wi = pick_l(wi_s)
        wt = pick_l(wt_s)
        en = pick_l(en_s)
        for h in range(ML_HEADS):
            hs = slice(h * ML_DH, (h + 1) * ML_DH)
            q = proj_ref[j, :,OFF_QML + h * ML_DH:OFF_QML + (h + 1) * ML_DH]
            k = proj_ref[j, :,OFF_KML + h * ML_DH:OFF_KML + (h + 1) * ML_DH] * (ML_DH ** -0.5)
            v = proj_ref[j, :,OFF_VML + h * ML_DH:OFF_VML + (h + 1) * ML_DH]
            og = proj_ref[j, :,OFF_OML + h * ML_DH:OFF_OML + (h + 1) * ML_DH]
            w_in_h = wi[:, h:h + 1]
            w_tr_h = wt[:, h:h + 1]
            c_new = w_tr_h * c_in[j, h] + w_in_h * outer(v, k)
            n_new = w_tr_h * n_in[j, h:h + 1, :] + w_in_h * k
            c_out[j, h] = c_new
            n_out[j, h:h + 1, :] = n_new
            num = _dot_nt(row1(q), c_new.astype(BF16))[0:1, :]
            den = jnp.sum(n_new * q, axis=-1, keepdims=True)
            den = jnp.maximum(jnp.abs(den), en[:, h:h + 1])
            hh = num / den
            hn = hh * lax.rsqrt(jnp.mean(hh * hh, axis=-1, keepdims=True) + EPS)
            y_ref[j, :,hs] = hn * mlg_ref[:, hs] * _sigmoid(og)
        u = proj_ref[j, :,OFF_XBC:OFF_XBC + SSD_CONV_DIM]
        prev = conv_in[j]
        conv = convb_ref[...] + convw_ref[CONV_W - 1:CONV_W, :] * u
        for jj in range(CONV_W - 1):
            conv = conv + convw_ref[jj:jj + 1, :] * prev[jj:jj + 1, :]
        conv_out[j, 0:CONV_W - 2, :] = prev[1:CONV_W - 1, :]
        conv_out[j, CONV_W - 2:CONV_W - 1, :] = u
        xc = _silu(conv)
        xs = xc[:, :SSD_WIDTH]
        xdt = xs * pick_w(dtx_s)
        ea = pick_w(eax_s)
        y_cols = []
        for g in range(SSD_GROUPS):
            bg = xc[:, SSD_WIDTH + g * SSD_STATE:SSD_WIDTH + (g + 1) * SSD_STATE]
            cg = xc[:, SSD_WIDTH + (SSD_GROUPS + g) * SSD_STATE:SSD_WIDTH + (SSD_GROUPS + g + 1) * SSD_STATE]
            for pr in range(hpg // 2):
                h0 = g * hpg + 2 * pr
                pw = slice(h0 * SSD_HEADDIM, (h0 + 2) * SSD_HEADDIM)
                lo = h0 * SSD_HEADDIM
                decay = jnp.where(row_hi, ea[:, lo:lo + 1], ea[:, lo + SSD_HEADDIM:lo + SSD_HEADDIM + 1])
                st_new = decay * ssd_in[j, pw, :] + outer(xdt[:, pw], bg)
                ssd_out[j, pw, :] = st_new
                y_cols.append(_dot_nt(row1(cg), st_new.astype(BF16))[0:1, :])
        ys = jnp.concatenate(y_cols, axis=-1) + ssdd_ref[...] * xs
        yz = ys * _silu(proj_ref[j, :,OFF_Z:OFF_Z + SSD_WIDTH])
        for g in range(SSD_GROUPS):
            gw = slice(g * SSD_GROUP_WIDTH, (g + 1) * SSD_GROUP_WIDTH)
            seg = yz[:, gw]
            y_ref[j, :,ML_WIDTH + g * SSD_GROUP_WIDTH:ML_WIDTH + (g + 1) * SSD_GROUP_WIDTH] = (
                seg * lax.rsqrt(jnp.mean(seg * seg, axis=-1, keepdims=True) + EPS) * ssdg_ref[:, gw])
        for h in range(RET_HEADS):
            hs = slice(h * RET_DH, (h + 1) * RET_DH)
            q = proj_ref[j, :,OFF_QR + h * RET_DH:OFF_QR + (h + 1) * RET_DH]
            k = proj_ref[j, :,OFF_KR + h * RET_DH:OFF_KR + (h + 1) * RET_DH]
            v = proj_ref[j, :,OFF_VR + h * RET_DH:OFF_VR + (h + 1) * RET_DH]
            gr = proj_ref[j, :,OFF_GR + h * RET_DH:OFF_GR + (h + 1) * RET_DH]
            qr = q * cosb + pltpu.roll(q, RET_DH // 2, 1) * sinb
            kr = (k * cosb + pltpu.roll(k, RET_DH // 2, 1) * sinb) * (RET_DH ** -0.5)
            s_new = ret_gamma[h] * ret_in[j, h] + outer(kr, v)
            ret_out[j, h] = s_new
            o = _dot(row1(qr), s_new.astype(BF16))[0:1, :]
            on = o * lax.rsqrt(jnp.mean(o * o, axis=-1, keepdims=True) + EPS)
            y_ref[j, :,ML_WIDTH + SSD_WIDTH + h * RET_DH:ML_WIDTH + SSD_WIDTH + (h + 1) * RET_DH] = (
                on * retg_ref[:, hs] * _silu(gr))
        return carry

    lax.fori_loop(0, bt, body, 0)


def mixer_sample(proj, lw, consts, states, layer, prev_out=None, *, bt=8):
    b = proj.shape[0]
    row = lambda w: pl.BlockSpec((1, w), lambda i: (0, 0))
    blk = lambda *shape: pl.BlockSpec((bt,) + shape, lambda i: (i,) + (0,) * len(shape))
    lblk = lambda *shape: pl.BlockSpec((None, bt) + shape, lambda i: (layer, i) + (0,) * len(shape))
    state_specs = [lblk(ML_HEADS, ML_DH, ML_DH), lblk(ML_HEADS, ML_DH), lblk(LANES),
                   lblk(CONV_W - 1, SSD_CONV_DIM), lblk(SSD_WIDTH, SSD_STATE), lblk(RET_HEADS, RET_DH, RET_DH)]
    state_shapes = [jax.ShapeDtypeStruct(s.shape, F32) for s in states]
    proj3 = proj.reshape(b, 1, D_PROJ)
    small = proj[:, OFF_SMALL:OFF_SMALL + LANES]
    n_fixed = 13 + N_SAMPLE_STATES
    alias_args = list(prev_out) if prev_out is not None else []
    aliases = {n_fixed + k: 1 + k for k in range(len(alias_args))}
    outs = pl.pallas_call(
        functools.partial(_mixer_sample_kernel, bt=bt, ret_gamma=consts["ret_gamma"], n_alias=len(alias_args)),
        grid=(b // bt,),
        in_specs=[blk(1, D_PROJ), blk(LANES), row(LANES), row(LANES),
                  pl.BlockSpec((CONV_W, SSD_CONV_DIM), lambda i: (0, 0)), row(SSD_CONV_DIM),
                  row(ML_WIDTH), row(SSD_WIDTH), row(SSD_WIDTH), row(RET_WIDTH), row(LANES), row(LANES),
                  pl.BlockSpec((LANES, SSD_WIDTH), lambda i: (0, 0))] + state_specs
                 + [pl.BlockSpec(memory_space=pl.ANY)] * len(alias_args),
        out_specs=[blk(1, D_MIX)] + state_specs,
        out_shape=[jax.ShapeDtypeStruct((b, 1, D_MIX), F32)] + state_shapes,
        input_output_aliases=aliases,
        scratch_shapes=[pltpu.VMEM((bt, LANES), F32)] * 3 + [pltpu.VMEM((bt, SSD_WIDTH), F32)] * 2,
        compiler_params=_cparams(("parallel",)),
        name="mixer_sample",
    )(proj3, small, lw["bias"], lw["alog"], lw["conv_w"], lw["conv_b"], lw["ml_norm"], lw["ssd_d"],
      lw["ssd_norm"], lw["ret_norm"], consts["cos_s"], consts["sin_s"], consts["expand"],
      *states, *alias_args)
    return outs[0].reshape(b, D_MIX), list(outs[1:])


def _pack_in_proj(w):
    sizes = [ML_WIDTH] * 4 + [ML_HEADS, ML_HEADS, SSD_WIDTH, SSD_CONV_DIM, SSD_HEADS] + [RET_WIDTH] * 4
    offs = np.concatenate([[0], np.cumsum(sizes)])
    seg = lambda i: w[:, int(offs[i]):int(offs[i + 1])]
    small = jnp.concatenate([seg(4), seg(5), seg(8)], axis=1)
    small = jnp.pad(small, ((0, 0), (0, LANES - small.shape[1])))
    cols = [seg(0), seg(1), seg(2), seg(3), seg(6), seg(7), seg(9), seg(10), seg(11), seg(12), small]
    return jnp.concatenate(cols, axis=1).astype(BF16)


def _lane_row(parts):
    v = jnp.concatenate([p.astype(F32) for p in parts])
    return jnp.pad(v, (0, LANES - v.shape[0]))[None, :]


def kernel(x_prompt, x_sample, state_mlstm_c, state_mlstm_n, state_mlstm_m, state_ssd_conv, state_ssd, state_ret,
           cache_mem_k, cache_mem_v, mem_prompt,
           norm_mix, w_in, ml_i_bias, ml_f_bias, ml_norm, ssd_conv_w, ssd_conv_b, ssd_dt_bias, ssd_a_log, ssd_d,
           ssd_norm, ret_norm, w_out, norm_ca, norm_mem, w_ca_q, w_ca_k, w_ca_v, w_ca_o, norm_ffn,
           ffn_w_gate, ffn_w_up, ffn_w_down, moe_w_router, moe_w_gate, moe_w_up, moe_w_down, norm_final):
    bp, seq, d = x_prompt.shape
    bs = x_sample.shape[0]
    assert x_sample.shape[1] == 1 and seq % CHUNK == 0 and DEPTH % 2 == 0

    ret_w, ret_cols, ret_decay = _retention_constants(CHUNK)
    _, _, ret_gamma = _retention_constants(1)
    cos_p, sin_p = _rotary_tables(jnp.arange(seq, dtype=F32))
    cos_s, sin_s = _rotary_tables(PAST_LEN + jnp.arange(1, dtype=F32))
    consts = {"ret_w": jnp.asarray(ret_w), "ret_cols": jnp.asarray(ret_cols), "ret_decay": ret_decay,
              "ret_gamma": ret_gamma, "cos_p": cos_p, "sin_p": sin_p, "cos_s": cos_s, "sin_s": sin_s,
              "expand": jnp.asarray(_head_expand_matrix()),
              "expand3": jnp.asarray(np.tile(_head_expand_matrix(), (3, 1))).astype(BF16)}
    zeros12 = jnp.zeros((2 * ML_HEADS,), F32)
    ones_row = jnp.ones((1, D_MIX), F32)

    mem2d = mem_prompt.reshape(bp * MEM_LEN, d)
    mem_k_p, mem_v_p = kv_proj(mem2d, norm_mem[:, None, :], w_ca_k.astype(BF16), w_ca_v.astype(BF16))

    mem_k_p = mem_k_p.reshape(DEPTH, bp, MEM_LEN, d)
    mem_v_p = mem_v_p.reshape(DEPTH, bp, MEM_LEN, d)
    sample_states = (state_mlstm_c, state_mlstm_n,
                     jnp.pad(state_mlstm_m, ((0, 0), (0, 0), (0, LANES - ML_HEADS))), state_ssd_conv,
                     state_ssd.reshape(DEPTH, bs, SSD_WIDTH, SSD_STATE), state_ret)
    st_s = None

    xp = x_prompt
    xs = x_sample.reshape(bs, d)
    new_p = [[] for _ in range(6)]
    for l in range(DEPTH):
        lw = {"norm_mix": norm_mix[l][None, :], "w_in": _pack_in_proj(w_in[l]), "w_out": w_out[l].astype(BF16),
              "bias": _lane_row([ml_i_bias[l], ml_f_bias[l], ssd_dt_bias[l]]),
              "alog": _lane_row([zeros12, ssd_a_log[l]]),
              "conv_w": ssd_conv_w[l], "conv_b": ssd_conv_b[l][None, :], "ml_norm": ml_norm[l][None, :],
              "ssd_d": jnp.repeat(ssd_d[l], SSD_HEADDIM)[None, :], "ssd_norm": ssd_norm[l][None, :],
              "ret_norm": ret_norm[l][None, :]}
        wq, wo = w_ca_q[l].astype(BF16), w_ca_o[l].astype(BF16)
        g_ca, g_ffn = norm_ca[l][None, :], norm_ffn[l][None, :]
        last = l == DEPTH - 1
        j = l // 2
        if l % 2 == 0:
            ffn_w = (ffn_w_gate[j].astype(BF16), ffn_w_up[j].astype(BF16), ffn_w_down[j].astype(BF16))
        else:
            wrt = jnp.pad(moe_w_router[j].T, ((0, 2 * N_EXPERTS - N_EXPERTS), (0, 0)))
            ffn_w = (wrt, moe_w_gate[j].astype(BF16), moe_w_up[j].astype(BF16), moe_w_down[j].astype(BF16))

        def run_ffn(x2d, tm):
            if l % 2 == 0:
                return ffn_dense(x2d, g_ffn, *ffn_w, tm=tm)
            return moe_ffn(x2d, g_ffn, *ffn_w, norm_final[None, :], tm=tm, sub=_moe_sub_rows(tm), final_norm=last)

        xp, c1, n1, m1, conv1, ssd1, ret1 = mixer_prompt(xp, lw, consts, tm=256)
        st_p = (c1, n1, m1[:, 0, :ML_HEADS], conv1,
                ssd1.reshape(bp, SSD_HEADS, SSD_HEADDIM, SSD_STATE), ret1)
        xp = ca_prompt(xp, g_ca, wq, wo, mem_k_p, mem_v_p, l)
        xp = run_ffn(xp.reshape(bp * seq, d), 1024 if l % 2 else 512).reshape(bp, seq, d)

        proj = norm_matmul(xs, lw["norm_mix"], lw["w_in"], tn=D_PROJ // 3)
        y, st_s = mixer_sample(proj, lw, consts, sample_states, l, st_s)
        xs = norm_matmul(y, ones_row, lw["w_out"], residual=xs, norm=False)
        q = norm_matmul(xs, g_ca, wq)
        o = ca_sample(q.reshape(bs, CA_HEADS, CA_DH), cache_mem_k, cache_mem_v, l).reshape(bs, d)
        xs = norm_matmul(o, ones_row[:, :d], wo, residual=xs, norm=False)
        xs = run_ffn(xs, bs)

        for lst, a in zip(new_p, st_p):
            lst.append(a)

    shape5 = (DEPTH, bp, MEM_LEN, CA_HEADS, CA_DH)
    c_s, n_s, m_s, conv_s, ssd_s, ret_s = st_s
    return (xp, xs.reshape(bs, 1, d),
            jnp.stack(new_p[0]), jnp.stack(new_p[1]), jnp.stack(new_p[2]), jnp.stack(new_p[3]),
            jnp.stack(new_p[4]), jnp.stack(new_p[5]), mem_k_p.reshape(shape5), mem_v_p.reshape(shape5),
            c_s, n_s, m_s[:, :, :ML_HEADS], conv_s,
            ssd_s.reshape(DEPTH, bs, SSD_HEADS, SSD_HEADDIM, SSD_STATE), ret_s)
```

```python
import functools
import itertools
import math

import numpy as np
import jax
import jax.numpy as jnp
from jax import lax
from jax.experimental import pallas as pl
from jax.experimental.pallas import tpu as pltpu

F32 = jnp.float32
BF16 = jnp.bfloat16
HIGHEST = lax.Precision.HIGHEST

D_MODEL = 1024
DEPTH = 2
PAST_LEN = 16384
D_MIX = 2 * D_MODEL
ML_WIDTH = 3 * D_MIX // 8
ML_HEADS = 6
ML_DH = ML_WIDTH // ML_HEADS
SSD_WIDTH = 3 * D_MIX // 8
SSD_HEADDIM = 64
SSD_HEADS = SSD_WIDTH // SSD_HEADDIM
SSD_STATE = 128
SSD_GROUPS = 2
SSD_GROUP_WIDTH = SSD_WIDTH // SSD_GROUPS
SSD_CONV_DIM = SSD_WIDTH + 2 * SSD_GROUPS * SSD_STATE
CONV_W = 4
RET_WIDTH = D_MIX // 4
RET_HEADS = 4
RET_DH = RET_WIDTH // RET_HEADS
ROPE_BASE = 10000.0
CHUNK = 128
MEM_LEN = 256
CA_HEADS = 4
CA_DH = D_MODEL // CA_HEADS
D_FF = 11 * D_MODEL // 4
N_EXPERTS = 8
TOP_K = 2
D_EXPERT = D_FF // 2
EPS = 1e-6

LANES = 128
BF16_ROWS = 16
MXU_ROWS = 128

OFF_QML = 0
OFF_KML = OFF_QML + ML_WIDTH
OFF_VML = OFF_KML + ML_WIDTH
OFF_OML = OFF_VML + ML_WIDTH
OFF_Z = OFF_OML + ML_WIDTH
OFF_XBC = OFF_Z + SSD_WIDTH
OFF_QR = OFF_XBC + SSD_CONV_DIM
OFF_KR = OFF_QR + RET_WIDTH
OFF_VR = OFF_KR + RET_WIDTH
OFF_GR = OFF_VR + RET_WIDTH
OFF_SMALL = OFF_GR + RET_WIDTH
D_PROJ = OFF_SMALL + LANES
LANE_I = 0
LANE_F = ML_HEADS
LANE_DT = 2 * ML_HEADS

VMEM_LIMIT = 60000 * 1024


def _cparams(sem):
    return pltpu.CompilerParams(dimension_semantics=sem, vmem_limit_bytes=VMEM_LIMIT)


def _const_spec(shape):
    nd = len(shape)
    return pl.BlockSpec(shape, lambda *_: (0,) * nd, pipeline_mode=pl.Buffered(1))


def _rms(x, g):
    return x * lax.rsqrt(jnp.mean(x * x, axis=-1, keepdims=True) + EPS) * g


def _sigmoid(x):
    return 1.0 / (1.0 + jnp.exp(-x))


def _silu(x):
    return x * _sigmoid(x)


def _dot(a, b):
    return jnp.dot(a, b, preferred_element_type=F32)


def _dot_nt(a, b):
    return lax.dot_general(a, b, (((1,), (1,)), ((), ())), preferred_element_type=F32)


def _dot_exact(a, b):
    return jnp.dot(a, b, preferred_element_type=F32, precision=HIGHEST)


def _split3(v):
    v1 = v.astype(BF16)
    r1 = v - v1.astype(F32)
    v2 = r1.astype(BF16)
    v3 = (r1 - v2.astype(F32)).astype(BF16)
    return v1, v2, v3


def _norm_matmul_kernel(*refs, norm, residual):
    x_ref, g_ref, w_ref = refs[:3]
    o_ref = refs[-1]
    x = x_ref[...].astype(F32)
    if norm:
        x = _rms(x, g_ref[...])
    acc = _dot(x.astype(BF16), w_ref[...])
    if residual:
        acc = acc + refs[3][...]
    o_ref[...] = acc


def norm_matmul(x, g, w, residual=None, *, norm=True, tm=None, tn=None):
    m, k = x.shape
    n = w.shape[1]
    tm = tm or min(m, 512)
    tn = tn or min(n, 1024)
    in_specs = [pl.BlockSpec((tm, k), lambda j, i: (i, 0)),
                pl.BlockSpec((1, k), lambda j, i: (0, 0)),
                pl.BlockSpec((k, tn), lambda j, i: (0, j))]
    args = [x, g, w]
    if residual is not None:
        in_specs.append(pl.BlockSpec((tm, tn), lambda j, i: (i, j)))
        args.append(residual)
    return pl.pallas_call(
        functools.partial(_norm_matmul_kernel, norm=norm, residual=residual is not None),
        grid=(pl.cdiv(n, tn), m // tm),
        in_specs=in_specs,
        out_specs=pl.BlockSpec((tm, tn), lambda j, i: (i, j)),
        out_shape=jax.ShapeDtypeStruct((m, n), F32),
        compiler_params=_cparams(("parallel", "parallel")),
        name="norm_matmul",
    )(*args)


def _kv_proj_kernel(x_ref, g_ref, wk_ref, wv_ref, k_ref, v_ref):
    xn = _rms(x_ref[...], g_ref[...]).astype(BF16)
    k_ref[...] = _dot(xn, wk_ref[...])
    v_ref[...] = _dot(xn, wv_ref[...])


def kv_proj(mem2d, g, wk, wv, *, tm=512):
    m, d = mem2d.shape
    depth = wk.shape[0]
    out = jax.ShapeDtypeStruct((depth, m, d), F32)
    return pl.pallas_call(
        _kv_proj_kernel,
        grid=(depth, m // tm),
        in_specs=[pl.BlockSpec((tm, d), lambda l, i: (i, 0)),
                  pl.BlockSpec((None, 1, d), lambda l, i: (l, 0, 0)),
                  pl.BlockSpec((None, d, d), lambda l, i: (l, 0, 0)),
                  pl.BlockSpec((None, d, d), lambda l, i: (l, 0, 0))],
        out_specs=[pl.BlockSpec((None, tm, d), lambda l, i: (l, i, 0)),
                   pl.BlockSpec((None, tm, d), lambda l, i: (l, i, 0))],
        out_shape=[out, out],
        compiler_params=_cparams(("parallel", "parallel")),
        name="kv_proj",
    )(mem2d, g, wk, wv)


def _ca_prompt_kernel(x_ref, g_ref, wq_ref, wo_ref, k_ref, v_ref, o_ref):
    x = x_ref[...]
    xn = _rms(x, g_ref[...]).astype(BF16)
    q = _dot(xn, wq_ref[...])
    kb = k_ref[...].astype(BF16)
    vb = v_ref[...].astype(BF16)
    heads = [slice(h * CA_DH, (h + 1) * CA_DH) for h in range(CA_HEADS)]

    def scores(sl):
        return _dot_nt(q[:, sl].astype(BF16), kb[:, sl]) * (CA_DH ** -0.5)

    outs = []
    s_next = scores(heads[0])
    for h, sl in enumerate(heads):
        s = s_next
        if h + 1 < CA_HEADS:
            s_next = scores(heads[h + 1])
        p = jnp.exp(s - jnp.max(s, axis=-1, keepdims=True))
        p = p / jnp.sum(p, axis=-1, keepdims=True)
        outs.append(_dot(p.astype(BF16), vb[:, sl]).astype(BF16))
    o = jnp.concatenate(outs, axis=-1)
    o_ref[...] = x + _dot(o, wo_ref[...])


def ca_prompt(x, g, wq, wo, mem_k, mem_v, layer, *, tm=512):
    b, length, d = x.shape
    mlen = mem_k.shape[2]
    return pl.pallas_call(
        _ca_prompt_kernel,
        grid=(b, length // tm),
        in_specs=[pl.BlockSpec((None, tm, d), lambda i, j: (i, j, 0)),
                  _const_spec((1, d)), _const_spec((d, d)), _const_spec((d, d)),
                  pl.BlockSpec((None, None, mlen, d), lambda i, j: (layer, i, 0, 0)),
                  pl.BlockSpec((None, None, mlen, d), lambda i, j: (layer, i, 0, 0))],
        out_specs=pl.BlockSpec((None, tm, d), lambda i, j: (i, j, 0)),
        out_shape=jax.ShapeDtypeStruct(x.shape, F32),
        compiler_params=_cparams(("parallel", "parallel")),
        name="ca_prompt",
    )(x, g, wq, wo, mem_k, mem_v)


def _ca_sample_kernel(q_ref, k_ref, v_ref, o_ref, *, bt):
    for j in range(bt):
        kq = k_ref[j] * (q_ref[j] * (CA_DH ** -0.5))
        s = jnp.sum(kq, axis=-1, keepdims=True)
        p = jnp.exp(s - jnp.max(s, axis=0, keepdims=True))
        o = jnp.sum(p * v_ref[j], axis=0)
        o_ref[j] = o / jnp.sum(p, axis=0)


def ca_sample(q, mem_k, mem_v, layer, *, bt=8):
    b = q.shape[0]
    blk = (None, bt) + mem_k.shape[2:]
    return pl.pallas_call(
        functools.partial(_ca_sample_kernel, bt=bt),
        grid=(b // bt,),
        in_specs=[pl.BlockSpec((bt,) + q.shape[1:], lambda i: (i, 0, 0)),
                  pl.BlockSpec(blk, lambda i: (layer, i, 0, 0, 0)),
                  pl.BlockSpec(blk, lambda i: (layer, i, 0, 0, 0))],
        out_specs=pl.BlockSpec((bt,) + q.shape[1:], lambda i: (i, 0, 0)),
        out_shape=jax.ShapeDtypeStruct(q.shape, F32),
        compiler_params=_cparams(("parallel",)),
        name="ca_sample",
    )(q, mem_k, mem_v)


def _ffn_kernel(x_ref, g_ref, wg_ref, wu_ref, wd_ref, o_ref, a_ref, *, fchunk):
    x = x_ref[...]
    xn = _rms(x, g_ref[...]).astype(BF16)
    dff = wg_ref.shape[1]
    for f0 in range(0, dff, fchunk):
        gate = _dot(xn, wg_ref[:, f0:f0 + fchunk])
        up = _dot(xn, wu_ref[:, f0:f0 + fchunk])
        a_ref[:, f0:f0 + fchunk] = (_silu(gate) * up).astype(BF16)
    o_ref[...] = x + _dot(a_ref[...], wd_ref[...])


def ffn_dense(x, g, wg, wu, wd, *, tm=512):
    m, d = x.shape
    dff = wg.shape[1]
    return pl.pallas_call(
        functools.partial(_ffn_kernel, fchunk=dff // 2),
        grid=(m // tm,),
        in_specs=[pl.BlockSpec((tm, d), lambda i: (i, 0)),
                  _const_spec((1, d)), _const_spec((d, dff)), _const_spec((d, dff)), _const_spec((dff, d))],
        out_specs=pl.BlockSpec((tm, d), lambda i: (i, 0)),
        out_shape=jax.ShapeDtypeStruct((m, d), F32),
        scratch_shapes=[pltpu.VMEM((tm, dff), BF16)],
        compiler_params=_cparams(("parallel",)),
        name="ffn_dense",
    )(x, g, wg, wu, wd)


def _moe_kernel(x_ref, g_ref, wrt_ref, wg_ref, wu_ref, wd_ref, gf_ref, o_ref,
                xn_ref, gate_t_ref, pos_t_ref, *, tm, sub, final_norm):
    e = pl.program_id(1)
    ne = pl.num_programs(1)
    epad = gate_t_ref.shape[0]

    @pl.when(e == 0)
    def _route():
        x = x_ref[...]
        xn = _rms(x, g_ref[...])
        xn_ref[...] = xn.astype(BF16)
        x_hi = xn.astype(BF16)
        x_lo = (xn - x_hi.astype(F32)).astype(BF16)
        w = wrt_ref[...]
        w_hi = w.astype(BF16)
        w_lo = (w - w_hi.astype(F32)).astype(BF16)
        logits = _dot_nt(w_hi, x_hi) + (_dot_nt(w_hi, x_lo) + _dot_nt(w_lo, x_hi))
        row = lax.broadcasted_iota(jnp.int32, (epad, tm), 0)
        logits = jnp.where(row < N_EXPERTS, logits, -jnp.inf)
        m1 = jnp.max(logits, axis=0, keepdims=True)
        i1 = jnp.min(jnp.where(logits == m1, row, epad), axis=0, keepdims=True)
        rest = jnp.where(row == i1, -jnp.inf, logits)
        m2 = jnp.max(rest, axis=0, keepdims=True)
        i2 = jnp.min(jnp.where(rest == m2, row, epad), axis=0, keepdims=True)
        e2 = jnp.exp(m2 - m1)
        den = 1.0 + e2
        gate_t = jnp.where(row == i1, 1.0 / den, 0.0) + jnp.where(row == i2, e2 / den, 0.0)
        sel_t = jnp.where((row == i1) | (row == i2), 1.0, 0.0)
        r_i = lax.broadcasted_iota(jnp.int32, (tm, tm), 0)
        c_i = lax.broadcasted_iota(jnp.int32, (tm, tm), 1)
        upper = jnp.where(r_i < c_i, 1.0, 0.0).astype(BF16)
        pos_t = _dot(sel_t.astype(BF16), upper)
        pos_t = jnp.where(sel_t > 0, pos_t, -1.0)
        gate_t_ref[...] = gate_t
        pos_t_ref[...] = pos_t
        o_ref[...] = x

    pos_row = pos_t_ref[pl.ds(e, 1), :]
    gate_row = gate_t_ref[pl.ds(e, 1), :]
    count = jnp.sum(jnp.where(pos_row >= 0, 1.0, 0.0)).astype(jnp.int32)
    half = sub // 2
    n_full = (count + (half - 1)) // sub
    done = n_full * sub

    def expert_rows(base, rows):
        slot = lax.broadcasted_iota(jnp.int32, (rows, tm), 0).astype(F32) + base.astype(F32)
        hit = pos_row == slot
        gather = jnp.where(hit, 1.0, 0.0).astype(BF16)
        gate = jnp.sum(jnp.where(hit, gate_row, 0.0), axis=-1, keepdims=True)
        xg = _dot(gather, xn_ref[...]).astype(BF16)
        act = (_silu(_dot(xg, wg_ref[...])) * _dot(xg, wu_ref[...])).astype(BF16)
        ye = (gate * _dot(act, wd_ref[...])).astype(BF16)
        o_ref[...] += lax.dot_general(gather, ye, (((0,), (0,)), ((), ())), preferred_element_type=F32)

    def full_body(j, carry):
        expert_rows(j * sub, sub)
        return carry

    lax.fori_loop(0, n_full, full_body, 0)

    @pl.when(count > done)
    def _remainder():
        expert_rows(done, half)

    if final_norm:
        @pl.when(e == ne - 1)
        def _final():
            o_ref[...] = _rms(o_ref[...], gf_ref[...])


def _moe_sub_rows(tm):
    mean = tm * TOP_K // N_EXPERTS
    return min(tm, max(MXU_ROWS, -(-mean // MXU_ROWS) * MXU_ROWS))


def moe_ffn(x, g, w_router_t, wg, wu, wd, g_final, *, tm, sub, final_norm):
    m, d = x.shape
    ne, _, de = wg.shape
    epad = w_router_t.shape[0]
    return pl.pallas_call(
        functools.partial(_moe_kernel, tm=tm, sub=sub, final_norm=final_norm),
        grid=(m // tm, ne),
        in_specs=[pl.BlockSpec((tm, d), lambda i, e: (i, 0)),
                  pl.BlockSpec((1, d), lambda i, e: (0, 0)),
                  pl.BlockSpec((epad, d), lambda i, e: (0, 0)),
                  pl.BlockSpec((None, d, de), lambda i, e: (e, 0, 0)),
                  pl.BlockSpec((None, d, de), lambda i, e: (e, 0, 0)),
                  pl.BlockSpec((None, de, d), lambda i, e: (e, 0, 0)),
                  pl.BlockSpec((1, d), lambda i, e: (0, 0))],
        out_specs=pl.BlockSpec((tm, d), lambda i, e: (i, 0)),
        out_shape=jax.ShapeDtypeStruct((m, d), F32),
        scratch_shapes=[pltpu.VMEM((tm, d), BF16),
                        pltpu.VMEM((epad, tm), F32), pltpu.VMEM((epad, tm), F32)],
        compiler_params=_cparams(("parallel", "arbitrary")),
        name="moe_ffn",
    )(x, g, w_router_t, wg, wu, wd, g_final)


def _retention_constants(c):
    log_g = np.log1p(-np.exp2(-5.0 - np.arange(RET_HEADS, dtype=np.float32))).astype(np.float32)
    j = np.arange(c, dtype=np.float32)
    rel = j[:, None] - j[None, :]
    w_intra = np.where(rel >= 0, np.exp(rel[None] * log_g[:, None, None]), 0.0).astype(np.float32)
    w_inter = np.exp((j[None, :] + 1.0) * log_g[:, None]).astype(np.float32)
    w_end = np.exp((c - 1.0 - j[None, :]) * log_g[:, None]).astype(np.float32)
    decay = np.exp(c * log_g).astype(np.float32)
    cols = np.zeros((c, LANES), np.float32)
    cols[:, :RET_HEADS] = w_inter.T
    cols[:, RET_HEADS:2 * RET_HEADS] = w_end.T
    return w_intra, cols, [float(d) for d in decay]


def _rotary_tables(pos):
    half = RET_DH // 2
    inv = ROPE_BASE ** (-jnp.arange(half, dtype=F32) / half)
    ang = pos[:, None] * inv[None, :]
    cos, sin = jnp.cos(ang), jnp.sin(ang)
    return jnp.concatenate([cos, cos], axis=-1), jnp.concatenate([-sin, sin], axis=-1)


def _head_expand_matrix():
    e = np.zeros((LANES, SSD_WIDTH), np.float32)
    for h in range(SSD_HEADS):
        e[LANE_DT + h, h * SSD_HEADDIM:(h + 1) * SSD_HEADDIM] = 1.0
    return e


def _softplus_parts(v):
    sp = jnp.log1p(jnp.exp(-jnp.abs(v)))
    return jnp.minimum(v, 0.0) - sp, jnp.maximum(v, 0.0) + sp


def _mixer_prompt_kernel(x_ref, xnext_ref, g_ref, win_ref, wout_ref, bias_ref, alog_ref, convw_ref, convb_ref,
                         mlg_ref, ssdd_ref, ssdg_ref, retg_ref, cos_ref, sin_ref, retw_ref, retcol_ref,
                         expand_ref,
                         xo_ref, c_ref, n_ref, m_ref, conv_ref, ssd_ref, ret_ref,
                         proj_s, cs_s, cst_s, smt_s, dtx_s, acx_s, y_s, u_s, *, tm, tiles_per_seq, ret_decay):
    c = CHUNK
    i = pl.program_id(0)

    @pl.when(i % tiles_per_seq == 0)
    def _init():
        c_ref[...] = jnp.zeros_like(c_ref)
        n_ref[...] = jnp.zeros_like(n_ref)
        m_ref[...] = jnp.zeros_like(m_ref)
        ssd_ref[...] = jnp.zeros_like(ssd_ref)
        ret_ref[...] = jnp.zeros_like(ret_ref)
        u_s[0:8, :] = jnp.zeros((8, SSD_CONV_DIM), F32)

    seg_w = 512
    r_i = lax.broadcasted_iota(jnp.int32, (c, c), 0)
    c_i = lax.broadcasted_iota(jnp.int32, (c, c), 1)
    causal = r_i >= c_i
    tril = jnp.where(causal, 1.0, 0.0).astype(BF16)
    lane1 = lax.broadcasted_iota(jnp.int32, (1, LANES), 1)
    lane_c = lax.broadcasted_iota(jnp.int32, (c, LANES), 1)
    a_vec = -jnp.exp(alog_ref[...])
    expand3 = expand_ref[...]
    bias = bias_ref[...]

    def cumsum_rows(v):
        p = _dot(tril, jnp.concatenate(_split3(v), axis=-1))
        return p[:, :LANES] + p[:, LANES:2 * LANES] + p[:, 2 * LANES:]

    def expand_heads(v):
        return _dot(jnp.concatenate(_split3(v), axis=-1), expand3)

    seg_offs = list(range(0, D_PROJ, seg_w))
    assert seg_offs[-1] == OFF_SMALL

    def prepare(src_ref, slot):
        xn = _rms(src_ref[...], g_ref[...]).astype(BF16)
        for off in seg_offs[-1:] + seg_offs[:-1]:
            wdt = min(seg_w, D_PROJ - off)
            proj_s[slot, :, off:off + wdt] = _dot(xn, win_ref[:, off:off + wdt])
            yield
            if off != OFF_SMALL:
                continue
            is_f = (lane_c >= LANE_F) & (lane_c < LANE_DT)
            is_dt = (lane_c >= LANE_DT) & (lane_c < LANE_DT + SSD_HEADS)
            for ci in range(tm // c):
                rows = slice(ci * c, (ci + 1) * c)
                small = proj_s[slot, rows, OFF_SMALL:OFF_SMALL + LANES] + bias
                logf, dt = _softplus_parts(small)
                cs = cumsum_rows(jnp.where(is_f, logf, jnp.where(is_dt, dt * a_vec, 0.0)))
                cs_s[slot, rows, :] = cs
                cst_s[slot, :, rows] = cs.T
                smt_s[slot, :, rows] = small.T
                dtx_s[slot, rows, :] = expand_heads(jnp.where(is_dt, dt, 0.0))
                acx_s[slot, rows, :] = expand_heads(jnp.where(is_dt, cs, 0.0))
                yield

    @pl.when(i == 0)
    def _first():
        for _ in prepare(x_ref, 0):
            pass

    cur = i % 2
    proj_cur = proj_s.at[cur]
    x = x_ref[...]

    def chunk_body(ci):
        rows = slice(ci * c, (ci + 1) * c)
        small = proj_cur[rows, OFF_SMALL:OFF_SMALL + LANES] + bias
        cs = cs_s[cur, rows, :]
        cs_t = cst_s[cur, :, rows]
        small_t = smt_s[cur, :, rows]

        m_vec = m_ref[...]
        m_out = m_vec

        def ml_front(h):
            q = proj_cur[rows, OFF_QML + h * ML_DH:OFF_QML + (h + 1) * ML_DH]
            k = proj_cur[rows, OFF_KML + h * ML_DH:OFF_KML + (h + 1) * ML_DH] * (ML_DH ** -0.5)
            v = proj_cur[rows, OFF_VML + h * ML_DH:OFF_VML + (h + 1) * ML_DH]
            qb, kb, vb = q.astype(BF16), k.astype(BF16), v.astype(BF16)
            cm = c_ref[h]
            return q, k, v, kb, vb, cm, _dot_nt(qb, kb), _dot_nt(qb, cm.astype(BF16))

        def ml_finish(hs, num_intra, num_inter, den):
            hh = (num_intra + num_inter) / den
            hn = hh * lax.rsqrt(jnp.mean(hh * hh, axis=-1, keepdims=True) + EPS)
            og = proj_cur[rows, OFF_OML + hs.start:OFF_OML + hs.stop]
            y_s[rows, hs] = (hn * mlg_ref[:, hs] * _sigmoid(og)).astype(BF16)

        front = ml_front(0)
        pending = None
        for h in range(ML_HEADS):
            q, k, v, kb, vb, cm, qk, cq = front
            if h + 1 < ML_HEADS:
                front = ml_front(h + 1)
            hs = slice(h * ML_DH, (h + 1) * ML_DH)
            b_col = cs[:, LANE_F + h:LANE_F + h + 1]
            b_row = cs_t[LANE_F + h:LANE_F + h + 1, :]
            i_col = small[:, LANE_I + h:LANE_I + h + 1]
            i_row = small_t[LANE_I + h:LANE_I + h + 1, :]
            m_prev = m_vec[:, h:h + 1]
            lw = jnp.where(causal, b_col + (i_row - b_row), -jnp.inf)
            log_a = b_col + m_prev
            m_t = jnp.maximum(log_a, jnp.max(lw, axis=-1, keepdims=True))
            w_intra = jnp.exp(lw - m_t)
            w_inter = jnp.exp(log_a - m_t)
            s = w_intra * qk
            n_row = n_ref[h:h + 1, :]
            num_intra = _dot(s.astype(BF16), vb)
            den = jnp.sum(s, axis=-1, keepdims=True) + w_inter * jnp.sum(q * n_row, axis=-1, keepdims=True)
            den = jnp.maximum(jnp.abs(den), jnp.exp(-m_t))
            m_new = m_t[c - 1:c, :]
            decay = jnp.exp(log_a[c - 1:c, :] - m_new)
            w_end = jnp.exp(b_col[c - 1:c, :] - b_col + i_col - m_new)
            c_ref[h] = decay * cm + _dot((v * w_end).T.astype(BF16), kb)
            n_ref[h:h + 1, :] = decay * n_row + jnp.sum(k * w_end, axis=0, keepdims=True)
            m_out = jnp.where(lane1 == h, m_new, m_out)
            if pending is not None:
                ml_finish(*pending)
            pending = (hs, num_intra, w_inter * cq, den)
            yield
        ml_finish(*pending)
        m_ref[...] = m_out

        u = proj_cur[rows, OFF_XBC:OFF_XBC + SSD_CONV_DIM]
        u_s[8:8 + c, :] = u
        conv = convb_ref[...] + convw_ref[3:4, :] * u
        for jj in range(CONV_W - 1):
            conv = conv + convw_ref[jj:jj + 1, :] * u_s[5 + jj:5 + jj + c, :]
        u_s[0:8, :] = u_s[c:c + 8, :]
        xc = _silu(conv)
        xs = xc[:, :SSD_WIDTH]
        dt_exp = dtx_s[cur, rows, :]
        acum_exp = acx_s[cur, rows, :]
        xdt = xs * dt_exp
        w_end_exp = jnp.exp(acum_exp[c - 1:c, :] - acum_exp)
        xw = xdt * w_end_exp
        ea_exp = jnp.exp(acum_exp)
        lane_lo = lane_c < SSD_HEADDIM
        hpg = SSD_HEADS // SSD_GROUPS
        y_cols = []
        yield
        for g in range(SSD_GROUPS):
            gw = slice(g * SSD_GROUP_WIDTH, (g + 1) * SSD_GROUP_WIDTH)
            bg = xc[:, SSD_WIDTH + g * SSD_STATE:SSD_WIDTH + (g + 1) * SSD_STATE].astype(BF16)
            cg = xc[:, SSD_WIDTH + (SSD_GROUPS + g) * SSD_STATE:
                    SSD_WIDTH + (SSD_GROUPS + g + 1) * SSD_STATE].astype(BF16)
            cb = _dot_nt(cg, bg)
            st = ssd_ref[gw, :]
            inter = _dot_nt(cg, st.astype(BF16)) * ea_exp[:, gw]
            def decayed_scores(pr):
                out = []
                for hidx in (g * hpg + 2 * pr, g * hpg + 2 * pr + 1):
                    a_col = cs[:, LANE_DT + hidx:LANE_DT + hidx + 1]
                    a_row = cs_t[LANE_DT + hidx:LANE_DT + hidx + 1, :]
                    dec = jnp.exp(jnp.where(causal, a_col - a_row, -jnp.inf))
                    out.append((dec * cb).astype(BF16))
                return out

            att = decayed_scores(0)
            for pr in range(hpg // 2):
                h0 = g * hpg + 2 * pr
                xp = xdt[:, h0 * SSD_HEADDIM:(h0 + 2) * SSD_HEADDIM].astype(BF16)
                pair = [_dot(att[0], xp), _dot(att[1], xp)]
                if pr + 1 < hpg // 2:
                    att = decayed_scores(pr + 1)
                y_cols.append(jnp.where(lane_lo, pair[0], pair[1]) + inter[:, pr * LANES:(pr + 1) * LANES])
                yield
            upd = _dot(xw[:, gw].T.astype(BF16), bg)
            for r in range(hpg):
                hidx = g * hpg + r
                scal = jnp.exp(cs[c - 1:c, LANE_DT + hidx:LANE_DT + hidx + 1])
                hr = slice(hidx * SSD_HEADDIM, (hidx + 1) * SSD_HEADDIM)
                ssd_ref[hr, :] = scal * ssd_ref[hr, :] + upd[r * SSD_HEADDIM:(r + 1) * SSD_HEADDIM, :]
            yield
        ys = jnp.concatenate(y_cols, axis=-1) + ssdd_ref[...] * xs
        yz = ys * _silu(proj_cur[rows, OFF_Z:OFF_Z + SSD_WIDTH])
        for g in range(SSD_GROUPS):
            gw = slice(g * SSD_GROUP_WIDTH, (g + 1) * SSD_GROUP_WIDTH)
            seg = yz[:, gw]
            seg = seg * lax.rsqrt(jnp.mean(seg * seg, axis=-1, keepdims=True) + EPS) * ssdg_ref[:, gw]
            y_s[rows, ML_WIDTH + g * SSD_GROUP_WIDTH:ML_WIDTH + (g + 1) * SSD_GROUP_WIDTH] = seg.astype(BF16)
        yield

        cosb = cos_ref[rows, :]
        sinb = sin_ref[rows, :]
        retcol = retcol_ref[...]
        def ret_front(h):
            q = proj_cur[rows, OFF_QR + h * RET_DH:OFF_QR + (h + 1) * RET_DH]
            k = proj_cur[rows, OFF_KR + h * RET_DH:OFF_KR + (h + 1) * RET_DH]
            v = proj_cur[rows, OFF_VR + h * RET_DH:OFF_VR + (h + 1) * RET_DH]
            qr = q * cosb + pltpu.roll(q, RET_DH // 2, 1) * sinb
            kr = (k * cosb + pltpu.roll(k, RET_DH // 2, 1) * sinb) * (RET_DH ** -0.5)
            qb, kb, vb = qr.astype(BF16), kr.astype(BF16), v.astype(BF16)
            sm = ret_ref[h]
            return kr, vb, sm, _dot_nt(qb, kb), _dot(qb, sm.astype(BF16))

        front = ret_front(0)
        for h in range(RET_HEADS):
            kr, vb, sm, qk, qs = front
            if h + 1 < RET_HEADS:
                front = ret_front(h + 1)
            hs = slice(h * RET_DH, (h + 1) * RET_DH)
            gr = proj_cur[rows, OFF_GR + h * RET_DH:OFF_GR + (h + 1) * RET_DH]
            att = qk * retw_ref[h]
            o = _dot(att.astype(BF16), vb) + qs * retcol[:, h:h + 1]
            kw = kr * retcol[:, RET_HEADS + h:RET_HEADS + h + 1]
            ret_ref[h] = ret_decay[h] * sm + _dot(kw.T.astype(BF16), vb)
            on = o * lax.rsqrt(jnp.mean(o * o, axis=-1, keepdims=True) + EPS)
            y_s[rows, ML_WIDTH + SSD_WIDTH + h * RET_DH:ML_WIDTH + SSD_WIDTH + (h + 1) * RET_DH] = (
                on * retg_ref[:, hs] * _silu(gr)).astype(BF16)
            yield

    stages_per_chunk = ML_HEADS + 1 + SSD_GROUPS * (SSD_HEADS // SSD_GROUPS // 2 + 1) + 1 + RET_HEADS
    n_segments = len(seg_offs) + tm // c
    every = max(1, (tm // c) * stages_per_chunk // n_segments)
    segments = prepare(xnext_ref, (i + 1) % 2)
    for k, _ in enumerate(itertools.chain.from_iterable(chunk_body(ci) for ci in range(tm // c))):
        if k % every == 0:
            next(segments, None)
    for _ in segments:
        pass
    xo_ref[...] = x + _dot(y_s[...], wout_ref[...])
    conv_ref[...] = u_s[5:8, :]


def mixer_prompt(x, lw, consts, *, tm):
    b, length, d = x.shape
    c = CHUNK
    tps = length // tm
    n_tiles = b * tps
    x2d = x.reshape(b * length, d)
    row = lambda w: _const_spec((1, w))
    state_spec = lambda *shape: pl.BlockSpec((None,) + shape, lambda i: (i // tps,) + (0,) * len(shape))
    out_shapes = [jax.ShapeDtypeStruct(x2d.shape, F32),
                  jax.ShapeDtypeStruct((b, ML_HEADS, ML_DH, ML_DH), F32),
                  jax.ShapeDtypeStruct((b, ML_HEADS, ML_DH), F32),
                  jax.ShapeDtypeStruct((b, 1, LANES), F32),
                  jax.ShapeDtypeStruct((b, CONV_W - 1, SSD_CONV_DIM), F32),
                  jax.ShapeDtypeStruct((b, SSD_WIDTH, SSD_STATE), F32),
                  jax.ShapeDtypeStruct((b, RET_HEADS, RET_DH, RET_DH), F32)]
    outs = pl.pallas_call(
        functools.partial(_mixer_prompt_kernel, tm=tm, tiles_per_seq=tps, ret_decay=consts["ret_decay"]),
        grid=(n_tiles,),
        in_specs=[pl.BlockSpec((tm, d), lambda i: (i, 0)),
                  pl.BlockSpec((tm, d), lambda i: (jnp.minimum(i + 1, n_tiles - 1), 0)),
                  row(d), _const_spec((d, D_PROJ)), _const_spec((D_MIX, d)),
                  row(LANES), row(LANES), _const_spec((CONV_W, SSD_CONV_DIM)), row(SSD_CONV_DIM),
                  row(ML_WIDTH), row(SSD_WIDTH), row(SSD_WIDTH), row(RET_WIDTH),
                  pl.BlockSpec((tm, LANES), lambda i: (i % tps, 0)),
                  pl.BlockSpec((tm, LANES), lambda i: (i % tps, 0)),
                  _const_spec((RET_HEADS, c, c)), _const_spec((c, LANES)),
                  _const_spec((3 * LANES, SSD_WIDTH))],
        out_specs=[pl.BlockSpec((tm, d), lambda i: (i, 0)),
                   state_spec(ML_HEADS, ML_DH, ML_DH), state_spec(ML_HEADS, ML_DH), state_spec(1, LANES),
                   state_spec(CONV_W - 1, SSD_CONV_DIM), state_spec(SSD_WIDTH, SSD_STATE),
                   state_spec(RET_HEADS, RET_DH, RET_DH)],
        out_shape=out_shapes,
        scratch_shapes=[pltpu.VMEM((2, tm, D_PROJ), F32),
                        pltpu.VMEM((2, tm, LANES), F32), pltpu.VMEM((2, LANES, tm), F32),
                        pltpu.VMEM((2, LANES, tm), F32), pltpu.VMEM((2, tm, SSD_WIDTH), F32),
                        pltpu.VMEM((2, tm, SSD_WIDTH), F32), pltpu.VMEM((tm, D_MIX), BF16),
                        pltpu.VMEM((c + 8, SSD_CONV_DIM), F32)],
        compiler_params=_cparams(("arbitrary",)),
        name="mixer_prompt",
    )(x2d, x2d, lw["norm_mix"], lw["w_in"], lw["w_out"], lw["bias"], lw["alog"], lw["conv_w"], lw["conv_b"],
      lw["ml_norm"], lw["ssd_d"], lw["ssd_norm"], lw["ret_norm"], consts["cos_p"], consts["sin_p"],
      consts["ret_w"], consts["ret_cols"], consts["expand3"])
    return [outs[0].reshape(b, length, d)] + list(outs[1:])


N_SAMPLE_STATES = 6


def _mixer_sample_kernel(*refs, bt, ret_gamma, n_alias):
    (proj_ref, small_ref, bias_ref, alog_ref, convw_ref, convb_ref, mlg_ref, ssdd_ref,
     ssdg_ref, retg_ref, cos_ref, sin_ref, expand_ref,
     c_in, n_in, m_in, conv_in, ssd_in, ret_in) = refs[:19]
    (y_ref, c_out, n_out, m_out, conv_out, ssd_out, ret_out,
     wi_s, wt_s, en_s, dtx_s, eax_s) = refs[19 + n_alias:]
    small = small_ref[...] + bias_ref[...]
    logf, dt = _softplus_parts(small)
    lane_b = lax.broadcasted_iota(jnp.int32, (bt, LANES), 1)
    is_dt = (lane_b >= LANE_DT) & (lane_b < LANE_DT + SSD_HEADS)
    log_a = pltpu.roll(logf, LANES - LANE_F, 1) + m_in[...]
    m_t = jnp.maximum(log_a, small)
    m_out[...] = m_t
    wi_s[...] = jnp.exp(small - m_t)
    wt_s[...] = jnp.exp(log_a - m_t)
    en_s[...] = jnp.exp(-m_t)
    a_vec = -jnp.exp(alog_ref[...])
    expand = expand_ref[...]
    dtx_s[...] = _dot_exact(jnp.where(is_dt, dt, 0.0), expand)
    eax_s[...] = jnp.exp(_dot_exact(jnp.where(is_dt, dt * a_vec, 0.0), expand))

    sub16 = lax.broadcasted_iota(jnp.int32, (BF16_ROWS, LANES), 0)
    row_hi = lax.broadcasted_iota(jnp.int32, (LANES, LANES), 0) < SSD_HEADDIM

    def rows3(a, b, c):
        return jnp.where(sub16 == 0, a, jnp.where(sub16 == 1, b, jnp.where(sub16 == 2, c, 0.0))).astype(BF16)

    def split(v):
        hi = v.astype(BF16).astype(F32)
        return hi, v - hi

    def outer(a, b):
        a_hi, a_lo = split(a)
        b_hi, b_lo = split(b)
        return lax.dot_general(rows3(a_hi, a_hi, a_lo), rows3(b_hi, b_lo, b_hi), (((0,), (0,)), ((), ())),
                               preferred_element_type=F32)

    def row1(a):
        return jnp.where(sub16 == 0, a, 0.0).astype(BF16)

    cosb = cos_ref[...]
    sinb = sin_ref[...]
    hpg = SSD_HEADS // SSD_GROUPS

    sub_l = lax.broadcasted_iota(jnp.int32, (bt, LANES), 0)
    sub_w = lax.broadcasted_iota(jnp.int32, (bt, SSD_WIDTH), 0)

    def body(j):
        pick_l = lambda ref: jnp.sum(jnp.where(sub_l == j, ref[...], 0.0), axis=0, keepdims=True)
        pick_w = lambda ref: jnp.sum(jnp.where(sub_w == j, ref[...], 0.0), axis=0, keepdims=True)
        wi = pick_l(wi_s)
        wt = pick_l(wt_s)
        en = pick_l(en_s)
        def ml_front(h):
            q = proj_ref[j, :,OFF_QML + h * ML_DH:OFF_QML + (h + 1) * ML_DH]
            k = proj_ref[j, :,OFF_KML + h * ML_DH:OFF_KML + (h + 1) * ML_DH] * (ML_DH ** -0.5)
            v = proj_ref[j, :,OFF_VML + h * ML_DH:OFF_VML + (h + 1) * ML_DH]
            return q, k, outer(v, k)

        front = ml_front(0)
        for h in range(ML_HEADS):
            q, k, vk = front
            if h + 1 < ML_HEADS:
                front = ml_front(h + 1)
            hs = slice(h * ML_DH, (h + 1) * ML_DH)
            og = proj_ref[j, :,OFF_OML + h * ML_DH:OFF_OML + (h + 1) * ML_DH]
            w_in_h = wi[:, h:h + 1]
            w_tr_h = wt[:, h:h + 1]
            c_new = w_tr_h * c_in[j, h] + w_in_h * vk
            n_new = w_tr_h * n_in[j, h:h + 1, :] + w_in_h * k
            c_out[j, h] = c_new
            n_out[j, h:h + 1, :] = n_new
            num = _dot_nt(row1(q), c_new.astype(BF16))[0:1, :]
            den = jnp.sum(n_new * q, axis=-1, keepdims=True)
            den = jnp.maximum(jnp.abs(den), en[:, h:h + 1])
            hh = num / den
            hn = hh * lax.rsqrt(jnp.mean(hh * hh, axis=-1, keepdims=True) + EPS)
            y_ref[j, :,hs] = hn * mlg_ref[:, hs] * _sigmoid(og)
            yield
        u = proj_ref[j, :,OFF_XBC:OFF_XBC + SSD_CONV_DIM]
        prev = conv_in[j]
        conv = convb_ref[...] + convw_ref[CONV_W - 1:CONV_W, :] * u
        for jj in range(CONV_W - 1):
            conv = conv + convw_ref[jj:jj + 1, :] * prev[jj:jj + 1, :]
        conv_out[j, 0:CONV_W - 2, :] = prev[1:CONV_W - 1, :]
        conv_out[j, CONV_W - 2:CONV_W - 1, :] = u
        xc = _silu(conv)
        xs = xc[:, :SSD_WIDTH]
        xdt = xs * pick_w(dtx_s)
        ea = pick_w(eax_s)
        y_cols = []
        n_pairs = SSD_HEADS // 2

        def ssd_front(p):
            g = p // (hpg // 2)
            bg = xc[:, SSD_WIDTH + g * SSD_STATE:SSD_WIDTH + (g + 1) * SSD_STATE]
            pw = slice(2 * p * SSD_HEADDIM, (2 * p + 2) * SSD_HEADDIM)
            return pw, outer(xdt[:, pw], bg)

        front = ssd_front(0)
        for p in range(n_pairs):
            pw, xb = front
            if p + 1 < n_pairs:
                front = ssd_front(p + 1)
            g = p // (hpg // 2)
            cg = xc[:, SSD_WIDTH + (SSD_GROUPS + g) * SSD_STATE:SSD_WIDTH + (SSD_GROUPS + g + 1) * SSD_STATE]
            lo = pw.start
            decay = jnp.where(row_hi, ea[:, lo:lo + 1], ea[:, lo + SSD_HEADDIM:lo + SSD_HEADDIM + 1])
            st_new = decay * ssd_in[j, pw, :] + xb
            ssd_out[j, pw, :] = st_new
            y_cols.append(_dot_nt(row1(cg), st_new.astype(BF16))[0:1, :])
            yield
        ys = jnp.concatenate(y_cols, axis=-1) + ssdd_ref[...] * xs
        yz = ys * _silu(proj_ref[j, :,OFF_Z:OFF_Z + SSD_WIDTH])
        for g in range(SSD_GROUPS):
            gw = slice(g * SSD_GROUP_WIDTH, (g + 1) * SSD_GROUP_WIDTH)
            seg = yz[:, gw]
            y_ref[j, :,ML_WIDTH + g * SSD_GROUP_WIDTH:ML_WIDTH + (g + 1) * SSD_GROUP_WIDTH] = (
                seg * lax.rsqrt(jnp.mean(seg * seg, axis=-1, keepdims=True) + EPS) * ssdg_ref[:, gw])
        def ret_front(h):
            q = proj_ref[j, :,OFF_QR + h * RET_DH:OFF_QR + (h + 1) * RET_DH]
            k = proj_ref[j, :,OFF_KR + h * RET_DH:OFF_KR + (h + 1) * RET_DH]
            v = proj_ref[j, :,OFF_VR + h * RET_DH:OFF_VR + (h + 1) * RET_DH]
            qr = q * cosb + pltpu.roll(q, RET_DH // 2, 1) * sinb
            kr = (k * cosb + pltpu.roll(k, RET_DH // 2, 1) * sinb) * (RET_DH ** -0.5)
            return qr, outer(kr, v)

        front = ret_front(0)
        for h in range(RET_HEADS):
            qr, kv = front
            if h + 1 < RET_HEADS:
                front = ret_front(h + 1)
            hs = slice(h * RET_DH, (h + 1) * RET_DH)
            gr = proj_ref[j, :,OFF_GR + h * RET_DH:OFF_GR + (h + 1) * RET_DH]
            s_new = ret_gamma[h] * ret_in[j, h] + kv
            ret_out[j, h] = s_new
            o = _dot(row1(qr), s_new.astype(BF16))[0:1, :]
            on = o * lax.rsqrt(jnp.mean(o * o, axis=-1, keepdims=True) + EPS)
            y_ref[j, :,ML_WIDTH + SSD_WIDTH + h * RET_DH:ML_WIDTH + SSD_WIDTH + (h + 1) * RET_DH] = (
                on * retg_ref[:, hs] * _silu(gr))
            yield

    def body_pair(jj, carry):
        for _ in itertools.zip_longest(body(2 * jj), body(2 * jj + 1)):
            pass
        return carry

    lax.fori_loop(0, bt // 2, body_pair, 0)


def mixer_sample(proj, lw, consts, states, layer, prev_out=None, *, bt=8):
    b = proj.shape[0]
    row = lambda w: pl.BlockSpec((1, w), lambda i: (0, 0))
    blk = lambda *shape: pl.BlockSpec((bt,) + shape, lambda i: (i,) + (0,) * len(shape))
    lblk = lambda *shape: pl.BlockSpec((None, bt) + shape, lambda i: (layer, i) + (0,) * len(shape))
    state_specs = [lblk(ML_HEADS, ML_DH, ML_DH), lblk(ML_HEADS, ML_DH), lblk(LANES),
                   lblk(CONV_W - 1, SSD_CONV_DIM), lblk(SSD_WIDTH, SSD_STATE), lblk(RET_HEADS, RET_DH, RET_DH)]
    state_shapes = [jax.ShapeDtypeStruct(s.shape, F32) for s in states]
    proj3 = proj.reshape(b, 1, D_PROJ)
    small = proj[:, OFF_SMALL:OFF_SMALL + LANES]
    n_fixed = 13 + N_SAMPLE_STATES
    alias_args = list(prev_out) if prev_out is not None else []
    aliases = {n_fixed + k: 1 + k for k in range(len(alias_args))}
    outs = pl.pallas_call(
        functools.partial(_mixer_sample_kernel, bt=bt, ret_gamma=consts["ret_gamma"], n_alias=len(alias_args)),
        grid=(b // bt,),
        in_specs=[blk(1, D_PROJ), blk(LANES), row(LANES), row(LANES),
                  pl.BlockSpec((CONV_W, SSD_CONV_DIM), lambda i: (0, 0)), row(SSD_CONV_DIM),
                  row(ML_WIDTH), row(SSD_WIDTH), row(SSD_WIDTH), row(RET_WIDTH), row(LANES), row(LANES),
                  pl.BlockSpec((LANES, SSD_WIDTH), lambda i: (0, 0))] + state_specs
                 + [pl.BlockSpec(memory_space=pl.ANY)] * len(alias_args),
        out_specs=[blk(1, D_MIX)] + state_specs,
        out_shape=[jax.ShapeDtypeStruct((b, 1, D_MIX), F32)] + state_shapes,
        input_output_aliases=aliases,
        scratch_shapes=[pltpu.VMEM((bt, LANES), F32)] * 3 + [pltpu.VMEM((bt, SSD_WIDTH), F32)] * 2,
        compiler_params=_cparams(("parallel",)),
        name="mixer_sample",
    )(proj3, small, lw["bias"], lw["alog"], lw["conv_w"], lw["conv_b"], lw["ml_norm"], lw["ssd_d"],
      lw["ssd_norm"], lw["ret_norm"], consts["cos_s"], consts["sin_s"], consts["expand"],
      *states, *alias_args)
    return outs[0].reshape(b, D_MIX), list(outs[1:])


def _pack_in_proj(w):
    sizes = [ML_WIDTH] * 4 + [ML_HEADS, ML_HEADS, SSD_WIDTH, SSD_CONV_DIM, SSD_HEADS] + [RET_WIDTH] * 4
    offs = np.concatenate([[0], np.cumsum(sizes)])
    seg = lambda i: w[:, int(offs[i]):int(offs[i + 1])]
    small = jnp.concatenate([seg(4), seg(5), seg(8)], axis=1)
    small = jnp.pad(small, ((0, 0), (0, LANES - small.shape[1])))
    cols = [seg(0), seg(1), seg(2), seg(3), seg(6), seg(7), seg(9), seg(10), seg(11), seg(12), small]
    return jnp.concatenate(cols, axis=1).astype(BF16)


def _lane_row(parts):
    v = jnp.concatenate([p.astype(F32) for p in parts])
    return jnp.pad(v, (0, LANES - v.shape[0]))[None, :]


def kernel(x_prompt, x_sample, state_mlstm_c, state_mlstm_n, state_mlstm_m, state_ssd_conv, state_ssd, state_ret,
           cache_mem_k, cache_mem_v, mem_prompt,
           norm_mix, w_in, ml_i_bias, ml_f_bias, ml_norm, ssd_conv_w, ssd_conv_b, ssd_dt_bias, ssd_a_log, ssd_d,
           ssd_norm, ret_norm, w_out, norm_ca, norm_mem, w_ca_q, w_ca_k, w_ca_v, w_ca_o, norm_ffn,
           ffn_w_gate, ffn_w_up, ffn_w_down, moe_w_router, moe_w_gate, moe_w_up, moe_w_down, norm_final):
    bp, seq, d = x_prompt.shape
    bs = x_sample.shape[0]
    assert x_sample.shape[1] == 1 and seq % CHUNK == 0 and DEPTH % 2 == 0

    ret_w, ret_cols, ret_decay = _retention_constants(CHUNK)
    _, _, ret_gamma = _retention_constants(1)
    cos_p, sin_p = _rotary_tables(jnp.arange(seq, dtype=F32))
    cos_s, sin_s = _rotary_tables(PAST_LEN + jnp.arange(1, dtype=F32))
    consts = {"ret_w": jnp.asarray(ret_w), "ret_cols": jnp.asarray(ret_cols), "ret_decay": ret_decay,
              "ret_gamma": ret_gamma, "cos_p": cos_p, "sin_p": sin_p, "cos_s": cos_s, "sin_s": sin_s,
              "expand": jnp.asarray(_head_expand_matrix()),
              "expand3": jnp.asarray(np.tile(_head_expand_matrix(), (3, 1))).astype(BF16)}
    zeros12 = jnp.zeros((2 * ML_HEADS,), F32)
    ones_row = jnp.ones((1, D_MIX), F32)

    mem2d = mem_prompt.reshape(bp * MEM_LEN, d)
    mem_k_p, mem_v_p = kv_proj(mem2d, norm_mem[:, None, :], w_ca_k.astype(BF16), w_ca_v.astype(BF16))

    mem_k_p = mem_k_p.reshape(DEPTH, bp, MEM_LEN, d)
    mem_v_p = mem_v_p.reshape(DEPTH, bp, MEM_LEN, d)
    sample_states = (state_mlstm_c, state_mlstm_n,
                     jnp.pad(state_mlstm_m, ((0, 0), (0, 0), (0, LANES - ML_HEADS))), state_ssd_conv,
                     state_ssd.reshape(DEPTH, bs, SSD_WIDTH, SSD_STATE), state_ret)
    st_s = None

    xp = x_prompt
    xs = x_sample.reshape(bs, d)
    new_p = [[] for _ in range(6)]
    for l in range(DEPTH):
        lw = {"norm_mix": norm_mix[l][None, :], "w_in": _pack_in_proj(w_in[l]), "w_out": w_out[l].astype(BF16),
              "bias": _lane_row([ml_i_bias[l], ml_f_bias[l], ssd_dt_bias[l]]),
              "alog": _lane_row([zeros12, ssd_a_log[l]]),
              "conv_w": ssd_conv_w[l], "conv_b": ssd_conv_b[l][None, :], "ml_norm": ml_norm[l][None, :],
              "ssd_d": jnp.repeat(ssd_d[l], SSD_HEADDIM)[None, :], "ssd_norm": ssd_norm[l][None, :],
              "ret_norm": ret_norm[l][None, :]}
        wq, wo = w_ca_q[l].astype(BF16), w_ca_o[l].astype(BF16)
        g_ca, g_ffn = norm_ca[l][None, :], norm_ffn[l][None, :]
        last = l == DEPTH - 1
        j = l // 2
        if l % 2 == 0:
            ffn_w = (ffn_w_gate[j].astype(BF16), ffn_w_up[j].astype(BF16), ffn_w_down[j].astype(BF16))
        else:
            wrt = jnp.pad(moe_w_router[j].T, ((0, 2 * N_EXPERTS - N_EXPERTS), (0, 0)))
            ffn_w = (wrt, moe_w_gate[j].astype(BF16), moe_w_up[j].astype(BF16), moe_w_down[j].astype(BF16))

        def run_ffn(x2d, tm):
            if l % 2 == 0:
                return ffn_dense(x2d, g_ffn, *ffn_w, tm=tm)
            return moe_ffn(x2d, g_ffn, *ffn_w, norm_final[None, :], tm=tm, sub=_moe_sub_rows(tm), final_norm=last)

        xp, c1, n1, m1, conv1, ssd1, ret1 = mixer_prompt(xp, lw, consts, tm=256)
        st_p = (c1, n1, m1[:, 0, :ML_HEADS], conv1,
                ssd1.reshape(bp, SSD_HEADS, SSD_HEADDIM, SSD_STATE), ret1)
        xp = ca_prompt(xp, g_ca, wq, wo, mem_k_p, mem_v_p, l)
        xp = run_ffn(xp.reshape(bp * seq, d), 1024 if l % 2 else 512).reshape(bp, seq, d)

        proj = norm_matmul(xs, lw["norm_mix"], lw["w_in"], tn=D_PROJ // 3)
        y, st_s = mixer_sample(proj, lw, consts, sample_states, l, st_s)
        xs = norm_matmul(y, ones_row, lw["w_out"], residual=xs, norm=False)
        q = norm_matmul(xs, g_ca, wq)
        o = ca_sample(q.reshape(bs, CA_HEADS, CA_DH), cache_mem_k, cache_mem_v, l).reshape(bs, d)
        xs = norm_matmul(o, ones_row[:, :d], wo, residual=xs, norm=False)
        xs = run_ffn(xs, bs)

        for lst, a in zip(new_p, st_p):
            lst.append(a)

    shape5 = (DEPTH, bp, MEM_LEN, CA_HEADS, CA_DH)
    c_s, n_s, m_s, conv_s, ssd_s, ret_s = st_s
    return (xp, xs.reshape(bs, 1, d),
            jnp.stack(new_p[0]), jnp.stack(new_p[1]), jnp.stack(new_p[2]), jnp.stack(new_p[3]),
            jnp.stack(new_p[4]), jnp.stack(new_p[5]), mem_k_p.reshape(shape5), mem_v_p.reshape(shape5),
            c_s, n_s, m_s[:, :, :ML_HEADS], conv_s,
            ssd_s.reshape(DEPTH, bs, SSD_HEADS, SSD_HEADDIM, SSD_STATE), ret_s)
```

```python
import functools
import itertools
import math

import numpy as np
import jax
import jax.numpy as jnp
from jax import lax
from jax.experimental import pallas as pl
from jax.experimental.pallas import tpu as pltpu

F32 = jnp.float32
BF16 = jnp.bfloat16
HIGHEST = lax.Precision.HIGHEST

D_MODEL = 1024
DEPTH = 2
PAST_LEN = 16384
D_MIX = 2 * D_MODEL
ML_WIDTH = 3 * D_MIX // 8
ML_HEADS = 6
ML_DH = ML_WIDTH // ML_HEADS
SSD_WIDTH = 3 * D_MIX // 8
SSD_HEADDIM = 64
SSD_HEADS = SSD_WIDTH // SSD_HEADDIM
SSD_STATE = 128
SSD_GROUPS = 2
SSD_GROUP_WIDTH = SSD_WIDTH // SSD_GROUPS
SSD_CONV_DIM = SSD_WIDTH + 2 * SSD_GROUPS * SSD_STATE
CONV_W = 4
RET_WIDTH = D_MIX // 4
RET_HEADS = 4
RET_DH = RET_WIDTH // RET_HEADS
ROPE_BASE = 10000.0
CHUNK = 128
MEM_LEN = 256
CA_HEADS = 4
CA_DH = D_MODEL // CA_HEADS
D_FF = 11 * D_MODEL // 4
N_EXPERTS = 8
TOP_K = 2
D_EXPERT = D_FF // 2
EPS = 1e-6

LANES = 128
BF16_ROWS = 16
MXU_ROWS = 128

OFF_QML = 0
OFF_KML = OFF_QML + ML_WIDTH
OFF_VML = OFF_KML + ML_WIDTH
OFF_OML = OFF_VML + ML_WIDTH
OFF_Z = OFF_OML + ML_WIDTH
OFF_XBC = OFF_Z + SSD_WIDTH
OFF_QR = OFF_XBC + SSD_CONV_DIM
OFF_KR = OFF_QR + RET_WIDTH
OFF_VR = OFF_KR + RET_WIDTH
OFF_GR = OFF_VR + RET_WIDTH
OFF_SMALL = OFF_GR + RET_WIDTH
D_PROJ = OFF_SMALL + LANES
LANE_I = 0
LANE_F = ML_HEADS
LANE_DT = 2 * ML_HEADS

VMEM_LIMIT = 60000 * 1024


def _cparams(sem):
    return pltpu.CompilerParams(dimension_semantics=sem, vmem_limit_bytes=VMEM_LIMIT)


def _const_spec(shape):
    nd = len(shape)
    return pl.BlockSpec(shape, lambda *_: (0,) * nd, pipeline_mode=pl.Buffered(1))


def _rms(x, g):
    return x * lax.rsqrt(jnp.mean(x * x, axis=-1, keepdims=True) + EPS) * g


def _sigmoid(x):
    return 1.0 / (1.0 + jnp.exp(-x))


def _silu(x):
    return x * _sigmoid(x)


def _dot(a, b):
    return jnp.dot(a, b, preferred_element_type=F32)


def _dot_nt(a, b):
    return lax.dot_general(a, b, (((1,), (1,)), ((), ())), preferred_element_type=F32)


def _dot_exact(a, b):
    return jnp.dot(a, b, preferred_element_type=F32, precision=HIGHEST)


def _split3(v):
    v1 = v.astype(BF16)
    r1 = v - v1.astype(F32)
    v2 = r1.astype(BF16)
    v3 = (r1 - v2.astype(F32)).astype(BF16)
    return v1, v2, v3


def _norm_matmul_kernel(*refs, norm, residual):
    x_ref, g_ref, w_ref = refs[:3]
    o_ref = refs[-1]
    x = x_ref[...].astype(F32)
    if norm:
        x = _rms(x, g_ref[...])
    acc = _dot(x.astype(BF16), w_ref[...])
    if residual:
        acc = acc + refs[3][...]
    o_ref[...] = acc


def norm_matmul(x, g, w, residual=None, *, norm=True, tm=None, tn=None):
    m, k = x.shape
    n = w.shape[1]
    tm = tm or min(m, 512)
    tn = tn or min(n, 1024)
    in_specs = [pl.BlockSpec((tm, k), lambda j, i: (i, 0)),
                pl.BlockSpec((1, k), lambda j, i: (0, 0)),
                pl.BlockSpec((k, tn), lambda j, i: (0, j))]
    args = [x, g, w]
    if residual is not None:
        in_specs.append(pl.BlockSpec((tm, tn), lambda j, i: (i, j)))
        args.append(residual)
    return pl.pallas_call(
        functools.partial(_norm_matmul_kernel, norm=norm, residual=residual is not None),
        grid=(pl.cdiv(n, tn), m // tm),
        in_specs=in_specs,
        out_specs=pl.BlockSpec((tm, tn), lambda j, i: (i, j)),
        out_shape=jax.ShapeDtypeStruct((m, n), F32),
        compiler_params=_cparams(("parallel", "parallel")),
        name="norm_matmul",
    )(*args)


def _kv_proj_kernel(x_ref, g_ref, wk_ref, wv_ref, k_ref, v_ref):
    xn = _rms(x_ref[...], g_ref[...]).astype(BF16)
    k_ref[...] = _dot(xn, wk_ref[...])
    v_ref[...] = _dot(xn, wv_ref[...])


def kv_proj(mem2d, g, wk, wv, *, tm=512):
    m, d = mem2d.shape
    depth = wk.shape[0]
    out = jax.ShapeDtypeStruct((depth, m, d), F32)
    return pl.pallas_call(
        _kv_proj_kernel,
        grid=(depth, m // tm),
        in_specs=[pl.BlockSpec((tm, d), lambda l, i: (i, 0)),
                  pl.BlockSpec((None, 1, d), lambda l, i: (l, 0, 0)),
                  pl.BlockSpec((None, d, d), lambda l, i: (l, 0, 0)),
                  pl.BlockSpec((None, d, d), lambda l, i: (l, 0, 0))],
        out_specs=[pl.BlockSpec((None, tm, d), lambda l, i: (l, i, 0)),
                   pl.BlockSpec((None, tm, d), lambda l, i: (l, i, 0))],
        out_shape=[out, out],
        compiler_params=_cparams(("parallel", "parallel")),
        name="kv_proj",
    )(mem2d, g, wk, wv)


def _ca_prompt_kernel(x_ref, g_ref, wq_ref, wo_ref, k_ref, v_ref, o_ref):
    x = x_ref[...]
    xn = _rms(x, g_ref[...]).astype(BF16)
    q = _dot(xn, wq_ref[...])
    kb = k_ref[...].astype(BF16)
    vb = v_ref[...].astype(BF16)
    heads = [slice(h * CA_DH, (h + 1) * CA_DH) for h in range(CA_HEADS)]

    def scores(sl):
        return _dot_nt(q[:, sl].astype(BF16), kb[:, sl]) * (CA_DH ** -0.5)

    outs = []
    s_next = scores(heads[0])
    for h, sl in enumerate(heads):
        s = s_next
        if h + 1 < CA_HEADS:
            s_next = scores(heads[h + 1])
        p = jnp.exp(s - jnp.max(s, axis=-1, keepdims=True))
        p = p / jnp.sum(p, axis=-1, keepdims=True)
        outs.append(_dot(p.astype(BF16), vb[:, sl]).astype(BF16))
    o = jnp.concatenate(outs, axis=-1)
    o_ref[...] = x + _dot(o, wo_ref[...])


def ca_prompt(x, g, wq, wo, mem_k, mem_v, layer, *, tm=512):
    b, length, d = x.shape
    mlen = mem_k.shape[2]
    return pl.pallas_call(
        _ca_prompt_kernel,
        grid=(b, length // tm),
        in_specs=[pl.BlockSpec((None, tm, d), lambda i, j: (i, j, 0)),
                  _const_spec((1, d)), _const_spec((d, d)), _const_spec((d, d)),
                  pl.BlockSpec((None, None, mlen, d), lambda i, j: (layer, i, 0, 0)),
                  pl.BlockSpec((None, None, mlen, d), lambda i, j: (layer, i, 0, 0))],
        out_specs=pl.BlockSpec((None, tm, d), lambda i, j: (i, j, 0)),
        out_shape=jax.ShapeDtypeStruct(x.shape, F32),
        compiler_params=_cparams(("parallel", "parallel")),
        name="ca_prompt",
    )(x, g, wq, wo, mem_k, mem_v)


def _ca_sample_kernel(q_ref, k_ref, v_ref, o_ref, *, bt):
    for j in range(bt):
        kq = k_ref[j] * (q_ref[j] * (CA_DH ** -0.5))
        s = jnp.sum(kq, axis=-1, keepdims=True)
        p = jnp.exp(s - jnp.max(s, axis=0, keepdims=True))
        o = jnp.sum(p * v_ref[j], axis=0)
        o_ref[j] = o / jnp.sum(p, axis=0)


def ca_sample(q, mem_k, mem_v, layer, *, bt=8):
    b = q.shape[0]
    blk = (None, bt) + mem_k.shape[2:]
    return pl.pallas_call(
        functools.partial(_ca_sample_kernel, bt=bt),
        grid=(b // bt,),
        in_specs=[pl.BlockSpec((bt,) + q.shape[1:], lambda i: (i, 0, 0)),
                  pl.BlockSpec(blk, lambda i: (layer, i, 0, 0, 0)),
                  pl.BlockSpec(blk, lambda i: (layer, i, 0, 0, 0))],
        out_specs=pl.BlockSpec((bt,) + q.shape[1:], lambda i: (i, 0, 0)),
        out_shape=jax.ShapeDtypeStruct(q.shape, F32),
        compiler_params=_cparams(("parallel",)),
        name="ca_sample",
    )(q, mem_k, mem_v)


def _ffn_kernel(x_ref, g_ref, wg_ref, wu_ref, wd_ref, o_ref, a_ref, *, fchunk):
    x = x_ref[...]
    xn = _rms(x, g_ref[...]).astype(BF16)
    dff = wg_ref.shape[1]
    for f0 in range(0, dff, fchunk):
        gate = _dot(xn, wg_ref[:, f0:f0 + fchunk])
        up = _dot(xn, wu_ref[:, f0:f0 + fchunk])
        a_ref[:, f0:f0 + fchunk] = (_silu(gate) * up).astype(BF16)
    o_ref[...] = x + _dot(a_ref[...], wd_ref[...])


def ffn_dense(x, g, wg, wu, wd, *, tm=512):
    m, d = x.shape
    dff = wg.shape[1]
    return pl.pallas_call(
        functools.partial(_ffn_kernel, fchunk=dff // 2),
        grid=(m // tm,),
        in_specs=[pl.BlockSpec((tm, d), lambda i: (i, 0)),
                  _const_spec((1, d)), _const_spec((d, dff)), _const_spec((d, dff)), _const_spec((dff, d))],
        out_specs=pl.BlockSpec((tm, d), lambda i: (i, 0)),
        out_shape=jax.ShapeDtypeStruct((m, d), F32),
        scratch_shapes=[pltpu.VMEM((tm, dff), BF16)],
        compiler_params=_cparams(("parallel",)),
        name="ffn_dense",
    )(x, g, wg, wu, wd)


def _moe_kernel(x_ref, g_ref, wrt_ref, wg_ref, wu_ref, wd_ref, gf_ref, o_ref,
                xn_ref, gate_t_ref, pos_t_ref, *, tm, sub, final_norm):
    e = pl.program_id(1)
    ne = pl.num_programs(1)
    epad = gate_t_ref.shape[0]

    @pl.when(e == 0)
    def _route():
        x = x_ref[...]
        xn = _rms(x, g_ref[...])
        xn_ref[...] = xn.astype(BF16)
        x_hi = xn.astype(BF16)
        x_lo = (xn - x_hi.astype(F32)).astype(BF16)
        w = wrt_ref[...]
        w_hi = w.astype(BF16)
        w_lo = (w - w_hi.astype(F32)).astype(BF16)
        logits = _dot_nt(w_hi, x_hi) + (_dot_nt(w_hi, x_lo) + _dot_nt(w_lo, x_hi))
        row = lax.broadcasted_iota(jnp.int32, (epad, tm), 0)
        logits = jnp.where(row < N_EXPERTS, logits, -jnp.inf)
        m1 = jnp.max(logits, axis=0, keepdims=True)
        i1 = jnp.min(jnp.where(logits == m1, row, epad), axis=0, keepdims=True)
        rest = jnp.where(row == i1, -jnp.inf, logits)
        m2 = jnp.max(rest, axis=0, keepdims=True)
        i2 = jnp.min(jnp.where(rest == m2, row, epad), axis=0, keepdims=True)
        e2 = jnp.exp(m2 - m1)
        den = 1.0 + e2
        gate_t = jnp.where(row == i1, 1.0 / den, 0.0) + jnp.where(row == i2, e2 / den, 0.0)
        sel_t = jnp.where((row == i1) | (row == i2), 1.0, 0.0)
        r_i = lax.broadcasted_iota(jnp.int32, (tm, tm), 0)
        c_i = lax.broadcasted_iota(jnp.int32, (tm, tm), 1)
        upper = jnp.where(r_i < c_i, 1.0, 0.0).astype(BF16)
        pos_t = _dot(sel_t.astype(BF16), upper)
        pos_t = jnp.where(sel_t > 0, pos_t, -1.0)
        gate_t_ref[...] = gate_t
        pos_t_ref[...] = pos_t
        o_ref[...] = x

    pos_row = pos_t_ref[pl.ds(e, 1), :]
    gate_row = gate_t_ref[pl.ds(e, 1), :]
    count = jnp.sum(jnp.where(pos_row >= 0, 1.0, 0.0)).astype(jnp.int32)
    half = sub // 2
    wide = sub + sub // 4
    use_wide = (count > sub) & (count <= wide) if wide < tm else False
    n_full = jnp.where(use_wide, 0, (count + (half - 1)) // sub)
    done = jnp.where(use_wide, count, n_full * sub)

    def expert_rows(base, rows):
        slot = lax.broadcasted_iota(jnp.int32, (rows, tm), 0).astype(F32) + base.astype(F32)
        hit = pos_row == slot
        gather = jnp.where(hit, 1.0, 0.0).astype(BF16)
        gate = jnp.sum(jnp.where(hit, gate_row, 0.0), axis=-1, keepdims=True)
        xg = _dot(gather, xn_ref[...]).astype(BF16)
        act = (_silu(_dot(xg, wg_ref[...])) * _dot(xg, wu_ref[...])).astype(BF16)
        ye = (gate * _dot(act, wd_ref[...])).astype(BF16)
        o_ref[...] += lax.dot_general(gather, ye, (((0,), (0,)), ((), ())), preferred_element_type=F32)

    def full_body(j, carry):
        expert_rows(j * sub, sub)
        return carry

    lax.fori_loop(0, n_full, full_body, 0)

    @pl.when(count > done)
    def _remainder():
        expert_rows(done, half)

    if wide < tm:
        @pl.when(use_wide)
        def _wide():
            expert_rows(jnp.int32(0), wide)

    if final_norm:
        @pl.when(e == ne - 1)
        def _final():
            o_ref[...] = _rms(o_ref[...], gf_ref[...])


def _moe_sub_rows(tm):
    mean = tm * TOP_K // N_EXPERTS
    return min(tm, max(MXU_ROWS, -(-mean // MXU_ROWS) * MXU_ROWS))


def moe_ffn(x, g, w_router_t, wg, wu, wd, g_final, *, tm, sub, final_norm):
    m, d = x.shape
    ne, _, de = wg.shape
    epad = w_router_t.shape[0]
    return pl.pallas_call(
        functools.partial(_moe_kernel, tm=tm, sub=sub, final_norm=final_norm),
        grid=(m // tm, ne),
        in_specs=[pl.BlockSpec((tm, d), lambda i, e: (i, 0)),
                  pl.BlockSpec((1, d), lambda i, e: (0, 0)),
                  pl.BlockSpec((epad, d), lambda i, e: (0, 0)),
                  pl.BlockSpec((None, d, de), lambda i, e: (e, 0, 0)),
                  pl.BlockSpec((None, d, de), lambda i, e: (e, 0, 0)),
                  pl.BlockSpec((None, de, d), lambda i, e: (e, 0, 0)),
                  pl.BlockSpec((1, d), lambda i, e: (0, 0))],
        out_specs=pl.BlockSpec((tm, d), lambda i, e: (i, 0)),
        out_shape=jax.ShapeDtypeStruct((m, d), F32),
        scratch_shapes=[pltpu.VMEM((tm, d), BF16),
                        pltpu.VMEM((epad, tm), F32), pltpu.VMEM((epad, tm), F32)],
        compiler_params=_cparams(("parallel", "arbitrary")),
        name="moe_ffn",
    )(x, g, w_router_t, wg, wu, wd, g_final)


def _retention_constants(c):
    log_g = np.log1p(-np.exp2(-5.0 - np.arange(RET_HEADS, dtype=np.float32))).astype(np.float32)
    j = np.arange(c, dtype=np.float32)
    rel = j[:, None] - j[None, :]
    w_intra = np.where(rel >= 0, np.exp(rel[None] * log_g[:, None, None]), 0.0).astype(np.float32)
    w_inter = np.exp((j[None, :] + 1.0) * log_g[:, None]).astype(np.float32)
    w_end = np.exp((c - 1.0 - j[None, :]) * log_g[:, None]).astype(np.float32)
    decay = np.exp(c * log_g).astype(np.float32)
    cols = np.zeros((c, LANES), np.float32)
    cols[:, :RET_HEADS] = w_inter.T
    cols[:, RET_HEADS:2 * RET_HEADS] = w_end.T
    return w_intra, cols, [float(d) for d in decay]


def _rotary_tables(pos):
    half = RET_DH // 2
    inv = ROPE_BASE ** (-jnp.arange(half, dtype=F32) / half)
    ang = pos[:, None] * inv[None, :]
    cos, sin = jnp.cos(ang), jnp.sin(ang)
    return jnp.concatenate([cos, cos], axis=-1), jnp.concatenate([-sin, sin], axis=-1)


def _head_expand_matrix():
    e = np.zeros((LANES, SSD_WIDTH), np.float32)
    for h in range(SSD_HEADS):
        e[LANE_DT + h, h * SSD_HEADDIM:(h + 1) * SSD_HEADDIM] = 1.0
    return e


def _softplus_parts(v):
    sp = jnp.log1p(jnp.exp(-jnp.abs(v)))
    return jnp.minimum(v, 0.0) - sp, jnp.maximum(v, 0.0) + sp


def _mixer_prompt_kernel(x_ref, xnext_ref, g_ref, win_ref, wout_ref, bias_ref, alog_ref, convw_ref, convb_ref,
                         mlg_ref, ssdd_ref, ssdg_ref, retg_ref, cos_ref, sin_ref, retw_ref, retcol_ref,
                         expand_ref,
                         xo_ref, c_ref, n_ref, m_ref, conv_ref, ssd_ref, ret_ref,
                         proj_s, cs_s, cst_s, smt_s, dtx_s, acx_s, y_s, u_s, *, tm, tiles_per_seq, ret_decay):
    c = CHUNK
    i = pl.program_id(0)

    @pl.when(i % tiles_per_seq == 0)
    def _init():
        c_ref[...] = jnp.zeros_like(c_ref)
        n_ref[...] = jnp.zeros_like(n_ref)
        m_ref[...] = jnp.zeros_like(m_ref)
        ssd_ref[...] = jnp.zeros_like(ssd_ref)
        ret_ref[...] = jnp.zeros_like(ret_ref)
        u_s[0:8, :] = jnp.zeros((8, SSD_CONV_DIM), F32)

    seg_w = 512
    r_i = lax.broadcasted_iota(jnp.int32, (c, c), 0)
    c_i = lax.broadcasted_iota(jnp.int32, (c, c), 1)
    causal = r_i >= c_i
    tril = jnp.where(causal, 1.0, 0.0).astype(BF16)
    lane1 = lax.broadcasted_iota(jnp.int32, (1, LANES), 1)
    lane_c = lax.broadcasted_iota(jnp.int32, (c, LANES), 1)
    a_vec = -jnp.exp(alog_ref[...])
    expand3 = expand_ref[...]
    bias = bias_ref[...]

    def cumsum_rows(v):
        p = _dot(tril, jnp.concatenate(_split3(v), axis=-1))
        return p[:, :LANES] + p[:, LANES:2 * LANES] + p[:, 2 * LANES:]

    def expand_heads(v):
        return _dot(jnp.concatenate(_split3(v), axis=-1), expand3)

    seg_offs = list(range(0, D_PROJ, seg_w))
    assert seg_offs[-1] == OFF_SMALL

    def prepare(src_ref, slot):
        xn = _rms(src_ref[...], g_ref[...]).astype(BF16)
        for off in seg_offs[-1:] + seg_offs[:-1]:
            wdt = min(seg_w, D_PROJ - off)
            proj_s[slot, :, off:off + wdt] = _dot(xn, win_ref[:, off:off + wdt])
            yield
            if off != OFF_SMALL:
                continue
            is_f = (lane_c >= LANE_F) & (lane_c < LANE_DT)
            is_dt = (lane_c >= LANE_DT) & (lane_c < LANE_DT + SSD_HEADS)
            for ci in range(tm // c):
                rows = slice(ci * c, (ci + 1) * c)
                small = proj_s[slot, rows, OFF_SMALL:OFF_SMALL + LANES] + bias
                logf, dt = _softplus_parts(small)
                cs = cumsum_rows(jnp.where(is_f, logf, jnp.where(is_dt, dt * a_vec, 0.0)))
                cs_s[slot, rows, :] = cs
                cst_s[slot, :, rows] = cs.T
                smt_s[slot, :, rows] = small.T
                dtx_s[slot, rows, :] = expand_heads(jnp.where(is_dt, dt, 0.0))
                acx_s[slot, rows, :] = expand_heads(jnp.where(is_dt, cs, 0.0))
                yield

    @pl.when(i == 0)
    def _first():
        for _ in prepare(x_ref, 0):
            pass

    cur = i % 2
    proj_cur = proj_s.at[cur]
    x = x_ref[...]

    def chunk_body(ci):
        rows = slice(ci * c, (ci + 1) * c)
        small = proj_cur[rows, OFF_SMALL:OFF_SMALL + LANES] + bias
        cs = cs_s[cur, rows, :]
        cs_t = cst_s[cur, :, rows]
        small_t = smt_s[cur, :, rows]

        m_vec = m_ref[...]
        m_out = m_vec

        def ml_front(h):
            q = proj_cur[rows, OFF_QML + h * ML_DH:OFF_QML + (h + 1) * ML_DH]
            k = proj_cur[rows, OFF_KML + h * ML_DH:OFF_KML + (h + 1) * ML_DH] * (ML_DH ** -0.5)
            v = proj_cur[rows, OFF_VML + h * ML_DH:OFF_VML + (h + 1) * ML_DH]
            qb, kb, vb = q.astype(BF16), k.astype(BF16), v.astype(BF16)
            cm = c_ref[h]
            return q, k, v, kb, vb, cm, _dot_nt(qb, kb), _dot_nt(qb, cm.astype(BF16))

        def ml_finish(hs, num_intra, num_inter, den):
            hh = (num_intra + num_inter) / den
            hn = hh * lax.rsqrt(jnp.mean(hh * hh, axis=-1, keepdims=True) + EPS)
            og = proj_cur[rows, OFF_OML + hs.start:OFF_OML + hs.stop]
            y_s[rows, hs] = (hn * mlg_ref[:, hs] * _sigmoid(og)).astype(BF16)

        front = ml_front(0)
        pending = None
        for h in range(ML_HEADS):
            q, k, v, kb, vb, cm, qk, cq = front
            if h + 1 < ML_HEADS:
                front = ml_front(h + 1)
            hs = slice(h * ML_DH, (h + 1) * ML_DH)
            b_col = cs[:, LANE_F + h:LANE_F + h + 1]
            b_row = cs_t[LANE_F + h:LANE_F + h + 1, :]
            i_col = small[:, LANE_I + h:LANE_I + h + 1]
            i_row = small_t[LANE_I + h:LANE_I + h + 1, :]
            m_prev = m_vec[:, h:h + 1]
            lw = jnp.where(causal, b_col + (i_row - b_row), -jnp.inf)
            log_a = b_col + m_prev
            m_t = jnp.maximum(log_a, jnp.max(lw, axis=-1, keepdims=True))
            w_intra = jnp.exp(lw - m_t)
            w_inter = jnp.exp(log_a - m_t)
            s = w_intra * qk
            n_row = n_ref[h:h + 1, :]
            num_intra = _dot(s.astype(BF16), vb)
            den = jnp.sum(s, axis=-1, keepdims=True) + w_inter * jnp.sum(q * n_row, axis=-1, keepdims=True)
            den = jnp.maximum(jnp.abs(den), jnp.exp(-m_t))
            m_new = m_t[c - 1:c, :]
            decay = jnp.exp(log_a[c - 1:c, :] - m_new)
            w_end = jnp.exp(b_col[c - 1:c, :] - b_col + i_col - m_new)
            c_ref[h] = decay * cm + _dot((v * w_end).T.astype(BF16), kb)
            n_ref[h:h + 1, :] = decay * n_row + jnp.sum(k * w_end, axis=0, keepdims=True)
            m_out = jnp.where(lane1 == h, m_new, m_out)
            if pending is not None:
                ml_finish(*pending)
            pending = (hs, num_intra, w_inter * cq, den)
            yield
        ml_finish(*pending)
        m_ref[...] = m_out

        u = proj_cur[rows, OFF_XBC:OFF_XBC + SSD_CONV_DIM]
        u_s[8:8 + c, :] = u
        conv = convb_ref[...] + convw_ref[3:4, :] * u
        for jj in range(CONV_W - 1):
            conv = conv + convw_ref[jj:jj + 1, :] * u_s[5 + jj:5 + jj + c, :]
        u_s[0:8, :] = u_s[c:c + 8, :]
        xc = _silu(conv)
        xs = xc[:, :SSD_WIDTH]
        dt_exp = dtx_s[cur, rows, :]
        acum_exp = acx_s[cur, rows, :]
        xdt = xs * dt_exp
        w_end_exp = jnp.exp(acum_exp[c - 1:c, :] - acum_exp)
        xw = xdt * w_end_exp
        ea_exp = jnp.exp(acum_exp)
        lane_lo = lane_c < SSD_HEADDIM
        hpg = SSD_HEADS // SSD_GROUPS
        y_cols = []
        yield
        for g in range(SSD_GROUPS):
            gw = slice(g * SSD_GROUP_WIDTH, (g + 1) * SSD_GROUP_WIDTH)
            bg = xc[:, SSD_WIDTH + g * SSD_STATE:SSD_WIDTH + (g + 1) * SSD_STATE].astype(BF16)
            cg = xc[:, SSD_WIDTH + (SSD_GROUPS + g) * SSD_STATE:
                    SSD_WIDTH + (SSD_GROUPS + g + 1) * SSD_STATE].astype(BF16)
            cb = _dot_nt(cg, bg)
            st = ssd_ref[gw, :]
            inter = _dot_nt(cg, st.astype(BF16)) * ea_exp[:, gw]
            def decayed_scores(pr):
                out = []
                for hidx in (g * hpg + 2 * pr, g * hpg + 2 * pr + 1):
                    a_col = cs[:, LANE_DT + hidx:LANE_DT + hidx + 1]
                    a_row = cs_t[LANE_DT + hidx:LANE_DT + hidx + 1, :]
                    dec = jnp.exp(jnp.where(causal, a_col - a_row, -jnp.inf))
                    out.append((dec * cb).astype(BF16))
                return out

            att = decayed_scores(0)
            for pr in range(hpg // 2):
                h0 = g * hpg + 2 * pr
                xp = xdt[:, h0 * SSD_HEADDIM:(h0 + 2) * SSD_HEADDIM].astype(BF16)
                pair = [_dot(att[0], xp), _dot(att[1], xp)]
                if pr + 1 < hpg // 2:
                    att = decayed_scores(pr + 1)
                y_cols.append(jnp.where(lane_lo, pair[0], pair[1]) + inter[:, pr * LANES:(pr + 1) * LANES])
                yield
            upd = _dot(xw[:, gw].T.astype(BF16), bg)
            for r in range(hpg):
                hidx = g * hpg + r
                scal = jnp.exp(cs[c - 1:c, LANE_DT + hidx:LANE_DT + hidx + 1])
                hr = slice(hidx * SSD_HEADDIM, (hidx + 1) * SSD_HEADDIM)
                ssd_ref[hr, :] = scal * ssd_ref[hr, :] + upd[r * SSD_HEADDIM:(r + 1) * SSD_HEADDIM, :]
            yield
        ys = jnp.concatenate(y_cols, axis=-1) + ssdd_ref[...] * xs
        yz = ys * _silu(proj_cur[rows, OFF_Z:OFF_Z + SSD_WIDTH])
        for g in range(SSD_GROUPS):
            gw = slice(g * SSD_GROUP_WIDTH, (g + 1) * SSD_GROUP_WIDTH)
            seg = yz[:, gw]
            seg = seg * lax.rsqrt(jnp.mean(seg * seg, axis=-1, keepdims=True) + EPS) * ssdg_ref[:, gw]
            y_s[rows, ML_WIDTH + g * SSD_GROUP_WIDTH:ML_WIDTH + (g + 1) * SSD_GROUP_WIDTH] = seg.astype(BF16)
        yield

        cosb = cos_ref[rows, :]
        sinb = sin_ref[rows, :]
        retcol = retcol_ref[...]
        def ret_front(h):
            q = proj_cur[rows, OFF_QR + h * RET_DH:OFF_QR + (h + 1) * RET_DH]
            k = proj_cur[rows, OFF_KR + h * RET_DH:OFF_KR + (h + 1) * RET_DH]
            v = proj_cur[rows, OFF_VR + h * RET_DH:OFF_VR + (h + 1) * RET_DH]
            qr = q * cosb + pltpu.roll(q, RET_DH // 2, 1) * sinb
            kr = (k * cosb + pltpu.roll(k, RET_DH // 2, 1) * sinb) * (RET_DH ** -0.5)
            qb, kb, vb = qr.astype(BF16), kr.astype(BF16), v.astype(BF16)
            sm = ret_ref[h]
            return kr, vb, sm, _dot_nt(qb, kb), _dot(qb, sm.astype(BF16))

        front = ret_front(0)
        for h in range(RET_HEADS):
            kr, vb, sm, qk, qs = front
            if h + 1 < RET_HEADS:
                front = ret_front(h + 1)
            hs = slice(h * RET_DH, (h + 1) * RET_DH)
            gr = proj_cur[rows, OFF_GR + h * RET_DH:OFF_GR + (h + 1) * RET_DH]
            att = qk * retw_ref[h]
            o = _dot(att.astype(BF16), vb) + qs * retcol[:, h:h + 1]
            kw = kr * retcol[:, RET_HEADS + h:RET_HEADS + h + 1]
            ret_ref[h] = ret_decay[h] * sm + _dot(kw.T.astype(BF16), vb)
            on = o * lax.rsqrt(jnp.mean(o * o, axis=-1, keepdims=True) + EPS)
            y_s[rows, ML_WIDTH + SSD_WIDTH + h * RET_DH:ML_WIDTH + SSD_WIDTH + (h + 1) * RET_DH] = (
                on * retg_ref[:, hs] * _silu(gr)).astype(BF16)
            yield

    stages_per_chunk = ML_HEADS + 1 + SSD_GROUPS * (SSD_HEADS // SSD_GROUPS // 2 + 1) + 1 + RET_HEADS
    n_segments = len(seg_offs) + tm // c
    every = max(1, (tm // c) * stages_per_chunk // n_segments)
    segments = prepare(xnext_ref, (i + 1) % 2)
    for k, _ in enumerate(itertools.chain.from_iterable(chunk_body(ci) for ci in range(tm // c))):
        if k % every == 0:
            next(segments, None)
    for _ in segments:
        pass
    xo_ref[...] = x + _dot(y_s[...], wout_ref[...])
    conv_ref[...] = u_s[5:8, :]


def mixer_prompt(x, lw, consts, *, tm):
    b, length, d = x.shape
    c = CHUNK
    tps = length // tm
    n_tiles = b * tps
    x2d = x.reshape(b * length, d)
    row = lambda w: _const_spec((1, w))
    state_spec = lambda *shape: pl.BlockSpec((None,) + shape, lambda i: (i // tps,) + (0,) * len(shape))
    out_shapes = [jax.ShapeDtypeStruct(x2d.shape, F32),
                  jax.ShapeDtypeStruct((b, ML_HEADS, ML_DH, ML_DH), F32),
                  jax.ShapeDtypeStruct((b, ML_HEADS, ML_DH), F32),
                  jax.ShapeDtypeStruct((b, 1, LANES), F32),
                  jax.ShapeDtypeStruct((b, CONV_W - 1, SSD_CONV_DIM), F32),
                  jax.ShapeDtypeStruct((b, SSD_WIDTH, SSD_STATE), F32),
                  jax.ShapeDtypeStruct((b, RET_HEADS, RET_DH, RET_DH), F32)]
    outs = pl.pallas_call(
        functools.partial(_mixer_prompt_kernel, tm=tm, tiles_per_seq=tps, ret_decay=consts["ret_decay"]),
        grid=(n_tiles,),
        in_specs=[pl.BlockSpec((tm, d), lambda i: (i, 0)),
                  pl.BlockSpec((tm, d), lambda i: (jnp.minimum(i + 1, n_tiles - 1), 0)),
                  row(d), _const_spec((d, D_PROJ)), _const_spec((D_MIX, d)),
                  row(LANES), row(LANES), _const_spec((CONV_W, SSD_CONV_DIM)), row(SSD_CONV_DIM),
                  row(ML_WIDTH), row(SSD_WIDTH), row(SSD_WIDTH), row(RET_WIDTH),
                  pl.BlockSpec((tm, LANES), lambda i: (i % tps, 0)),
                  pl.BlockSpec((tm, LANES), lambda i: (i % tps, 0)),
                  _const_spec((RET_HEADS, c, c)), _const_spec((c, LANES)),
                  _const_spec((3 * LANES, SSD_WIDTH))],
        out_specs=[pl.BlockSpec((tm, d), lambda i: (i, 0)),
                   state_spec(ML_HEADS, ML_DH, ML_DH), state_spec(ML_HEADS, ML_DH), state_spec(1, LANES),
                   state_spec(CONV_W - 1, SSD_CONV_DIM), state_spec(SSD_WIDTH, SSD_STATE),
                   state_spec(RET_HEADS, RET_DH, RET_DH)],
        out_shape=out_shapes,
        scratch_shapes=[pltpu.VMEM((2, tm, D_PROJ), F32),
                        pltpu.VMEM((2, tm, LANES), F32), pltpu.VMEM((2, LANES, tm), F32),
                        pltpu.VMEM((2, LANES, tm), F32), pltpu.VMEM((2, tm, SSD_WIDTH), F32),
                        pltpu.VMEM((2, tm, SSD_WIDTH), F32), pltpu.VMEM((tm, D_MIX), BF16),
                        pltpu.VMEM((c + 8, SSD_CONV_DIM), F32)],
        compiler_params=_cparams(("arbitrary",)),
        name="mixer_prompt",
    )(x2d, x2d, lw["norm_mix"], lw["w_in"], lw["w_out"], lw["bias"], lw["alog"], lw["conv_w"], lw["conv_b"],
      lw["ml_norm"], lw["ssd_d"], lw["ssd_norm"], lw["ret_norm"], consts["cos_p"], consts["sin_p"],
      consts["ret_w"], consts["ret_cols"], consts["expand3"])
    return [outs[0].reshape(b, length, d)] + list(outs[1:])


N_SAMPLE_STATES = 6


def _mixer_sample_kernel(*refs, bt, ret_gamma, n_alias):
    (proj_ref, small_ref, bias_ref, alog_ref, convw_ref, convb_ref, mlg_ref, ssdd_ref,
     ssdg_ref, retg_ref, cos_ref, sin_ref, expand_ref,
     c_in, n_in, m_in, conv_in, ssd_in, ret_in) = refs[:19]
    (y_ref, c_out, n_out, m_out, conv_out, ssd_out, ret_out,
     wi_s, wt_s, en_s, dtx_s, eax_s) = refs[19 + n_alias:]
    small = small_ref[...] + bias_ref[...]
    logf, dt = _softplus_parts(small)
    lane_b = lax.broadcasted_iota(jnp.int32, (bt, LANES), 1)
    is_dt = (lane_b >= LANE_DT) & (lane_b < LANE_DT + SSD_HEADS)
    log_a = pltpu.roll(logf, LANES - LANE_F, 1) + m_in[...]
    m_t = jnp.maximum(log_a, small)
    m_out[...] = m_t
    wi_s[...] = jnp.exp(small - m_t)
    wt_s[...] = jnp.exp(log_a - m_t)
    en_s[...] = jnp.exp(-m_t)
    a_vec = -jnp.exp(alog_ref[...])
    expand = expand_ref[...]
    dtx_s[...] = _dot_exact(jnp.where(is_dt, dt, 0.0), expand)
    eax_s[...] = jnp.exp(_dot_exact(jnp.where(is_dt, dt * a_vec, 0.0), expand))

    sub16 = lax.broadcasted_iota(jnp.int32, (BF16_ROWS, LANES), 0)
    row_hi = lax.broadcasted_iota(jnp.int32, (LANES, LANES), 0) < SSD_HEADDIM

    def rows3(a, b, c):
        return jnp.where(sub16 == 0, a, jnp.where(sub16 == 1, b, jnp.where(sub16 == 2, c, 0.0))).astype(BF16)

    def split(v):
        hi = v.astype(BF16).astype(F32)
        return hi, v - hi

    def outer(a, b):
        a_hi, a_lo = split(a)
        b_hi, b_lo = split(b)
        return lax.dot_general(rows3(a_hi, a_hi, a_lo), rows3(b_hi, b_lo, b_hi), (((0,), (0,)), ((), ())),
                               preferred_element_type=F32)

    def row1(a):
        return jnp.where(sub16 == 0, a, 0.0).astype(BF16)

    cosb = cos_ref[...]
    sinb = sin_ref[...]
    hpg = SSD_HEADS // SSD_GROUPS

    sub_l = lax.broadcasted_iota(jnp.int32, (bt, LANES), 0)
    sub_w = lax.broadcasted_iota(jnp.int32, (bt, SSD_WIDTH), 0)

    def body(j):
        pick_l = lambda ref: jnp.sum(jnp.where(sub_l == j, ref[...], 0.0), axis=0, keepdims=True)
        pick_w = lambda ref: jnp.sum(jnp.where(sub_w == j, ref[...], 0.0), axis=0, keepdims=True)
        wi = pick_l(wi_s)
        wt = pick_l(wt_s)
        en = pick_l(en_s)
        def ml_front(h):
            q = proj_ref[j, :,OFF_QML + h * ML_DH:OFF_QML + (h + 1) * ML_DH]
            k = proj_ref[j, :,OFF_KML + h * ML_DH:OFF_KML + (h + 1) * ML_DH] * (ML_DH ** -0.5)
            v = proj_ref[j, :,OFF_VML + h * ML_DH:OFF_VML + (h + 1) * ML_DH]
            return q, k, outer(v, k)

        front = ml_front(0)
        for h in range(ML_HEADS):
            q, k, vk = front
            if h + 1 < ML_HEADS:
                front = ml_front(h + 1)
            hs = slice(h * ML_DH, (h + 1) * ML_DH)
            og = proj_ref[j, :,OFF_OML + h * ML_DH:OFF_OML + (h + 1) * ML_DH]
            w_in_h = wi[:, h:h + 1]
            w_tr_h = wt[:, h:h + 1]
            c_new = w_tr_h * c_in[j, h] + w_in_h * vk
            n_new = w_tr_h * n_in[j, h:h + 1, :] + w_in_h * k
            c_out[j, h] = c_new
            n_out[j, h:h + 1, :] = n_new
            num = _dot_nt(row1(q), c_new.astype(BF16))[0:1, :]
            den = jnp.sum(n_new * q, axis=-1, keepdims=True)
            den = jnp.maximum(jnp.abs(den), en[:, h:h + 1])
            hh = num / den
            hn = hh * lax.rsqrt(jnp.mean(hh * hh, axis=-1, keepdims=True) + EPS)
            y_ref[j, :,hs] = hn * mlg_ref[:, hs] * _sigmoid(og)
            yield
        u = proj_ref[j, :,OFF_XBC:OFF_XBC + SSD_CONV_DIM]
        prev = conv_in[j]
        conv = convb_ref[...] + convw_ref[CONV_W - 1:CONV_W, :] * u
        for jj in range(CONV_W - 1):
            conv = conv + convw_ref[jj:jj + 1, :] * prev[jj:jj + 1, :]
        conv_out[j, 0:CONV_W - 2, :] = prev[1:CONV_W - 1, :]
        conv_out[j, CONV_W - 2:CONV_W - 1, :] = u
        xc = _silu(conv)
        xs = xc[:, :SSD_WIDTH]
        xdt = xs * pick_w(dtx_s)
        ea = pick_w(eax_s)
        y_cols = []
        n_pairs = SSD_HEADS // 2

        def ssd_front(p):
            g = p // (hpg // 2)
            bg = xc[:, SSD_WIDTH + g * SSD_STATE:SSD_WIDTH + (g + 1) * SSD_STATE]
            pw = slice(2 * p * SSD_HEADDIM, (2 * p + 2) * SSD_HEADDIM)
            return pw, outer(xdt[:, pw], bg)

        front = ssd_front(0)
        for p in range(n_pairs):
            pw, xb = front
            if p + 1 < n_pairs:
                front = ssd_front(p + 1)
            g = p // (hpg // 2)
            cg = xc[:, SSD_WIDTH + (SSD_GROUPS + g) * SSD_STATE:SSD_WIDTH + (SSD_GROUPS + g + 1) * SSD_STATE]
            lo = pw.start
            decay = jnp.where(row_hi, ea[:, lo:lo + 1], ea[:, lo + SSD_HEADDIM:lo + SSD_HEADDIM + 1])
            st_new = decay * ssd_in[j, pw, :] + xb
            ssd_out[j, pw, :] = st_new
            y_cols.append(_dot_nt(row1(cg), st_new.astype(BF16))[0:1, :])
            yield
        ys = jnp.concatenate(y_cols, axis=-1) + ssdd_ref[...] * xs
        yz = ys * _silu(proj_ref[j, :,OFF_Z:OFF_Z + SSD_WIDTH])
        for g in range(SSD_GROUPS):
            gw = slice(g * SSD_GROUP_WIDTH, (g + 1) * SSD_GROUP_WIDTH)
            seg = yz[:, gw]
            y_ref[j, :,ML_WIDTH + g * SSD_GROUP_WIDTH:ML_WIDTH + (g + 1) * SSD_GROUP_WIDTH] = (
                seg * lax.rsqrt(jnp.mean(seg * seg, axis=-1, keepdims=True) + EPS) * ssdg_ref[:, gw])
        def ret_front(h):
            q = proj_ref[j, :,OFF_QR + h * RET_DH:OFF_QR + (h + 1) * RET_DH]
            k = proj_ref[j, :,OFF_KR + h * RET_DH:OFF_KR + (h + 1) * RET_DH]
            v = proj_ref[j, :,OFF_VR + h * RET_DH:OFF_VR + (h + 1) * RET_DH]
            qr = q * cosb + pltpu.roll(q, RET_DH // 2, 1) * sinb
            kr = (k * cosb + pltpu.roll(k, RET_DH // 2, 1) * sinb) * (RET_DH ** -0.5)
            return qr, outer(kr, v)

        front = ret_front(0)
        for h in range(RET_HEADS):
            qr, kv = front
            if h + 1 < RET_HEADS:
                front = ret_front(h + 1)
            hs = slice(h * RET_DH, (h + 1) * RET_DH)
            gr = proj_ref[j, :,OFF_GR + h * RET_DH:OFF_GR + (h + 1) * RET_DH]
            s_new = ret_gamma[h] * ret_in[j, h] + kv
            ret_out[j, h] = s_new
            o = _dot(row1(qr), s_new.astype(BF16))[0:1, :]
            on = o * lax.rsqrt(jnp.mean(o * o, axis=-1, keepdims=True) + EPS)
            y_ref[j, :,ML_WIDTH + SSD_WIDTH + h * RET_DH:ML_WIDTH + SSD_WIDTH + (h + 1) * RET_DH] = (
                on * retg_ref[:, hs] * _silu(gr))
            yield

    def body_pair(jj, carry):
        for _ in itertools.zip_longest(body(2 * jj), body(2 * jj + 1)):
            pass
        return carry

    lax.fori_loop(0, bt // 2, body_pair, 0)


def mixer_sample(proj, lw, consts, states, layer, prev_out=None, *, bt=8):
    b = proj.shape[0]
    row = lambda w: pl.BlockSpec((1, w), lambda i: (0, 0))
    blk = lambda *shape: pl.BlockSpec((bt,) + shape, lambda i: (i,) + (0,) * len(shape))
    lblk = lambda *shape: pl.BlockSpec((None, bt) + shape, lambda i: (layer, i) + (0,) * len(shape))
    state_specs = [lblk(ML_HEADS, ML_DH, ML_DH), lblk(ML_HEADS, ML_DH), lblk(LANES),
                   lblk(CONV_W - 1, SSD_CONV_DIM), lblk(SSD_WIDTH, SSD_STATE), lblk(RET_HEADS, RET_DH, RET_DH)]
    state_shapes = [jax.ShapeDtypeStruct(s.shape, F32) for s in states]
    proj3 = proj.reshape(b, 1, D_PROJ)
    small = proj[:, OFF_SMALL:OFF_SMALL + LANES]
    n_fixed = 13 + N_SAMPLE_STATES
    alias_args = list(prev_out) if prev_out is not None else []
    aliases = {n_fixed + k: 1 + k for k in range(len(alias_args))}
    outs = pl.pallas_call(
        functools.partial(_mixer_sample_kernel, bt=bt, ret_gamma=consts["ret_gamma"], n_alias=len(alias_args)),
        grid=(b // bt,),
        in_specs=[blk(1, D_PROJ), blk(LANES), row(LANES), row(LANES),
                  pl.BlockSpec((CONV_W, SSD_CONV_DIM), lambda i: (0, 0)), row(SSD_CONV_DIM),
                  row(ML_WIDTH), row(SSD_WIDTH), row(SSD_WIDTH), row(RET_WIDTH), row(LANES), row(LANES),
                  pl.BlockSpec((LANES, SSD_WIDTH), lambda i: (0, 0))] + state_specs
                 + [pl.BlockSpec(memory_space=pl.ANY)] * len(alias_args),
        out_specs=[blk(1, D_MIX)] + state_specs,
        out_shape=[jax.ShapeDtypeStruct((b, 1, D_MIX), F32)] + state_shapes,
        input_output_aliases=aliases,
        scratch_shapes=[pltpu.VMEM((bt, LANES), F32)] * 3 + [pltpu.VMEM((bt, SSD_WIDTH), F32)] * 2,
        compiler_params=_cparams(("parallel",)),
        name="mixer_sample",
    )(proj3, small, lw["bias"], lw["alog"], lw["conv_w"], lw["conv_b"], lw["ml_norm"], lw["ssd_d"],
      lw["ssd_norm"], lw["ret_norm"], consts["cos_s"], consts["sin_s"], consts["expand"],
      *states, *alias_args)
    return outs[0].reshape(b, D_MIX), list(outs[1:])


def _pack_in_proj(w):
    sizes = [ML_WIDTH] * 4 + [ML_HEADS, ML_HEADS, SSD_WIDTH, SSD_CONV_DIM, SSD_HEADS] + [RET_WIDTH] * 4
    offs = np.concatenate([[0], np.cumsum(sizes)])
    seg = lambda i: w[:, int(offs[i]):int(offs[i + 1])]
    small = jnp.concatenate([seg(4), seg(5), seg(8)], axis=1)
    small = jnp.pad(small, ((0, 0), (0, LANES - small.shape[1])))
    cols = [seg(0), seg(1), seg(2), seg(3), seg(6), seg(7), seg(9), seg(10), seg(11), seg(12), small]
    return jnp.concatenate(cols, axis=1).astype(BF16)


def _lane_row(parts):
    v = jnp.concatenate([p.astype(F32) for p in parts])
    return jnp.pad(v, (0, LANES - v.shape[0]))[None, :]


def kernel(x_prompt, x_sample, state_mlstm_c, state_mlstm_n, state_mlstm_m, state_ssd_conv, state_ssd, state_ret,
           cache_mem_k, cache_mem_v, mem_prompt,
           norm_mix, w_in, ml_i_bias, ml_f_bias, ml_norm, ssd_conv_w, ssd_conv_b, ssd_dt_bias, ssd_a_log, ssd_d,
           ssd_norm, ret_norm, w_out, norm_ca, norm_mem, w_ca_q, w_ca_k, w_ca_v, w_ca_o, norm_ffn,
           ffn_w_gate, ffn_w_up, ffn_w_down, moe_w_router, moe_w_gate, moe_w_up, moe_w_down, norm_final):
    bp, seq, d = x_prompt.shape
    bs = x_sample.shape[0]
    assert x_sample.shape[1] == 1 and seq % CHUNK == 0 and DEPTH % 2 == 0

    ret_w, ret_cols, ret_decay = _retention_constants(CHUNK)
    _, _, ret_gamma = _retention_constants(1)
    cos_p, sin_p = _rotary_tables(jnp.arange(seq, dtype=F32))
    cos_s, sin_s = _rotary_tables(PAST_LEN + jnp.arange(1, dtype=F32))
    consts = {"ret_w": jnp.asarray(ret_w), "ret_cols": jnp.asarray(ret_cols), "ret_decay": ret_decay,
              "ret_gamma": ret_gamma, "cos_p": cos_p, "sin_p": sin_p, "cos_s": cos_s, "sin_s": sin_s,
              "expand": jnp.asarray(_head_expand_matrix()),
              "expand3": jnp.asarray(np.tile(_head_expand_matrix(), (3, 1))).astype(BF16)}
    zeros12 = jnp.zeros((2 * ML_HEADS,), F32)
    ones_row = jnp.ones((1, D_MIX), F32)

    mem2d = mem_prompt.reshape(bp * MEM_LEN, d)
    mem_k_p, mem_v_p = kv_proj(mem2d, norm_mem[:, None, :], w_ca_k.astype(BF16), w_ca_v.astype(BF16))

    mem_k_p = mem_k_p.reshape(DEPTH, bp, MEM_LEN, d)
    mem_v_p = mem_v_p.reshape(DEPTH, bp, MEM_LEN, d)
    sample_states = (state_mlstm_c, state_mlstm_n,
                     jnp.pad(state_mlstm_m, ((0, 0), (0, 0), (0, LANES - ML_HEADS))), state_ssd_conv,
                     state_ssd.reshape(DEPTH, bs, SSD_WIDTH, SSD_STATE), state_ret)
    st_s = None

    xp = x_prompt
    xs = x_sample.reshape(bs, d)
    new_p = [[] for _ in range(6)]
    for l in range(DEPTH):
        lw = {"norm_mix": norm_mix[l][None, :], "w_in": _pack_in_proj(w_in[l]), "w_out": w_out[l].astype(BF16),
              "bias": _lane_row([ml_i_bias[l], ml_f_bias[l], ssd_dt_bias[l]]),
              "alog": _lane_row([zeros12, ssd_a_log[l]]),
              "conv_w": ssd_conv_w[l], "conv_b": ssd_conv_b[l][None, :], "ml_norm": ml_norm[l][None, :],
              "ssd_d": jnp.repeat(ssd_d[l], SSD_HEADDIM)[None, :], "ssd_norm": ssd_norm[l][None, :],
              "ret_norm": ret_norm[l][None, :]}
        wq, wo = w_ca_q[l].astype(BF16), w_ca_o[l].astype(BF16)
        g_ca, g_ffn = norm_ca[l][None, :], norm_ffn[l][None, :]
        last = l == DEPTH - 1
        j = l // 2
        if l % 2 == 0:
            ffn_w = (ffn_w_gate[j].astype(BF16), ffn_w_up[j].astype(BF16), ffn_w_down[j].astype(BF16))
        else:
            wrt = jnp.pad(moe_w_router[j].T, ((0, 2 * N_EXPERTS - N_EXPERTS), (0, 0)))
            ffn_w = (wrt, moe_w_gate[j].astype(BF16), moe_w_up[j].astype(BF16), moe_w_down[j].astype(BF16))

        def run_ffn(x2d, tm):
            if l % 2 == 0:
                return ffn_dense(x2d, g_ffn, *ffn_w, tm=tm)
            return moe_ffn(x2d, g_ffn, *ffn_w, norm_final[None, :], tm=tm, sub=_moe_sub_rows(tm), final_norm=last)

        xp, c1, n1, m1, conv1, ssd1, ret1 = mixer_prompt(xp, lw, consts, tm=256)
        st_p = (c1, n1, m1[:, 0, :ML_HEADS], conv1,
                ssd1.reshape(bp, SSD_HEADS, SSD_HEADDIM, SSD_STATE), ret1)
        xp = ca_prompt(xp, g_ca, wq, wo, mem_k_p, mem_v_p, l)
        xp = run_ffn(xp.reshape(bp * seq, d), 1024 if l % 2 else 512).reshape(bp, seq, d)

        proj = norm_matmul(xs, lw["norm_mix"], lw["w_in"], tn=D_PROJ // 3)
        y, st_s = mixer_sample(proj, lw, consts, sample_states, l, st_s)
        xs = norm_matmul(y, ones_row, lw["w_out"], residual=xs, norm=False)
        q = norm_matmul(xs, g_ca, wq)
        o = ca_sample(q.reshape(bs, CA_HEADS, CA_DH), cache_mem_k, cache_mem_v, l).reshape(bs, d)
        xs = norm_matmul(o, ones_row[:, :d], wo, residual=xs, norm=False)
        xs = run_ffn(xs, bs)

        for lst, a in zip(new_p, st_p):
            lst.append(a)

    shape5 = (DEPTH, bp, MEM_LEN, CA_HEADS, CA_DH)
    c_s, n_s, m_s, conv_s, ssd_s, ret_s = st_s
    return (xp, xs.reshape(bs, 1, d),
            jnp.stack(new_p[0]), jnp.stack(new_p[1]), jnp.stack(new_p[2]), jnp.stack(new_p[3]),
            jnp.stack(new_p[4]), jnp.stack(new_p[5]), mem_k_p.reshape(shape5), mem_v_p.reshape(shape5),
            c_s, n_s, m_s[:, :, :ML_HEADS], conv_s,
            ssd_s.reshape(DEPTH, bs, SSD_HEADS, SSD_HEADDIM, SSD_STATE), ret_s)
```

```python
import functools
import itertools
import math

import numpy as np
import jax
import jax.numpy as jnp
from jax import lax
from jax.experimental import pallas as pl
from jax.experimental.pallas import tpu as pltpu

F32 = jnp.float32
BF16 = jnp.bfloat16
HIGHEST = lax.Precision.HIGHEST

D_MODEL = 1024
DEPTH = 2
PAST_LEN = 16384
D_MIX = 2 * D_MODEL
ML_WIDTH = 3 * D_MIX // 8
ML_HEADS = 6
ML_DH = ML_WIDTH // ML_HEADS
SSD_WIDTH = 3 * D_MIX // 8
SSD_HEADDIM = 64
SSD_HEADS = SSD_WIDTH // SSD_HEADDIM
SSD_STATE = 128
SSD_GROUPS = 2
SSD_GROUP_WIDTH = SSD_WIDTH // SSD_GROUPS
SSD_CONV_DIM = SSD_WIDTH + 2 * SSD_GROUPS * SSD_STATE
CONV_W = 4
RET_WIDTH = D_MIX // 4
RET_HEADS = 4
RET_DH = RET_WIDTH // RET_HEADS
ROPE_BASE = 10000.0
CHUNK = 128
MEM_LEN = 256
CA_HEADS = 4
CA_DH = D_MODEL // CA_HEADS
D_FF = 11 * D_MODEL // 4
N_EXPERTS = 8
TOP_K = 2
D_EXPERT = D_FF // 2
EPS = 1e-6

LANES = 128
BF16_ROWS = 16
MXU_ROWS = 128

OFF_QML = 0
OFF_KML = OFF_QML + ML_WIDTH
OFF_VML = OFF_KML + ML_WIDTH
OFF_OML = OFF_VML + ML_WIDTH
OFF_Z = OFF_OML + ML_WIDTH
OFF_XBC = OFF_Z + SSD_WIDTH
OFF_QR = OFF_XBC + SSD_CONV_DIM
OFF_KR = OFF_QR + RET_WIDTH
OFF_VR = OFF_KR + RET_WIDTH
OFF_GR = OFF_VR + RET_WIDTH
OFF_SMALL = OFF_GR + RET_WIDTH
D_PROJ = OFF_SMALL + LANES
LANE_I = 0
LANE_F = ML_HEADS
LANE_DT = 2 * ML_HEADS

VMEM_LIMIT = 60000 * 1024


def _cparams(sem):
    return pltpu.CompilerParams(dimension_semantics=sem, vmem_limit_bytes=VMEM_LIMIT)


def _const_spec(shape):
    nd = len(shape)
    return pl.BlockSpec(shape, lambda *_: (0,) * nd, pipeline_mode=pl.Buffered(1))


def _rms(x, g):
    return x * lax.rsqrt(jnp.mean(x * x, axis=-1, keepdims=True) + EPS) * g


def _sigmoid(x):
    return 1.0 / (1.0 + jnp.exp(-x))


def _silu(x):
    return x * _sigmoid(x)


def _dot(a, b):
    return jnp.dot(a, b, preferred_element_type=F32)


def _dot_nt(a, b):
    return lax.dot_general(a, b, (((1,), (1,)), ((), ())), preferred_element_type=F32)


def _dot_exact(a, b):
    return jnp.dot(a, b, preferred_element_type=F32, precision=HIGHEST)


def _split3(v):
    v1 = v.astype(BF16)
    r1 = v - v1.astype(F32)
    v2 = r1.astype(BF16)
    v3 = (r1 - v2.astype(F32)).astype(BF16)
    return v1, v2, v3


def _norm_matmul_kernel(*refs, norm, residual):
    x_ref, g_ref, w_ref = refs[:3]
    o_ref = refs[-1]
    x = x_ref[...].astype(F32)
    if norm:
        x = _rms(x, g_ref[...])
    acc = _dot(x.astype(BF16), w_ref[...])
    if residual:
        acc = acc + refs[3][...]
    o_ref[...] = acc


def norm_matmul(x, g, w, residual=None, *, norm=True, tm=None, tn=None):
    m, k = x.shape
    n = w.shape[1]
    tm = tm or min(m, 512)
    tn = tn or min(n, 1024)
    in_specs = [pl.BlockSpec((tm, k), lambda j, i: (i, 0)),
                pl.BlockSpec((1, k), lambda j, i: (0, 0)),
                pl.BlockSpec((k, tn), lambda j, i: (0, j))]
    args = [x, g, w]
    if residual is not None:
        in_specs.append(pl.BlockSpec((tm, tn), lambda j, i: (i, j)))
        args.append(residual)
    return pl.pallas_call(
        functools.partial(_norm_matmul_kernel, norm=norm, residual=residual is not None),
        grid=(pl.cdiv(n, tn), m // tm),
        in_specs=in_specs,
        out_specs=pl.BlockSpec((tm, tn), lambda j, i: (i, j)),
        out_shape=jax.ShapeDtypeStruct((m, n), F32),
        compiler_params=_cparams(("parallel", "parallel")),
        name="norm_matmul",
    )(*args)


def _kv_proj_kernel(x_ref, g_ref, wk_ref, wv_ref, k_ref, v_ref, kh_ref, vh_ref):
    xn = _rms(x_ref[...], g_ref[...]).astype(BF16)
    k = _dot(xn, wk_ref[...])
    v = _dot(xn, wv_ref[...])
    k_ref[...] = k
    v_ref[...] = v
    for h in range(CA_HEADS):
        kh_ref[:, h, :] = k[:, h * CA_DH:(h + 1) * CA_DH]
        vh_ref[:, h, :] = v[:, h * CA_DH:(h + 1) * CA_DH]


def kv_proj(mem2d, g, wk, wv, *, tm=512):
    m, d = mem2d.shape
    depth = wk.shape[0]
    out = jax.ShapeDtypeStruct((depth, m, d), F32)
    out_h = jax.ShapeDtypeStruct((depth, m, CA_HEADS, CA_DH), F32)
    return pl.pallas_call(
        _kv_proj_kernel,
        grid=(depth, m // tm),
        in_specs=[pl.BlockSpec((tm, d), lambda l, i: (i, 0)),
                  pl.BlockSpec((None, 1, d), lambda l, i: (l, 0, 0)),
                  pl.BlockSpec((None, d, d), lambda l, i: (l, 0, 0)),
                  pl.BlockSpec((None, d, d), lambda l, i: (l, 0, 0))],
        out_specs=[pl.BlockSpec((None, tm, d), lambda l, i: (l, i, 0)),
                   pl.BlockSpec((None, tm, d), lambda l, i: (l, i, 0)),
                   pl.BlockSpec((None, tm, CA_HEADS, CA_DH), lambda l, i: (l, i, 0, 0)),
                   pl.BlockSpec((None, tm, CA_HEADS, CA_DH), lambda l, i: (l, i, 0, 0))],
        out_shape=[out, out, out_h, out_h],
        compiler_params=_cparams(("parallel", "parallel")),
        name="kv_proj",
    )(mem2d, g, wk, wv)


def _ca_prompt_kernel(x_ref, g_ref, wq_ref, wo_ref, k_ref, v_ref, o_ref):
    x = x_ref[...]
    xn = _rms(x, g_ref[...]).astype(BF16)
    q = _dot(xn, wq_ref[...])
    kb = k_ref[...].astype(BF16)
    vb = v_ref[...].astype(BF16)
    heads = [slice(h * CA_DH, (h + 1) * CA_DH) for h in range(CA_HEADS)]

    def scores(sl):
        return _dot_nt(q[:, sl].astype(BF16), kb[:, sl]) * (CA_DH ** -0.5)

    outs = []
    s_next = scores(heads[0])
    for h, sl in enumerate(heads):
        s = s_next
        if h + 1 < CA_HEADS:
            s_next = scores(heads[h + 1])
        p = jnp.exp(s - jnp.max(s, axis=-1, keepdims=True))
        p = p / jnp.sum(p, axis=-1, keepdims=True)
        outs.append(_dot(p.astype(BF16), vb[:, sl]).astype(BF16))
    o = jnp.concatenate(outs, axis=-1)
    o_ref[...] = x + _dot(o, wo_ref[...])


def ca_prompt(x, g, wq, wo, mem_k, mem_v, layer, *, tm=512):
    b, length, d = x.shape
    mlen = mem_k.shape[2]
    return pl.pallas_call(
        _ca_prompt_kernel,
        grid=(b, length // tm),
        in_specs=[pl.BlockSpec((None, tm, d), lambda i, j: (i, j, 0)),
                  _const_spec((1, d)), _const_spec((d, d)), _const_spec((d, d)),
                  pl.BlockSpec((None, None, mlen, d), lambda i, j: (layer, i, 0, 0)),
                  pl.BlockSpec((None, None, mlen, d), lambda i, j: (layer, i, 0, 0))],
        out_specs=pl.BlockSpec((None, tm, d), lambda i, j: (i, j, 0)),
        out_shape=jax.ShapeDtypeStruct(x.shape, F32),
        compiler_params=_cparams(("parallel", "parallel")),
        name="ca_prompt",
    )(x, g, wq, wo, mem_k, mem_v)


def _ca_sample_kernel(q_ref, k_ref, v_ref, o_ref, *, bt):
    for j in range(bt):
        kq = k_ref[j] * (q_ref[j] * (CA_DH ** -0.5))
        s = jnp.sum(kq, axis=-1, keepdims=True)
        p = jnp.exp(s - jnp.max(s, axis=0, keepdims=True))
        o = jnp.sum(p * v_ref[j], axis=0)
        o_ref[j] = o / jnp.sum(p, axis=0)


def ca_sample(q, mem_k, mem_v, layer, *, bt=8):
    b = q.shape[0]
    blk = (None, bt) + mem_k.shape[2:]
    return pl.pallas_call(
        functools.partial(_ca_sample_kernel, bt=bt),
        grid=(b // bt,),
        in_specs=[pl.BlockSpec((bt,) + q.shape[1:], lambda i: (i, 0, 0)),
                  pl.BlockSpec(blk, lambda i: (layer, i, 0, 0, 0)),
                  pl.BlockSpec(blk, lambda i: (layer, i, 0, 0, 0))],
        out_specs=pl.BlockSpec((bt,) + q.shape[1:], lambda i: (i, 0, 0)),
        out_shape=jax.ShapeDtypeStruct(q.shape, F32),
        compiler_params=_cparams(("parallel",)),
        name="ca_sample",
    )(q, mem_k, mem_v)


def _ffn_kernel(x_ref, g_ref, wg_ref, wu_ref, wd_ref, o_ref, a_ref, *, fchunk):
    x = x_ref[...]
    xn = _rms(x, g_ref[...]).astype(BF16)
    dff = wg_ref.shape[1]
    for f0 in range(0, dff, fchunk):
        gate = _dot(xn, wg_ref[:, f0:f0 + fchunk])
        up = _dot(xn, wu_ref[:, f0:f0 + fchunk])
        a_ref[:, f0:f0 + fchunk] = (_silu(gate) * up).astype(BF16)
    o_ref[...] = x + _dot(a_ref[...], wd_ref[...])


def ffn_dense(x, g, wg, wu, wd, *, tm=512):
    m, d = x.shape
    dff = wg.shape[1]
    return pl.pallas_call(
        functools.partial(_ffn_kernel, fchunk=dff // 2),
        grid=(m // tm,),
        in_specs=[pl.BlockSpec((tm, d), lambda i: (i, 0)),
                  _const_spec((1, d)), _const_spec((d, dff)), _const_spec((d, dff)), _const_spec((dff, d))],
        out_specs=pl.BlockSpec((tm, d), lambda i: (i, 0)),
        out_shape=jax.ShapeDtypeStruct((m, d), F32),
        scratch_shapes=[pltpu.VMEM((tm, dff), BF16)],
        compiler_params=_cparams(("parallel",)),
        name="ffn_dense",
    )(x, g, wg, wu, wd)


def _moe_kernel(x_ref, g_ref, wrt_ref, wg_ref, wu_ref, wd_ref, gf_ref, o_ref,
                xn_ref, gate_t_ref, pos_t_ref, *, tm, sub, final_norm):
    e = pl.program_id(1)
    ne = pl.num_programs(1)
    epad = gate_t_ref.shape[0]

    @pl.when(e == 0)
    def _route():
        x = x_ref[...]
        xn = _rms(x, g_ref[...])
        xn_ref[...] = xn.astype(BF16)
        x_hi = xn.astype(BF16)
        x_lo = (xn - x_hi.astype(F32)).astype(BF16)
        w = wrt_ref[...]
        w_hi = w.astype(BF16)
        w_lo = (w - w_hi.astype(F32)).astype(BF16)
        logits = _dot_nt(w_hi, x_hi) + (_dot_nt(w_hi, x_lo) + _dot_nt(w_lo, x_hi))
        row = lax.broadcasted_iota(jnp.int32, (epad, tm), 0)
        logits = jnp.where(row < N_EXPERTS, logits, -jnp.inf)
        m1 = jnp.max(logits, axis=0, keepdims=True)
        i1 = jnp.min(jnp.where(logits == m1, row, epad), axis=0, keepdims=True)
        rest = jnp.where(row == i1, -jnp.inf, logits)
        m2 = jnp.max(rest, axis=0, keepdims=True)
        i2 = jnp.min(jnp.where(rest == m2, row, epad), axis=0, keepdims=True)
        e2 = jnp.exp(m2 - m1)
        den = 1.0 + e2
        gate_t = jnp.where(row == i1, 1.0 / den, 0.0) + jnp.where(row == i2, e2 / den, 0.0)
        sel_t = jnp.where((row == i1) | (row == i2), 1.0, 0.0)
        r_i = lax.broadcasted_iota(jnp.int32, (tm, tm), 0)
        c_i = lax.broadcasted_iota(jnp.int32, (tm, tm), 1)
        upper = jnp.where(r_i < c_i, 1.0, 0.0).astype(BF16)
        pos_t = _dot(sel_t.astype(BF16), upper)
        pos_t = jnp.where(sel_t > 0, pos_t, -1.0)
        gate_t_ref[...] = gate_t
        pos_t_ref[...] = pos_t
        o_ref[...] = x

    pos_row = pos_t_ref[pl.ds(e, 1), :]
    gate_row = gate_t_ref[pl.ds(e, 1), :]
    count = jnp.sum(jnp.where(pos_row >= 0, 1.0, 0.0)).astype(jnp.int32)
    half = sub // 2
    wide = sub + sub // 4
    use_wide = (count > sub) & (count <= wide) if wide < tm else False
    n_full = jnp.where(use_wide, 0, (count + (half - 1)) // sub)
    done = jnp.where(use_wide, count, n_full * sub)

    def expert_rows(base, rows):
        slot = lax.broadcasted_iota(jnp.int32, (rows, tm), 0).astype(F32) + base.astype(F32)
        hit = pos_row == slot
        gather = jnp.where(hit, 1.0, 0.0).astype(BF16)
        gate = jnp.sum(jnp.where(hit, gate_row, 0.0), axis=-1, keepdims=True)
        xg = _dot(gather, xn_ref[...]).astype(BF16)
        act = (_silu(_dot(xg, wg_ref[...])) * _dot(xg, wu_ref[...])).astype(BF16)
        ye = (gate * _dot(act, wd_ref[...])).astype(BF16)
        o_ref[...] += lax.dot_general(gather, ye, (((0,), (0,)), ((), ())), preferred_element_type=F32)

    def full_body(j, carry):
        expert_rows(j * sub, sub)
        return carry

    lax.fori_loop(0, n_full, full_body, 0)

    @pl.when(count > done)
    def _remainder():
        expert_rows(done, half)

    if wide < tm:
        @pl.when(use_wide)
        def _wide():
            expert_rows(jnp.int32(0), wide)

    if final_norm:
        @pl.when(e == ne - 1)
        def _final():
            o_ref[...] = _rms(o_ref[...], gf_ref[...])


def _moe_sub_rows(tm):
    mean = tm * TOP_K // N_EXPERTS
    return min(tm, max(MXU_ROWS, -(-mean // MXU_ROWS) * MXU_ROWS))


def moe_ffn(x, g, w_router_t, wg, wu, wd, g_final, *, tm, sub, final_norm):
    m, d = x.shape
    ne, _, de = wg.shape
    epad = w_router_t.shape[0]
    return pl.pallas_call(
        functools.partial(_moe_kernel, tm=tm, sub=sub, final_norm=final_norm),
        grid=(m // tm, ne),
        in_specs=[pl.BlockSpec((tm, d), lambda i, e: (i, 0)),
                  pl.BlockSpec((1, d), lambda i, e: (0, 0)),
                  pl.BlockSpec((epad, d), lambda i, e: (0, 0)),
                  pl.BlockSpec((None, d, de), lambda i, e: (e, 0, 0)),
                  pl.BlockSpec((None, d, de), lambda i, e: (e, 0, 0)),
                  pl.BlockSpec((None, de, d), lambda i, e: (e, 0, 0)),
                  pl.BlockSpec((1, d), lambda i, e: (0, 0))],
        out_specs=pl.BlockSpec((tm, d), lambda i, e: (i, 0)),
        out_shape=jax.ShapeDtypeStruct((m, d), F32),
        scratch_shapes=[pltpu.VMEM((tm, d), BF16),
                        pltpu.VMEM((epad, tm), F32), pltpu.VMEM((epad, tm), F32)],
        compiler_params=_cparams(("parallel", "arbitrary")),
        name="moe_ffn",
    )(x, g, w_router_t, wg, wu, wd, g_final)


def _retention_constants(c):
    log_g = np.log1p(-np.exp2(-5.0 - np.arange(RET_HEADS, dtype=np.float32))).astype(np.float32)
    j = np.arange(c, dtype=np.float32)
    rel = j[:, None] - j[None, :]
    w_intra = np.where(rel >= 0, np.exp(rel[None] * log_g[:, None, None]), 0.0).astype(np.float32)
    w_inter = np.exp((j[None, :] + 1.0) * log_g[:, None]).astype(np.float32)
    w_end = np.exp((c - 1.0 - j[None, :]) * log_g[:, None]).astype(np.float32)
    decay = np.exp(c * log_g).astype(np.float32)
    cols = np.zeros((c, LANES), np.float32)
    cols[:, :RET_HEADS] = w_inter.T
    cols[:, RET_HEADS:2 * RET_HEADS] = w_end.T
    return w_intra, cols, [float(d) for d in decay]


def _rotary_tables(pos):
    half = RET_DH // 2
    inv = ROPE_BASE ** (-jnp.arange(half, dtype=F32) / half)
    ang = pos[:, None] * inv[None, :]
    cos, sin = jnp.cos(ang), jnp.sin(ang)
    return jnp.concatenate([cos, cos], axis=-1), jnp.concatenate([-sin, sin], axis=-1)


def _head_expand_matrix():
    e = np.zeros((LANES, SSD_WIDTH), np.float32)
    for h in range(SSD_HEADS):
        e[LANE_DT + h, h * SSD_HEADDIM:(h + 1) * SSD_HEADDIM] = 1.0
    return e


def _softplus_parts(v):
    sp = jnp.log1p(jnp.exp(-jnp.abs(v)))
    return jnp.minimum(v, 0.0) - sp, jnp.maximum(v, 0.0) + sp


def _mixer_prompt_kernel(x_ref, xnext_ref, g_ref, win_ref, wout_ref, bias_ref, alog_ref, convw_ref, convb_ref,
                         mlg_ref, ssdd_ref, ssdg_ref, retg_ref, cos_ref, sin_ref, retw_ref, retcol_ref,
                         expand_ref,
                         xo_ref, c_ref, n_ref, m_ref, conv_ref, ssd_ref, ret_ref,
                         proj_s, cs_s, cst_s, smt_s, dtx_s, acx_s, y_s, u_s, *, tm, tiles_per_seq, ret_decay):
    c = CHUNK
    i = pl.program_id(0)

    @pl.when(i % tiles_per_seq == 0)
    def _init():
        c_ref[...] = jnp.zeros_like(c_ref)
        n_ref[...] = jnp.zeros_like(n_ref)
        m_ref[...] = jnp.zeros_like(m_ref)
        ssd_ref[...] = jnp.zeros_like(ssd_ref)
        ret_ref[...] = jnp.zeros_like(ret_ref)
        u_s[0:8, :] = jnp.zeros((8, SSD_CONV_DIM), F32)

    seg_w = 512
    r_i = lax.broadcasted_iota(jnp.int32, (c, c), 0)
    c_i = lax.broadcasted_iota(jnp.int32, (c, c), 1)
    causal = r_i >= c_i
    tril = jnp.where(causal, 1.0, 0.0).astype(BF16)
    lane1 = lax.broadcasted_iota(jnp.int32, (1, LANES), 1)
    lane_c = lax.broadcasted_iota(jnp.int32, (c, LANES), 1)
    a_vec = -jnp.exp(alog_ref[...])
    expand3 = expand_ref[...]
    bias = bias_ref[...]

    def cumsum_rows(v):
        p = _dot(tril, jnp.concatenate(_split3(v), axis=-1))
        return p[:, :LANES] + p[:, LANES:2 * LANES] + p[:, 2 * LANES:]

    def expand_heads(v):
        return _dot(jnp.concatenate(_split3(v), axis=-1), expand3)

    seg_offs = list(range(0, D_PROJ, seg_w))
    assert seg_offs[-1] == OFF_SMALL

    def prepare(src_ref, slot):
        xn = _rms(src_ref[...], g_ref[...]).astype(BF16)
        for off in seg_offs[-1:] + seg_offs[:-1]:
            wdt = min(seg_w, D_PROJ - off)
            proj_s[slot, :, off:off + wdt] = _dot(xn, win_ref[:, off:off + wdt])
            yield
            if off != OFF_SMALL:
                continue
            is_f = (lane_c >= LANE_F) & (lane_c < LANE_DT)
            is_dt = (lane_c >= LANE_DT) & (lane_c < LANE_DT + SSD_HEADS)
            for ci in range(tm // c):
                rows = slice(ci * c, (ci + 1) * c)
                small = proj_s[slot, rows, OFF_SMALL:OFF_SMALL + LANES] + bias
                logf, dt = _softplus_parts(small)
                cs = cumsum_rows(jnp.where(is_f, logf, jnp.where(is_dt, dt * a_vec, 0.0)))
                cs_s[slot, rows, :] = cs
                cst_s[slot, :, rows] = cs.T
                smt_s[slot, :, rows] = small.T
                dtx_s[slot, rows, :] = expand_heads(jnp.where(is_dt, dt, 0.0))
                acx_s[slot, rows, :] = expand_heads(jnp.where(is_dt, cs, 0.0))
                yield

    @pl.when(i == 0)
    def _first():
        for _ in prepare(x_ref, 0):
            pass

    cur = i % 2
    proj_cur = proj_s.at[cur]
    x = x_ref[...]

    def chunk_body(ci):
        rows = slice(ci * c, (ci + 1) * c)
        small = proj_cur[rows, OFF_SMALL:OFF_SMALL + LANES] + bias
        cs = cs_s[cur, rows, :]
        cs_t = cst_s[cur, :, rows]
        small_t = smt_s[cur, :, rows]

        m_vec = m_ref[...]
        m_out = m_vec

        def ml_front(h):
            q = proj_cur[rows, OFF_QML + h * ML_DH:OFF_QML + (h + 1) * ML_DH]
            k = proj_cur[rows, OFF_KML + h * ML_DH:OFF_KML + (h + 1) * ML_DH] * (ML_DH ** -0.5)
            v = proj_cur[rows, OFF_VML + h * ML_DH:OFF_VML + (h + 1) * ML_DH]
            qb, kb, vb = q.astype(BF16), k.astype(BF16), v.astype(BF16)
            cm = c_ref[h]
            return q, k, v, kb, vb, cm, _dot_nt(qb, kb), _dot_nt(qb, cm.astype(BF16))

        def ml_finish(hs, num_intra, num_inter, den):
            hh = (num_intra + num_inter) / den
            hn = hh * lax.rsqrt(jnp.mean(hh * hh, axis=-1, keepdims=True) + EPS)
            og = proj_cur[rows, OFF_OML + hs.start:OFF_OML + hs.stop]
            y_s[rows, hs] = (hn * mlg_ref[:, hs] * _sigmoid(og)).astype(BF16)

        front = ml_front(0)
        pending = None
        for h in range(ML_HEADS):
            q, k, v, kb, vb, cm, qk, cq = front
            if h + 1 < ML_HEADS:
                front = ml_front(h + 1)
            hs = slice(h * ML_DH, (h + 1) * ML_DH)
            b_col = cs[:, LANE_F + h:LANE_F + h + 1]
            b_row = cs_t[LANE_F + h:LANE_F + h + 1, :]
            i_col = small[:, LANE_I + h:LANE_I + h + 1]
            i_row = small_t[LANE_I + h:LANE_I + h + 1, :]
            m_prev = m_vec[:, h:h + 1]
            lw = jnp.where(causal, b_col + (i_row - b_row), -jnp.inf)
            log_a = b_col + m_prev
            m_t = jnp.maximum(log_a, jnp.max(lw, axis=-1, keepdims=True))
            w_intra = jnp.exp(lw - m_t)
            w_inter = jnp.exp(log_a - m_t)
            s = w_intra * qk
            n_row = n_ref[h:h + 1, :]
            num_intra = _dot(s.astype(BF16), vb)
            den = jnp.sum(s, axis=-1, keepdims=True) + w_inter * jnp.sum(q * n_row, axis=-1, keepdims=True)
            den = jnp.maximum(jnp.abs(den), jnp.exp(-m_t))
            m_new = m_t[c - 1:c, :]
            decay = jnp.exp(log_a[c - 1:c, :] - m_new)
            w_end = jnp.exp(b_col[c - 1:c, :] - b_col + i_col - m_new)
            c_ref[h] = decay * cm + _dot((v * w_end).T.astype(BF16), kb)
            n_ref[h:h + 1, :] = decay * n_row + jnp.sum(k * w_end, axis=0, keepdims=True)
            m_out = jnp.where(lane1 == h, m_new, m_out)
            if pending is not None:
                ml_finish(*pending)
            pending = (hs, num_intra, w_inter * cq, den)
            yield
        ml_finish(*pending)
        m_ref[...] = m_out

        u = proj_cur[rows, OFF_XBC:OFF_XBC + SSD_CONV_DIM]
        u_s[8:8 + c, :] = u
        conv = convb_ref[...] + convw_ref[3:4, :] * u
        for jj in range(CONV_W - 1):
            conv = conv + convw_ref[jj:jj + 1, :] * u_s[5 + jj:5 + jj + c, :]
        u_s[0:8, :] = u_s[c:c + 8, :]
        xc = _silu(conv)
        xs = xc[:, :SSD_WIDTH]
        dt_exp = dtx_s[cur, rows, :]
        acum_exp = acx_s[cur, rows, :]
        xdt = xs * dt_exp
        w_end_exp = jnp.exp(acum_exp[c - 1:c, :] - acum_exp)
        xw = xdt * w_end_exp
        ea_exp = jnp.exp(acum_exp)
        lane_lo = lane_c < SSD_HEADDIM
        hpg = SSD_HEADS // SSD_GROUPS
        y_cols = []
        yield
        for g in range(SSD_GROUPS):
            gw = slice(g * SSD_GROUP_WIDTH, (g + 1) * SSD_GROUP_WIDTH)
            bg = xc[:, SSD_WIDTH + g * SSD_STATE:SSD_WIDTH + (g + 1) * SSD_STATE].astype(BF16)
            cg = xc[:, SSD_WIDTH + (SSD_GROUPS + g) * SSD_STATE:
                    SSD_WIDTH + (SSD_GROUPS + g + 1) * SSD_STATE].astype(BF16)
            cb = _dot_nt(cg, bg)
            st = ssd_ref[gw, :]
            inter = _dot_nt(cg, st.astype(BF16)) * ea_exp[:, gw]
            def decayed_scores(pr):
                out = []
                for hidx in (g * hpg + 2 * pr, g * hpg + 2 * pr + 1):
                    a_col = cs[:, LANE_DT + hidx:LANE_DT + hidx + 1]
                    a_row = cs_t[LANE_DT + hidx:LANE_DT + hidx + 1, :]
                    dec = jnp.exp(jnp.where(causal, a_col - a_row, -jnp.inf))
                    out.append((dec * cb).astype(BF16))
                return out

            att = decayed_scores(0)
            for pr in range(hpg // 2):
                h0 = g * hpg + 2 * pr
                xp = xdt[:, h0 * SSD_HEADDIM:(h0 + 2) * SSD_HEADDIM].astype(BF16)
                pair = [_dot(att[0], xp), _dot(att[1], xp)]
                if pr + 1 < hpg // 2:
                    att = decayed_scores(pr + 1)
                y_cols.append(jnp.where(lane_lo, pair[0], pair[1]) + inter[:, pr * LANES:(pr + 1) * LANES])
                yield
            upd = _dot(xw[:, gw].T.astype(BF16), bg)
            for r in range(hpg):
                hidx = g * hpg + r
                scal = jnp.exp(cs[c - 1:c, LANE_DT + hidx:LANE_DT + hidx + 1])
                hr = slice(hidx * SSD_HEADDIM, (hidx + 1) * SSD_HEADDIM)
                ssd_ref[hr, :] = scal * ssd_ref[hr, :] + upd[r * SSD_HEADDIM:(r + 1) * SSD_HEADDIM, :]
            yield
        ys = jnp.concatenate(y_cols, axis=-1) + ssdd_ref[...] * xs
        yz = ys * _silu(proj_cur[rows, OFF_Z:OFF_Z + SSD_WIDTH])
        for g in range(SSD_GROUPS):
            gw = slice(g * SSD_GROUP_WIDTH, (g + 1) * SSD_GROUP_WIDTH)
            seg = yz[:, gw]
            seg = seg * lax.rsqrt(jnp.mean(seg * seg, axis=-1, keepdims=True) + EPS) * ssdg_ref[:, gw]
            y_s[rows, ML_WIDTH + g * SSD_GROUP_WIDTH:ML_WIDTH + (g + 1) * SSD_GROUP_WIDTH] = seg.astype(BF16)
        yield

        cosb = cos_ref[rows, :]
        sinb = sin_ref[rows, :]
        retcol = retcol_ref[...]
        def ret_front(h):
            q = proj_cur[rows, OFF_QR + h * RET_DH:OFF_QR + (h + 1) * RET_DH]
            k = proj_cur[rows, OFF_KR + h * RET_DH:OFF_KR + (h + 1) * RET_DH]
            v = proj_cur[rows, OFF_VR + h * RET_DH:OFF_VR + (h + 1) * RET_DH]
            qr = q * cosb + pltpu.roll(q, RET_DH // 2, 1) * sinb
            kr = (k * cosb + pltpu.roll(k, RET_DH // 2, 1) * sinb) * (RET_DH ** -0.5)
            qb, kb, vb = qr.astype(BF16), kr.astype(BF16), v.astype(BF16)
            sm = ret_ref[h]
            return kr, vb, sm, _dot_nt(qb, kb), _dot(qb, sm.astype(BF16))

        front = ret_front(0)
        for h in range(RET_HEADS):
            kr, vb, sm, qk, qs = front
            if h + 1 < RET_HEADS:
                front = ret_front(h + 1)
            hs = slice(h * RET_DH, (h + 1) * RET_DH)
            gr = proj_cur[rows, OFF_GR + h * RET_DH:OFF_GR + (h + 1) * RET_DH]
            att = qk * retw_ref[h]
            o = _dot(att.astype(BF16), vb) + qs * retcol[:, h:h + 1]
            kw = kr * retcol[:, RET_HEADS + h:RET_HEADS + h + 1]
            ret_ref[h] = ret_decay[h] * sm + _dot(kw.T.astype(BF16), vb)
            on = o * lax.rsqrt(jnp.mean(o * o, axis=-1, keepdims=True) + EPS)
            y_s[rows, ML_WIDTH + SSD_WIDTH + h * RET_DH:ML_WIDTH + SSD_WIDTH + (h + 1) * RET_DH] = (
                on * retg_ref[:, hs] * _silu(gr)).astype(BF16)
            yield

    stages_per_chunk = ML_HEADS + 1 + SSD_GROUPS * (SSD_HEADS // SSD_GROUPS // 2 + 1) + 1 + RET_HEADS
    n_segments = len(seg_offs) + tm // c
    every = max(1, (tm // c) * stages_per_chunk // n_segments)
    segments = prepare(xnext_ref, (i + 1) % 2)
    for k, _ in enumerate(itertools.chain.from_iterable(chunk_body(ci) for ci in range(tm // c))):
        if k % every == 0:
            next(segments, None)
    for _ in segments:
        pass
    xo_ref[...] = x + _dot(y_s[...], wout_ref[...])
    conv_ref[...] = u_s[5:8, :]


def mixer_prompt(x, lw, consts, *, tm):
    b, length, d = x.shape
    c = CHUNK
    tps = length // tm
    n_tiles = b * tps
    x2d = x.reshape(b * length, d)
    row = lambda w: _const_spec((1, w))
    state_spec = lambda *shape: pl.BlockSpec((None,) + shape, lambda i: (i // tps,) + (0,) * len(shape))
    out_shapes = [jax.ShapeDtypeStruct(x2d.shape, F32),
                  jax.ShapeDtypeStruct((b, ML_HEADS, ML_DH, ML_DH), F32),
                  jax.ShapeDtypeStruct((b, ML_HEADS, ML_DH), F32),
                  jax.ShapeDtypeStruct((b, 1, LANES), F32),
                  jax.ShapeDtypeStruct((b, CONV_W - 1, SSD_CONV_DIM), F32),
                  jax.ShapeDtypeStruct((b, SSD_WIDTH, SSD_STATE), F32),
                  jax.ShapeDtypeStruct((b, RET_HEADS, RET_DH, RET_DH), F32)]
    outs = pl.pallas_call(
        functools.partial(_mixer_prompt_kernel, tm=tm, tiles_per_seq=tps, ret_decay=consts["ret_decay"]),
        grid=(n_tiles,),
        in_specs=[pl.BlockSpec((tm, d), lambda i: (i, 0)),
                  pl.BlockSpec((tm, d), lambda i: (jnp.minimum(i + 1, n_tiles - 1), 0)),
                  row(d), _const_spec((d, D_PROJ)), _const_spec((D_MIX, d)),
                  row(LANES), row(LANES), _const_spec((CONV_W, SSD_CONV_DIM)), row(SSD_CONV_DIM),
                  row(ML_WIDTH), row(SSD_WIDTH), row(SSD_WIDTH), row(RET_WIDTH),
                  pl.BlockSpec((tm, LANES), lambda i: (i % tps, 0)),
                  pl.BlockSpec((tm, LANES), lambda i: (i % tps, 0)),
                  _const_spec((RET_HEADS, c, c)), _const_spec((c, LANES)),
                  _const_spec((3 * LANES, SSD_WIDTH))],
        out_specs=[pl.BlockSpec((tm, d), lambda i: (i, 0)),
                   state_spec(ML_HEADS, ML_DH, ML_DH), state_spec(ML_HEADS, ML_DH), state_spec(1, LANES),
                   state_spec(CONV_W - 1, SSD_CONV_DIM), state_spec(SSD_WIDTH, SSD_STATE),
                   state_spec(RET_HEADS, RET_DH, RET_DH)],
        out_shape=out_shapes,
        scratch_shapes=[pltpu.VMEM((2, tm, D_PROJ), F32),
                        pltpu.VMEM((2, tm, LANES), F32), pltpu.VMEM((2, LANES, tm), F32),
                        pltpu.VMEM((2, LANES, tm), F32), pltpu.VMEM((2, tm, SSD_WIDTH), F32),
                        pltpu.VMEM((2, tm, SSD_WIDTH), F32), pltpu.VMEM((tm, D_MIX), BF16),
                        pltpu.VMEM((c + 8, SSD_CONV_DIM), F32)],
        compiler_params=_cparams(("arbitrary",)),
        name="mixer_prompt",
    )(x2d, x2d, lw["norm_mix"], lw["w_in"], lw["w_out"], lw["bias"], lw["alog"], lw["conv_w"], lw["conv_b"],
      lw["ml_norm"], lw["ssd_d"], lw["ssd_norm"], lw["ret_norm"], consts["cos_p"], consts["sin_p"],
      consts["ret_w"], consts["ret_cols"], consts["expand3"])
    return [outs[0].reshape(b, length, d)] + list(outs[1:])


N_SAMPLE_STATES = 6


def _mixer_sample_kernel(*refs, bt, ret_gamma, n_alias):
    (proj_ref, small_ref, bias_ref, alog_ref, convw_ref, convb_ref, mlg_ref, ssdd_ref,
     ssdg_ref, retg_ref, cos_ref, sin_ref, expand_ref,
     c_in, n_in, m_in, conv_in, ssd_in, ret_in) = refs[:19]
    (y_ref, c_out, n_out, m_out, conv_out, ssd_out, ret_out,
     wi_s, wt_s, en_s, dtx_s, eax_s) = refs[19 + n_alias:]
    small = small_ref[...] + bias_ref[...]
    logf, dt = _softplus_parts(small)
    lane_b = lax.broadcasted_iota(jnp.int32, (bt, LANES), 1)
    is_dt = (lane_b >= LANE_DT) & (lane_b < LANE_DT + SSD_HEADS)
    log_a = pltpu.roll(logf, LANES - LANE_F, 1) + m_in[...]
    m_t = jnp.maximum(log_a, small)
    m_out[...] = m_t
    wi_s[...] = jnp.exp(small - m_t)
    wt_s[...] = jnp.exp(log_a - m_t)
    en_s[...] = jnp.exp(-m_t)
    a_vec = -jnp.exp(alog_ref[...])
    expand = expand_ref[...]
    dtx_s[...] = _dot_exact(jnp.where(is_dt, dt, 0.0), expand)
    eax_s[...] = jnp.exp(_dot_exact(jnp.where(is_dt, dt * a_vec, 0.0), expand))

    sub16 = lax.broadcasted_iota(jnp.int32, (BF16_ROWS, LANES), 0)
    row_hi = lax.broadcasted_iota(jnp.int32, (LANES, LANES), 0) < SSD_HEADDIM

    def rows3(a, b, c):
        return jnp.where(sub16 == 0, a, jnp.where(sub16 == 1, b, jnp.where(sub16 == 2, c, 0.0))).astype(BF16)

    def split(v):
        hi = v.astype(BF16).astype(F32)
        return hi, v - hi

    def outer(a, b):
        a_hi, a_lo = split(a)
        b_hi, b_lo = split(b)
        return lax.dot_general(rows3(a_hi, a_hi, a_lo), rows3(b_hi, b_lo, b_hi), (((0,), (0,)), ((), ())),
                               preferred_element_type=F32)

    def row1(a):
        return jnp.where(sub16 == 0, a, 0.0).astype(BF16)

    cosb = cos_ref[...]
    sinb = sin_ref[...]
    hpg = SSD_HEADS // SSD_GROUPS

    sub_l = lax.broadcasted_iota(jnp.int32, (bt, LANES), 0)
    sub_w = lax.broadcasted_iota(jnp.int32, (bt, SSD_WIDTH), 0)

    def body(j):
        pick_l = lambda ref: jnp.sum(jnp.where(sub_l == j, ref[...], 0.0), axis=0, keepdims=True)
        pick_w = lambda ref: jnp.sum(jnp.where(sub_w == j, ref[...], 0.0), axis=0, keepdims=True)
        wi = pick_l(wi_s)
        wt = pick_l(wt_s)
        en = pick_l(en_s)
        def ml_front(h):
            q = proj_ref[j, :,OFF_QML + h * ML_DH:OFF_QML + (h + 1) * ML_DH]
            k = proj_ref[j, :,OFF_KML + h * ML_DH:OFF_KML + (h + 1) * ML_DH] * (ML_DH ** -0.5)
            v = proj_ref[j, :,OFF_VML + h * ML_DH:OFF_VML + (h + 1) * ML_DH]
            return q, k, outer(v, k)

        front = ml_front(0)
        for h in range(ML_HEADS):
            q, k, vk = front
            if h + 1 < ML_HEADS:
                front = ml_front(h + 1)
            hs = slice(h * ML_DH, (h + 1) * ML_DH)
            og = proj_ref[j, :,OFF_OML + h * ML_DH:OFF_OML + (h + 1) * ML_DH]
            w_in_h = wi[:, h:h + 1]
            w_tr_h = wt[:, h:h + 1]
            c_new = w_tr_h * c_in[j, h] + w_in_h * vk
            n_new = w_tr_h * n_in[j, h:h + 1, :] + w_in_h * k
            c_out[j, h] = c_new
            n_out[j, h:h + 1, :] = n_new
            num = _dot_nt(row1(q), c_new.astype(BF16))[0:1, :]
            den = jnp.sum(n_new * q, axis=-1, keepdims=True)
            den = jnp.maximum(jnp.abs(den), en[:, h:h + 1])
            hh = num / den
            hn = hh * lax.rsqrt(jnp.mean(hh * hh, axis=-1, keepdims=True) + EPS)
            y_ref[j, :,hs] = hn * mlg_ref[:, hs] * _sigmoid(og)
            yield
        u = proj_ref[j, :,OFF_XBC:OFF_XBC + SSD_CONV_DIM]
        prev = conv_in[j]
        conv = convb_ref[...] + convw_ref[CONV_W - 1:CONV_W, :] * u
        for jj in range(CONV_W - 1):
            conv = conv + convw_ref[jj:jj + 1, :] * prev[jj:jj + 1, :]
        conv_out[j, 0:CONV_W - 2, :] = prev[1:CONV_W - 1, :]
        conv_out[j, CONV_W - 2:CONV_W - 1, :] = u
        xc = _silu(conv)
        xs = xc[:, :SSD_WIDTH]
        xdt = xs * pick_w(dtx_s)
        ea = pick_w(eax_s)
        y_cols = []
        n_pairs = SSD_HEADS // 2

        def ssd_front(p):
            g = p // (hpg // 2)
            bg = xc[:, SSD_WIDTH + g * SSD_STATE:SSD_WIDTH + (g + 1) * SSD_STATE]
            pw = slice(2 * p * SSD_HEADDIM, (2 * p + 2) * SSD_HEADDIM)
            return pw, outer(xdt[:, pw], bg)

        front = ssd_front(0)
        for p in range(n_pairs):
            pw, xb = front
            if p + 1 < n_pairs:
                front = ssd_front(p + 1)
            g = p // (hpg // 2)
            cg = xc[:, SSD_WIDTH + (SSD_GROUPS + g) * SSD_STATE:SSD_WIDTH + (SSD_GROUPS + g + 1) * SSD_STATE]
            lo = pw.start
            decay = jnp.where(row_hi, ea[:, lo:lo + 1], ea[:, lo + SSD_HEADDIM:lo + SSD_HEADDIM + 1])
            st_new = decay * ssd_in[j, pw, :] + xb
            ssd_out[j, pw, :] = st_new
            y_cols.append(_dot_nt(row1(cg), st_new.astype(BF16))[0:1, :])
            yield
        ys = jnp.concatenate(y_cols, axis=-1) + ssdd_ref[...] * xs
        yz = ys * _silu(proj_ref[j, :,OFF_Z:OFF_Z + SSD_WIDTH])
        for g in range(SSD_GROUPS):
            gw = slice(g * SSD_GROUP_WIDTH, (g + 1) * SSD_GROUP_WIDTH)
            seg = yz[:, gw]
            y_ref[j, :,ML_WIDTH + g * SSD_GROUP_WIDTH:ML_WIDTH + (g + 1) * SSD_GROUP_WIDTH] = (
                seg * lax.rsqrt(jnp.mean(seg * seg, axis=-1, keepdims=True) + EPS) * ssdg_ref[:, gw])
        def ret_front(h):
            q = proj_ref[j, :,OFF_QR + h * RET_DH:OFF_QR + (h + 1) * RET_DH]
            k = proj_ref[j, :,OFF_KR + h * RET_DH:OFF_KR + (h + 1) * RET_DH]
            v = proj_ref[j, :,OFF_VR + h * RET_DH:OFF_VR + (h + 1) * RET_DH]
            qr = q * cosb + pltpu.roll(q, RET_DH // 2, 1) * sinb
            kr = (k * cosb + pltpu.roll(k, RET_DH // 2, 1) * sinb) * (RET_DH ** -0.5)
            return qr, outer(kr, v)

        front = ret_front(0)
        for h in range(RET_HEADS):
            qr, kv = front
            if h + 1 < RET_HEADS:
                front = ret_front(h + 1)
            hs = slice(h * RET_DH, (h + 1) * RET_DH)
            gr = proj_ref[j, :,OFF_GR + h * RET_DH:OFF_GR + (h + 1) * RET_DH]
            s_new = ret_gamma[h] * ret_in[j, h] + kv
            ret_out[j, h] = s_new
            o = _dot(row1(qr), s_new.astype(BF16))[0:1, :]
            on = o * lax.rsqrt(jnp.mean(o * o, axis=-1, keepdims=True) + EPS)
            y_ref[j, :,ML_WIDTH + SSD_WIDTH + h * RET_DH:ML_WIDTH + SSD_WIDTH + (h + 1) * RET_DH] = (
                on * retg_ref[:, hs] * _silu(gr))
            yield

    def body_pair(jj, carry):
        for _ in itertools.zip_longest(body(2 * jj), body(2 * jj + 1)):
            pass
        return carry

    lax.fori_loop(0, bt // 2, body_pair, 0)


def mixer_sample(proj, lw, consts, states, layer, prev_out=None, *, bt=8):
    b = proj.shape[0]
    row = lambda w: pl.BlockSpec((1, w), lambda i: (0, 0))
    blk = lambda *shape: pl.BlockSpec((bt,) + shape, lambda i: (i,) + (0,) * len(shape))
    lblk = lambda *shape: pl.BlockSpec((None, bt) + shape, lambda i: (layer, i) + (0,) * len(shape))
    state_specs = [lblk(ML_HEADS, ML_DH, ML_DH), lblk(ML_HEADS, ML_DH), lblk(LANES),
                   lblk(CONV_W - 1, SSD_CONV_DIM), lblk(SSD_WIDTH, SSD_STATE), lblk(RET_HEADS, RET_DH, RET_DH)]
    state_shapes = [jax.ShapeDtypeStruct(s.shape, F32) for s in states]
    proj3 = proj.reshape(b, 1, D_PROJ)
    small = proj[:, OFF_SMALL:OFF_SMALL + LANES]
    n_fixed = 13 + N_SAMPLE_STATES
    alias_args = list(prev_out) if prev_out is not None else []
    aliases = {n_fixed + k: 1 + k for k in range(len(alias_args))}
    outs = pl.pallas_call(
        functools.partial(_mixer_sample_kernel, bt=bt, ret_gamma=consts["ret_gamma"], n_alias=len(alias_args)),
        grid=(b // bt,),
        in_specs=[blk(1, D_PROJ), blk(LANES), row(LANES), row(LANES),
                  pl.BlockSpec((CONV_W, SSD_CONV_DIM), lambda i: (0, 0)), row(SSD_CONV_DIM),
                  row(ML_WIDTH), row(SSD_WIDTH), row(SSD_WIDTH), row(RET_WIDTH), row(LANES), row(LANES),
                  pl.BlockSpec((LANES, SSD_WIDTH), lambda i: (0, 0))] + state_specs
                 + [pl.BlockSpec(memory_space=pl.ANY)] * len(alias_args),
        out_specs=[blk(1, D_MIX)] + state_specs,
        out_shape=[jax.ShapeDtypeStruct((b, 1, D_MIX), F32)] + state_shapes,
        input_output_aliases=aliases,
        scratch_shapes=[pltpu.VMEM((bt, LANES), F32)] * 3 + [pltpu.VMEM((bt, SSD_WIDTH), F32)] * 2,
        compiler_params=_cparams(("parallel",)),
        name="mixer_sample",
    )(proj3, small, lw["bias"], lw["alog"], lw["conv_w"], lw["conv_b"], lw["ml_norm"], lw["ssd_d"],
      lw["ssd_norm"], lw["ret_norm"], consts["cos_s"], consts["sin_s"], consts["expand"],
      *states, *alias_args)
    return outs[0].reshape(b, D_MIX), list(outs[1:])


def _pack_in_proj(w):
    sizes = [ML_WIDTH] * 4 + [ML_HEADS, ML_HEADS, SSD_WIDTH, SSD_CONV_DIM, SSD_HEADS] + [RET_WIDTH] * 4
    offs = np.concatenate([[0], np.cumsum(sizes)])
    seg = lambda i: w[:, int(offs[i]):int(offs[i + 1])]
    small = jnp.concatenate([seg(4), seg(5), seg(8)], axis=1)
    small = jnp.pad(small, ((0, 0), (0, LANES - small.shape[1])))
    cols = [seg(0), seg(1), seg(2), seg(3), seg(6), seg(7), seg(9), seg(10), seg(11), seg(12), small]
    return jnp.concatenate(cols, axis=1).astype(BF16)


def _lane_row(parts):
    v = jnp.concatenate([p.astype(F32) for p in parts])
    return jnp.pad(v, (0, LANES - v.shape[0]))[None, :]


def kernel(x_prompt, x_sample, state_mlstm_c, state_mlstm_n, state_mlstm_m, state_ssd_conv, state_ssd, state_ret,
           cache_mem_k, cache_mem_v, mem_prompt,
           norm_mix, w_in, ml_i_bias, ml_f_bias, ml_norm, ssd_conv_w, ssd_conv_b, ssd_dt_bias, ssd_a_log, ssd_d,
           ssd_norm, ret_norm, w_out, norm_ca, norm_mem, w_ca_q, w_ca_k, w_ca_v, w_ca_o, norm_ffn,
           ffn_w_gate, ffn_w_up, ffn_w_down, moe_w_router, moe_w_gate, moe_w_up, moe_w_down, norm_final):
    bp, seq, d = x_prompt.shape
    bs = x_sample.shape[0]
    assert x_sample.shape[1] == 1 and seq % CHUNK == 0 and DEPTH % 2 == 0

    ret_w, ret_cols, ret_decay = _retention_constants(CHUNK)
    _, _, ret_gamma = _retention_constants(1)
    cos_p, sin_p = _rotary_tables(jnp.arange(seq, dtype=F32))
    cos_s, sin_s = _rotary_tables(PAST_LEN + jnp.arange(1, dtype=F32))
    consts = {"ret_w": jnp.asarray(ret_w), "ret_cols": jnp.asarray(ret_cols), "ret_decay": ret_decay,
              "ret_gamma": ret_gamma, "cos_p": cos_p, "sin_p": sin_p, "cos_s": cos_s, "sin_s": sin_s,
              "expand": jnp.asarray(_head_expand_matrix()),
              "expand3": jnp.asarray(np.tile(_head_expand_matrix(), (3, 1))).astype(BF16)}
    zeros12 = jnp.zeros((2 * ML_HEADS,), F32)
    ones_row = jnp.ones((1, D_MIX), F32)

    mem2d = mem_prompt.reshape(bp * MEM_LEN, d)
    mem_k_p, mem_v_p, mem_k_out, mem_v_out = kv_proj(mem2d, norm_mem[:, None, :], w_ca_k.astype(BF16),
                                                     w_ca_v.astype(BF16))

    mem_k_p = mem_k_p.reshape(DEPTH, bp, MEM_LEN, d)
    mem_v_p = mem_v_p.reshape(DEPTH, bp, MEM_LEN, d)
    sample_states = (state_mlstm_c, state_mlstm_n,
                     jnp.pad(state_mlstm_m, ((0, 0), (0, 0), (0, LANES - ML_HEADS))), state_ssd_conv,
                     state_ssd.reshape(DEPTH, bs, SSD_WIDTH, SSD_STATE), state_ret)
    st_s = None

    xp = x_prompt
    xs = x_sample.reshape(bs, d)
    new_p = [[] for _ in range(6)]
    for l in range(DEPTH):
        lw = {"norm_mix": norm_mix[l][None, :], "w_in": _pack_in_proj(w_in[l]), "w_out": w_out[l].astype(BF16),
              "bias": _lane_row([ml_i_bias[l], ml_f_bias[l], ssd_dt_bias[l]]),
              "alog": _lane_row([zeros12, ssd_a_log[l]]),
              "conv_w": ssd_conv_w[l], "conv_b": ssd_conv_b[l][None, :], "ml_norm": ml_norm[l][None, :],
              "ssd_d": jnp.repeat(ssd_d[l], SSD_HEADDIM)[None, :], "ssd_norm": ssd_norm[l][None, :],
              "ret_norm": ret_norm[l][None, :]}
        wq, wo = w_ca_q[l].astype(BF16), w_ca_o[l].astype(BF16)
        g_ca, g_ffn = norm_ca[l][None, :], norm_ffn[l][None, :]
        last = l == DEPTH - 1
        j = l // 2
        if l % 2 == 0:
            ffn_w = (ffn_w_gate[j].astype(BF16), ffn_w_up[j].astype(BF16), ffn_w_down[j].astype(BF16))
        else:
            wrt = jnp.pad(moe_w_router[j].T, ((0, 2 * N_EXPERTS - N_EXPERTS), (0, 0)))
            ffn_w = (wrt, moe_w_gate[j].astype(BF16), moe_w_up[j].astype(BF16), moe_w_down[j].astype(BF16))

        def run_ffn(x2d, tm):
            if l % 2 == 0:
                return ffn_dense(x2d, g_ffn, *ffn_w, tm=tm)
            return moe_ffn(x2d, g_ffn, *ffn_w, norm_final[None, :], tm=tm, sub=_moe_sub_rows(tm), final_norm=last)

        xp, c1, n1, m1, conv1, ssd1, ret1 = mixer_prompt(xp, lw, consts, tm=256)
        st_p = (c1, n1, m1[:, 0, :ML_HEADS], conv1,
                ssd1.reshape(bp, SSD_HEADS, SSD_HEADDIM, SSD_STATE), ret1)
        xp = ca_prompt(xp, g_ca, wq, wo, mem_k_p, mem_v_p, l)
        xp = run_ffn(xp.reshape(bp * seq, d), 1024 if l % 2 else 512).reshape(bp, seq, d)

        proj = norm_matmul(xs, lw["norm_mix"], lw["w_in"], tn=D_PROJ // 3)
        y, st_s = mixer_sample(proj, lw, consts, sample_states, l, st_s)
        xs = norm_matmul(y, ones_row, lw["w_out"], residual=xs, norm=False)
        q = norm_matmul(xs, g_ca, wq)
        o = ca_sample(q.reshape(bs, CA_HEADS, CA_DH), cache_mem_k, cache_mem_v, l).reshape(bs, d)
        xs = norm_matmul(o, ones_row[:, :d], wo, residual=xs, norm=False)
        xs = run_ffn(xs, bs)

        for lst, a in zip(new_p, st_p):
            lst.append(a)

    shape5 = (DEPTH, bp, MEM_LEN, CA_HEADS, CA_DH)
    c_s, n_s, m_s, conv_s, ssd_s, ret_s = st_s
    return (xp, xs.reshape(bs, 1, d),
            jnp.stack(new_p[0]), jnp.stack(new_p[1]), jnp.stack(new_p[2]), jnp.stack(new_p[3]),
            jnp.stack(new_p[4]), jnp.stack(new_p[5]), mem_k_out.reshape(shape5), mem_v_out.reshape(shape5),
            c_s, n_s, m_s[:, :, :ML_HEADS], conv_s,
            ssd_s.reshape(DEPTH, bs, SSD_HEADS, SSD_HEADDIM, SSD_STATE), ret_s)
```

```python
import functools
import itertools
import math

import numpy as np
import jax
import jax.numpy as jnp
from jax import lax
from jax.experimental import pallas as pl
from jax.experimental.pallas import tpu as pltpu

F32 = jnp.float32
BF16 = jnp.bfloat16
HIGHEST = lax.Precision.HIGHEST

D_MODEL = 1024
DEPTH = 2
PAST_LEN = 16384
D_MIX = 2 * D_MODEL
ML_WIDTH = 3 * D_MIX // 8
ML_HEADS = 6
ML_DH = ML_WIDTH // ML_HEADS
SSD_WIDTH = 3 * D_MIX // 8
SSD_HEADDIM = 64
SSD_HEADS = SSD_WIDTH // SSD_HEADDIM
SSD_STATE = 128
SSD_GROUPS = 2
SSD_GROUP_WIDTH = SSD_WIDTH // SSD_GROUPS
SSD_CONV_DIM = SSD_WIDTH + 2 * SSD_GROUPS * SSD_STATE
CONV_W = 4
RET_WIDTH = D_MIX // 4
RET_HEADS = 4
RET_DH = RET_WIDTH // RET_HEADS
ROPE_BASE = 10000.0
CHUNK = 128
MEM_LEN = 256
CA_HEADS = 4
CA_DH = D_MODEL // CA_HEADS
D_FF = 11 * D_MODEL // 4
N_EXPERTS = 8
TOP_K = 2
D_EXPERT = D_FF // 2
EPS = 1e-6

LANES = 128
BF16_ROWS = 16
MXU_ROWS = 128

OFF_QML = 0
OFF_KML = OFF_QML + ML_WIDTH
OFF_VML = OFF_KML + ML_WIDTH
OFF_OML = OFF_VML + ML_WIDTH
OFF_Z = OFF_OML + ML_WIDTH
OFF_XBC = OFF_Z + SSD_WIDTH
OFF_QR = OFF_XBC + SSD_CONV_DIM
OFF_KR = OFF_QR + RET_WIDTH
OFF_VR = OFF_KR + RET_WIDTH
OFF_GR = OFF_VR + RET_WIDTH
OFF_SMALL = OFF_GR + RET_WIDTH
D_PROJ = OFF_SMALL + LANES
LANE_I = 0
LANE_F = ML_HEADS
LANE_DT = 2 * ML_HEADS

VMEM_LIMIT = 60000 * 1024


def _cparams(sem):
    return pltpu.CompilerParams(dimension_semantics=sem, vmem_limit_bytes=VMEM_LIMIT)


def _const_spec(shape):
    nd = len(shape)
    return pl.BlockSpec(shape, lambda *_: (0,) * nd, pipeline_mode=pl.Buffered(1))


def _rms(x, g):
    return x * lax.rsqrt(jnp.mean(x * x, axis=-1, keepdims=True) + EPS) * g


def _sigmoid(x):
    return 1.0 / (1.0 + jnp.exp(-x))


def _silu(x):
    return x * _sigmoid(x)


def _dot(a, b):
    return jnp.dot(a, b, preferred_element_type=F32)


def _dot_nt(a, b):
    return lax.dot_general(a, b, (((1,), (1,)), ((), ())), preferred_element_type=F32)


def _dot_exact(a, b):
    return jnp.dot(a, b, preferred_element_type=F32, precision=HIGHEST)


def _split3(v):
    v1 = v.astype(BF16)
    r1 = v - v1.astype(F32)
    v2 = r1.astype(BF16)
    v3 = (r1 - v2.astype(F32)).astype(BF16)
    return v1, v2, v3


def _norm_matmul_kernel(*refs, norm, residual):
    x_ref, g_ref, w_ref = refs[:3]
    o_ref = refs[-1]
    x = x_ref[...].astype(F32)
    if norm:
        x = _rms(x, g_ref[...])
    acc = _dot(x.astype(BF16), w_ref[...])
    if residual:
        acc = acc + refs[3][...]
    o_ref[...] = acc


def norm_matmul(x, g, w, residual=None, *, norm=True, tm=None, tn=None):
    m, k = x.shape
    n = w.shape[1]
    tm = tm or min(m, 512)
    tn = tn or min(n, 1024)
    in_specs = [pl.BlockSpec((tm, k), lambda j, i: (i, 0)),
                pl.BlockSpec((1, k), lambda j, i: (0, 0)),
                pl.BlockSpec((k, tn), lambda j, i: (0, j))]
    args = [x, g, w]
    if residual is not None:
        in_specs.append(pl.BlockSpec((tm, tn), lambda j, i: (i, j)))
        args.append(residual)
    return pl.pallas_call(
        functools.partial(_norm_matmul_kernel, norm=norm, residual=residual is not None),
        grid=(pl.cdiv(n, tn), m // tm),
        in_specs=in_specs,
        out_specs=pl.BlockSpec((tm, tn), lambda j, i: (i, j)),
        out_shape=jax.ShapeDtypeStruct((m, n), F32),
        compiler_params=_cparams(("parallel", "parallel")),
        name="norm_matmul",
    )(*args)


def _kv_proj_kernel(x_ref, g_ref, wk_ref, wv_ref, k_ref, v_ref, kh_ref, vh_ref):
    xn = _rms(x_ref[...], g_ref[...]).astype(BF16)
    k = _dot(xn, wk_ref[...])
    v = _dot(xn, wv_ref[...])
    k_ref[...] = k
    v_ref[...] = v
    for h in range(CA_HEADS):
        kh_ref[:, h, :] = k[:, h * CA_DH:(h + 1) * CA_DH]
        vh_ref[:, h, :] = v[:, h * CA_DH:(h + 1) * CA_DH]


def kv_proj(mem2d, g, wk, wv, *, tm=512):
    m, d = mem2d.shape
    depth = wk.shape[0]
    out = jax.ShapeDtypeStruct((depth, m, d), F32)
    out_h = jax.ShapeDtypeStruct((depth, m, CA_HEADS, CA_DH), F32)
    return pl.pallas_call(
        _kv_proj_kernel,
        grid=(depth, m // tm),
        in_specs=[pl.BlockSpec((tm, d), lambda l, i: (i, 0)),
                  pl.BlockSpec((None, 1, d), lambda l, i: (l, 0, 0)),
                  pl.BlockSpec((None, d, d), lambda l, i: (l, 0, 0)),
                  pl.BlockSpec((None, d, d), lambda l, i: (l, 0, 0))],
        out_specs=[pl.BlockSpec((None, tm, d), lambda l, i: (l, i, 0)),
                   pl.BlockSpec((None, tm, d), lambda l, i: (l, i, 0)),
                   pl.BlockSpec((None, tm, CA_HEADS, CA_DH), lambda l, i: (l, i, 0, 0)),
                   pl.BlockSpec((None, tm, CA_HEADS, CA_DH), lambda l, i: (l, i, 0, 0))],
        out_shape=[out, out, out_h, out_h],
        compiler_params=_cparams(("parallel", "parallel")),
        name="kv_proj",
    )(mem2d, g, wk, wv)


def _ca_prompt_kernel(x_ref, g_ref, wq_ref, wo_ref, k_ref, v_ref, o_ref):
    x = x_ref[...]
    xn = _rms(x, g_ref[...]).astype(BF16)
    q = _dot(xn, wq_ref[...])
    kb = k_ref[...].astype(BF16)
    vb = v_ref[...].astype(BF16)
    heads = [slice(h * CA_DH, (h + 1) * CA_DH) for h in range(CA_HEADS)]

    def scores(sl):
        return _dot_nt(q[:, sl].astype(BF16), kb[:, sl]) * (CA_DH ** -0.5)

    outs = []
    s_next = scores(heads[0])
    for h, sl in enumerate(heads):
        s = s_next
        if h + 1 < CA_HEADS:
            s_next = scores(heads[h + 1])
        p = jnp.exp(s - jnp.max(s, axis=-1, keepdims=True))
        p = p / jnp.sum(p, axis=-1, keepdims=True)
        outs.append(_dot(p.astype(BF16), vb[:, sl]).astype(BF16))
    o = jnp.concatenate(outs, axis=-1)
    o_ref[...] = x + _dot(o, wo_ref[...])


def ca_prompt(x, g, wq, wo, mem_k, mem_v, layer, *, tm=512):
    b, length, d = x.shape
    mlen = mem_k.shape[2]
    return pl.pallas_call(
        _ca_prompt_kernel,
        grid=(b, length // tm),
        in_specs=[pl.BlockSpec((None, tm, d), lambda i, j: (i, j, 0)),
                  _const_spec((1, d)), _const_spec((d, d)), _const_spec((d, d)),
                  pl.BlockSpec((None, None, mlen, d), lambda i, j: (layer, i, 0, 0)),
                  pl.BlockSpec((None, None, mlen, d), lambda i, j: (layer, i, 0, 0))],
        out_specs=pl.BlockSpec((None, tm, d), lambda i, j: (i, j, 0)),
        out_shape=jax.ShapeDtypeStruct(x.shape, F32),
        compiler_params=_cparams(("parallel", "parallel")),
        name="ca_prompt",
    )(x, g, wq, wo, mem_k, mem_v)


def _ca_sample_kernel(q_ref, k_ref, v_ref, o_ref, *, bt):
    for j in range(bt):
        kq = k_ref[j] * (q_ref[j] * (CA_DH ** -0.5))
        s = jnp.sum(kq, axis=-1, keepdims=True)
        p = jnp.exp(s - jnp.max(s, axis=0, keepdims=True))
        o = jnp.sum(p * v_ref[j], axis=0)
        o_ref[j] = o / jnp.sum(p, axis=0)


def ca_sample(q, mem_k, mem_v, layer, *, bt=8):
    b = q.shape[0]
    blk = (None, bt) + mem_k.shape[2:]
    return pl.pallas_call(
        functools.partial(_ca_sample_kernel, bt=bt),
        grid=(b // bt,),
        in_specs=[pl.BlockSpec((bt,) + q.shape[1:], lambda i: (i, 0, 0)),
                  pl.BlockSpec(blk, lambda i: (layer, i, 0, 0, 0)),
                  pl.BlockSpec(blk, lambda i: (layer, i, 0, 0, 0))],
        out_specs=pl.BlockSpec((bt,) + q.shape[1:], lambda i: (i, 0, 0)),
        out_shape=jax.ShapeDtypeStruct(q.shape, F32),
        compiler_params=_cparams(("parallel",)),
        name="ca_sample",
    )(q, mem_k, mem_v)


def _ffn_kernel(x_ref, g_ref, wg_ref, wu_ref, wd_ref, o_ref, a_ref, *, fchunk):
    x = x_ref[...]
    xn = _rms(x, g_ref[...]).astype(BF16)
    dff = wg_ref.shape[1]
    for f0 in range(0, dff, fchunk):
        gate = _dot(xn, wg_ref[:, f0:f0 + fchunk])
        up = _dot(xn, wu_ref[:, f0:f0 + fchunk])
        a_ref[:, f0:f0 + fchunk] = (_silu(gate) * up).astype(BF16)
    o_ref[...] = x + _dot(a_ref[...], wd_ref[...])


def ffn_dense(x, g, wg, wu, wd, *, tm=512):
    m, d = x.shape
    dff = wg.shape[1]
    return pl.pallas_call(
        functools.partial(_ffn_kernel, fchunk=dff // 2),
        grid=(m // tm,),
        in_specs=[pl.BlockSpec((tm, d), lambda i: (i, 0)),
                  _const_spec((1, d)), _const_spec((d, dff)), _const_spec((d, dff)), _const_spec((dff, d))],
        out_specs=pl.BlockSpec((tm, d), lambda i: (i, 0)),
        out_shape=jax.ShapeDtypeStruct((m, d), F32),
        scratch_shapes=[pltpu.VMEM((tm, dff), BF16)],
        compiler_params=_cparams(("parallel",)),
        name="ffn_dense",
    )(x, g, wg, wu, wd)


def _moe_kernel(x_ref, g_ref, wrt_ref, wg_ref, wu_ref, wd_ref, gf_ref, o_ref,
                xn_ref, gate_t_ref, pos_t_ref, *, tm, sub, final_norm):
    e = pl.program_id(1)
    ne = pl.num_programs(1)
    epad = gate_t_ref.shape[0]

    @pl.when(e == 0)
    def _route():
        x = x_ref[...]
        xn = _rms(x, g_ref[...])
        xn_ref[...] = xn.astype(BF16)
        x_hi = xn.astype(BF16)
        x_lo = (xn - x_hi.astype(F32)).astype(BF16)
        w = wrt_ref[...]
        w_hi = w.astype(BF16)
        w_lo = (w - w_hi.astype(F32)).astype(BF16)
        logits = _dot_nt(w_hi, x_hi) + (_dot_nt(w_hi, x_lo) + _dot_nt(w_lo, x_hi))
        row = lax.broadcasted_iota(jnp.int32, (epad, tm), 0)
        logits = jnp.where(row < N_EXPERTS, logits, -jnp.inf)
        m1 = jnp.max(logits, axis=0, keepdims=True)
        i1 = jnp.min(jnp.where(logits == m1, row, epad), axis=0, keepdims=True)
        rest = jnp.where(row == i1, -jnp.inf, logits)
        m2 = jnp.max(rest, axis=0, keepdims=True)
        i2 = jnp.min(jnp.where(rest == m2, row, epad), axis=0, keepdims=True)
        e2 = jnp.exp(m2 - m1)
        den = 1.0 + e2
        gate_t = jnp.where(row == i1, 1.0 / den, 0.0) + jnp.where(row == i2, e2 / den, 0.0)
        sel_t = jnp.where((row == i1) | (row == i2), 1.0, 0.0)
        r_i = lax.broadcasted_iota(jnp.int32, (tm, tm), 0)
        c_i = lax.broadcasted_iota(jnp.int32, (tm, tm), 1)
        upper = jnp.where(r_i < c_i, 1.0, 0.0).astype(BF16)
        pos_t = _dot(sel_t.astype(BF16), upper)
        pos_t = jnp.where(sel_t > 0, pos_t, -1.0)
        gate_t_ref[...] = gate_t
        pos_t_ref[...] = pos_t
        o_ref[...] = x

    pos_row = pos_t_ref[pl.ds(e, 1), :]
    gate_row = gate_t_ref[pl.ds(e, 1), :]
    count = jnp.sum(jnp.where(pos_row >= 0, 1.0, 0.0)).astype(jnp.int32)
    half = sub // 2
    mid = sub + sub // 8
    wide = sub + sub // 4
    use_mid = (count > sub) & (count <= mid) if wide < tm else False
    use_wide = (count > mid) & (count <= wide) if wide < tm else False
    one_tile = use_mid | use_wide
    n_full = jnp.where(one_tile, 0, (count + (half - 1)) // sub)
    done = jnp.where(one_tile, count, n_full * sub)

    def expert_rows(base, rows):
        slot = lax.broadcasted_iota(jnp.int32, (rows, tm), 0).astype(F32) + base.astype(F32)
        hit = pos_row == slot
        gather = jnp.where(hit, 1.0, 0.0).astype(BF16)
        gate = jnp.sum(jnp.where(hit, gate_row, 0.0), axis=-1, keepdims=True)
        xg = _dot(gather, xn_ref[...]).astype(BF16)
        act = (_silu(_dot(xg, wg_ref[...])) * _dot(xg, wu_ref[...])).astype(BF16)
        ye = (gate * _dot(act, wd_ref[...])).astype(BF16)
        o_ref[...] += lax.dot_general(gather, ye, (((0,), (0,)), ((), ())), preferred_element_type=F32)

    def full_body(j, carry):
        expert_rows(j * sub, sub)
        return carry

    lax.fori_loop(0, n_full, full_body, 0)

    @pl.when(count > done)
    def _remainder():
        expert_rows(done, half)

    if wide < tm:
        @pl.when(use_mid)
        def _mid():
            expert_rows(jnp.int32(0), mid)

        @pl.when(use_wide)
        def _wide():
            expert_rows(jnp.int32(0), wide)

    if final_norm:
        @pl.when(e == ne - 1)
        def _final():
            o_ref[...] = _rms(o_ref[...], gf_ref[...])


def _moe_sub_rows(tm):
    mean = tm * TOP_K // N_EXPERTS
    return min(tm, max(MXU_ROWS, -(-mean // MXU_ROWS) * MXU_ROWS))


def moe_ffn(x, g, w_router_t, wg, wu, wd, g_final, *, tm, sub, final_norm):
    m, d = x.shape
    ne, _, de = wg.shape
    epad = w_router_t.shape[0]
    return pl.pallas_call(
        functools.partial(_moe_kernel, tm=tm, sub=sub, final_norm=final_norm),
        grid=(m // tm, ne),
        in_specs=[pl.BlockSpec((tm, d), lambda i, e: (i, 0)),
                  pl.BlockSpec((1, d), lambda i, e: (0, 0)),
                  pl.BlockSpec((epad, d), lambda i, e: (0, 0)),
                  pl.BlockSpec((None, d, de), lambda i, e: (e, 0, 0)),
                  pl.BlockSpec((None, d, de), lambda i, e: (e, 0, 0)),
                  pl.BlockSpec((None, de, d), lambda i, e: (e, 0, 0)),
                  pl.BlockSpec((1, d), lambda i, e: (0, 0))],
        out_specs=pl.BlockSpec((tm, d), lambda i, e: (i, 0)),
        out_shape=jax.ShapeDtypeStruct((m, d), F32),
        scratch_shapes=[pltpu.VMEM((tm, d), BF16),
                        pltpu.VMEM((epad, tm), F32), pltpu.VMEM((epad, tm), F32)],
        compiler_params=_cparams(("parallel", "arbitrary")),
        name="moe_ffn",
    )(x, g, w_router_t, wg, wu, wd, g_final)


def _retention_constants(c):
    log_g = np.log1p(-np.exp2(-5.0 - np.arange(RET_HEADS, dtype=np.float32))).astype(np.float32)
    j = np.arange(c, dtype=np.float32)
    rel = j[:, None] - j[None, :]
    w_intra = np.where(rel >= 0, np.exp(rel[None] * log_g[:, None, None]), 0.0).astype(np.float32)
    w_inter = np.exp((j[None, :] + 1.0) * log_g[:, None]).astype(np.float32)
    w_end = np.exp((c - 1.0 - j[None, :]) * log_g[:, None]).astype(np.float32)
    decay = np.exp(c * log_g).astype(np.float32)
    cols = np.zeros((c, LANES), np.float32)
    cols[:, :RET_HEADS] = w_inter.T
    cols[:, RET_HEADS:2 * RET_HEADS] = w_end.T
    return w_intra, cols, [float(d) for d in decay]


def _rotary_tables(pos):
    half = RET_DH // 2
    inv = ROPE_BASE ** (-jnp.arange(half, dtype=F32) / half)
    ang = pos[:, None] * inv[None, :]
    cos, sin = jnp.cos(ang), jnp.sin(ang)
    return jnp.concatenate([cos, cos], axis=-1), jnp.concatenate([-sin, sin], axis=-1)


def _head_expand_matrix():
    e = np.zeros((LANES, SSD_WIDTH), np.float32)
    for h in range(SSD_HEADS):
        e[LANE_DT + h, h * SSD_HEADDIM:(h + 1) * SSD_HEADDIM] = 1.0
    return e


def _softplus_parts(v):
    sp = jnp.log1p(jnp.exp(-jnp.abs(v)))
    return jnp.minimum(v, 0.0) - sp, jnp.maximum(v, 0.0) + sp


def _mixer_prompt_kernel(x_ref, xnext_ref, g_ref, win_ref, wout_ref, bias_ref, alog_ref, convw_ref, convb_ref,
                         mlg_ref, ssdd_ref, ssdg_ref, retg_ref, cos_ref, sin_ref, retw_ref, retcol_ref,
                         expand_ref,
                         xo_ref, c_ref, n_ref, m_ref, conv_ref, ssd_ref, ret_ref,
                         proj_s, cs_s, cst_s, smt_s, dtx_s, acx_s, y_s, u_s, *, tm, tiles_per_seq, ret_decay):
    c = CHUNK
    i = pl.program_id(0)

    @pl.when(i % tiles_per_seq == 0)
    def _init():
        c_ref[...] = jnp.zeros_like(c_ref)
        n_ref[...] = jnp.zeros_like(n_ref)
        m_ref[...] = jnp.zeros_like(m_ref)
        ssd_ref[...] = jnp.zeros_like(ssd_ref)
        ret_ref[...] = jnp.zeros_like(ret_ref)
        u_s[0:8, :] = jnp.zeros((8, SSD_CONV_DIM), F32)

    seg_w = 512
    r_i = lax.broadcasted_iota(jnp.int32, (c, c), 0)
    c_i = lax.broadcasted_iota(jnp.int32, (c, c), 1)
    causal = r_i >= c_i
    tril = jnp.where(causal, 1.0, 0.0).astype(BF16)
    lane1 = lax.broadcasted_iota(jnp.int32, (1, LANES), 1)
    lane_c = lax.broadcasted_iota(jnp.int32, (c, LANES), 1)
    a_vec = -jnp.exp(alog_ref[...])
    expand3 = expand_ref[...]
    bias = bias_ref[...]

    def cumsum_rows(v):
        p = _dot(tril, jnp.concatenate(_split3(v), axis=-1))
        return p[:, :LANES] + p[:, LANES:2 * LANES] + p[:, 2 * LANES:]

    def expand_heads(v):
        return _dot(jnp.concatenate(_split3(v), axis=-1), expand3)

    seg_offs = list(range(0, D_PROJ, seg_w))
    assert seg_offs[-1] == OFF_SMALL

    def prepare(src_ref, slot):
        xn = _rms(src_ref[...], g_ref[...]).astype(BF16)
        for off in seg_offs[-1:] + seg_offs[:-1]:
            wdt = min(seg_w, D_PROJ - off)
            proj_s[slot, :, off:off + wdt] = _dot(xn, win_ref[:, off:off + wdt])
            yield
            if off != OFF_SMALL:
                continue
            is_f = (lane_c >= LANE_F) & (lane_c < LANE_DT)
            is_dt = (lane_c >= LANE_DT) & (lane_c < LANE_DT + SSD_HEADS)
            for ci in range(tm // c):
                rows = slice(ci * c, (ci + 1) * c)
                small = proj_s[slot, rows, OFF_SMALL:OFF_SMALL + LANES] + bias
                logf, dt = _softplus_parts(small)
                cs = cumsum_rows(jnp.where(is_f, logf, jnp.where(is_dt, dt * a_vec, 0.0)))
                cs_s[slot, rows, :] = cs
                cst_s[slot, :, rows] = cs.T
                smt_s[slot, :, rows] = small.T
                dtx_s[slot, rows, :] = expand_heads(jnp.where(is_dt, dt, 0.0))
                acx_s[slot, rows, :] = expand_heads(jnp.where(is_dt, cs, 0.0))
                yield

    @pl.when(i == 0)
    def _first():
        for _ in prepare(x_ref, 0):
            pass

    cur = i % 2
    proj_cur = proj_s.at[cur]
    x = x_ref[...]

    def chunk_body(ci):
        rows = slice(ci * c, (ci + 1) * c)
        small = proj_cur[rows, OFF_SMALL:OFF_SMALL + LANES] + bias
        cs = cs_s[cur, rows, :]
        cs_t = cst_s[cur, :, rows]
        small_t = smt_s[cur, :, rows]

        m_vec = m_ref[...]
        m_out = m_vec

        def ml_front(h):
            q = proj_cur[rows, OFF_QML + h * ML_DH:OFF_QML + (h + 1) * ML_DH]
            k = proj_cur[rows, OFF_KML + h * ML_DH:OFF_KML + (h + 1) * ML_DH] * (ML_DH ** -0.5)
            v = proj_cur[rows, OFF_VML + h * ML_DH:OFF_VML + (h + 1) * ML_DH]
            qb, kb, vb = q.astype(BF16), k.astype(BF16), v.astype(BF16)
            cm = c_ref[h]
            return q, k, v, kb, vb, cm, _dot_nt(qb, kb), _dot_nt(qb, cm.astype(BF16))

        def ml_finish(hs, num_intra, num_inter, den):
            hh = (num_intra + num_inter) / den
            hn = hh * lax.rsqrt(jnp.mean(hh * hh, axis=-1, keepdims=True) + EPS)
            og = proj_cur[rows, OFF_OML + hs.start:OFF_OML + hs.stop]
            y_s[rows, hs] = (hn * mlg_ref[:, hs] * _sigmoid(og)).astype(BF16)

        front = ml_front(0)
        pending = None
        for h in range(ML_HEADS):
            q, k, v, kb, vb, cm, qk, cq = front
            if h + 1 < ML_HEADS:
                front = ml_front(h + 1)
            hs = slice(h * ML_DH, (h + 1) * ML_DH)
            b_col = cs[:, LANE_F + h:LANE_F + h + 1]
            b_row = cs_t[LANE_F + h:LANE_F + h + 1, :]
            i_col = small[:, LANE_I + h:LANE_I + h + 1]
            i_row = small_t[LANE_I + h:LANE_I + h + 1, :]
            m_prev = m_vec[:, h:h + 1]
            lw = jnp.where(causal, b_col + (i_row - b_row), -jnp.inf)
            log_a = b_col + m_prev
            m_t = jnp.maximum(log_a, jnp.max(lw, axis=-1, keepdims=True))
            w_intra = jnp.exp(lw - m_t)
            w_inter = jnp.exp(log_a - m_t)
            s = w_intra * qk
            n_row = n_ref[h:h + 1, :]
            num_intra = _dot(s.astype(BF16), vb)
            den = jnp.sum(s, axis=-1, keepdims=True) + w_inter * jnp.sum(q * n_row, axis=-1, keepdims=True)
            den = jnp.maximum(jnp.abs(den), jnp.exp(-m_t))
            m_new = m_t[c - 1:c, :]
            decay = jnp.exp(log_a[c - 1:c, :] - m_new)
            w_end = jnp.exp(b_col[c - 1:c, :] - b_col + i_col - m_new)
            c_ref[h] = decay * cm + _dot((v * w_end).T.astype(BF16), kb)
            n_ref[h:h + 1, :] = decay * n_row + jnp.sum(k * w_end, axis=0, keepdims=True)
            m_out = jnp.where(lane1 == h, m_new, m_out)
            if pending is not None:
                ml_finish(*pending)
            pending = (hs, num_intra, w_inter * cq, den)
            yield
        ml_finish(*pending)
        m_ref[...] = m_out

        u = proj_cur[rows, OFF_XBC:OFF_XBC + SSD_CONV_DIM]
        u_s[8:8 + c, :] = u
        conv = convb_ref[...] + convw_ref[3:4, :] * u
        for jj in range(CONV_W - 1):
            conv = conv + convw_ref[jj:jj + 1, :] * u_s[5 + jj:5 + jj + c, :]
        u_s[0:8, :] = u_s[c:c + 8, :]
        xc = _silu(conv)
        xs = xc[:, :SSD_WIDTH]
        dt_exp = dtx_s[cur, rows, :]
        acum_exp = acx_s[cur, rows, :]
        xdt = xs * dt_exp
        w_end_exp = jnp.exp(acum_exp[c - 1:c, :] - acum_exp)
        xw = xdt * w_end_exp
        ea_exp = jnp.exp(acum_exp)
        lane_lo = lane_c < SSD_HEADDIM
        hpg = SSD_HEADS // SSD_GROUPS
        y_cols = []
        yield
        for g in range(SSD_GROUPS):
            gw = slice(g * SSD_GROUP_WIDTH, (g + 1) * SSD_GROUP_WIDTH)
            bg = xc[:, SSD_WIDTH + g * SSD_STATE:SSD_WIDTH + (g + 1) * SSD_STATE].astype(BF16)
            cg = xc[:, SSD_WIDTH + (SSD_GROUPS + g) * SSD_STATE:
                    SSD_WIDTH + (SSD_GROUPS + g + 1) * SSD_STATE].astype(BF16)
            cb = _dot_nt(cg, bg)
            st = ssd_ref[gw, :]
            inter = _dot_nt(cg, st.astype(BF16)) * ea_exp[:, gw]
            def decayed_scores(pr):
                out = []
                for hidx in (g * hpg + 2 * pr, g * hpg + 2 * pr + 1):
                    a_col = cs[:, LANE_DT + hidx:LANE_DT + hidx + 1]
                    a_row = cs_t[LANE_DT + hidx:LANE_DT + hidx + 1, :]
                    dec = jnp.exp(jnp.where(causal, a_col - a_row, -jnp.inf))
                    out.append((dec * cb).astype(BF16))
                return out

            att = decayed_scores(0)
            for pr in range(hpg // 2):
                h0 = g * hpg + 2 * pr
                xp = xdt[:, h0 * SSD_HEADDIM:(h0 + 2) * SSD_HEADDIM].astype(BF16)
                pair = [_dot(att[0], xp), _dot(att[1], xp)]
                if pr + 1 < hpg // 2:
                    att = decayed_scores(pr + 1)
                y_cols.append(jnp.where(lane_lo, pair[0], pair[1]) + inter[:, pr * LANES:(pr + 1) * LANES])
                yield
            upd = _dot(xw[:, gw].T.astype(BF16), bg)
            for r in range(hpg):
                hidx = g * hpg + r
                scal = jnp.exp(cs[c - 1:c, LANE_DT + hidx:LANE_DT + hidx + 1])
                hr = slice(hidx * SSD_HEADDIM, (hidx + 1) * SSD_HEADDIM)
                ssd_ref[hr, :] = scal * ssd_ref[hr, :] + upd[r * SSD_HEADDIM:(r + 1) * SSD_HEADDIM, :]
            yield
        ys = jnp.concatenate(y_cols, axis=-1) + ssdd_ref[...] * xs
        yz = ys * _silu(proj_cur[rows, OFF_Z:OFF_Z + SSD_WIDTH])
        for g in range(SSD_GROUPS):
            gw = slice(g * SSD_GROUP_WIDTH, (g + 1) * SSD_GROUP_WIDTH)
            seg = yz[:, gw]
            seg = seg * lax.rsqrt(jnp.mean(seg * seg, axis=-1, keepdims=True) + EPS) * ssdg_ref[:, gw]
            y_s[rows, ML_WIDTH + g * SSD_GROUP_WIDTH:ML_WIDTH + (g + 1) * SSD_GROUP_WIDTH] = seg.astype(BF16)
        yield

        cosb = cos_ref[rows, :]
        sinb = sin_ref[rows, :]
        retcol = retcol_ref[...]
        def ret_front(h):
            q = proj_cur[rows, OFF_QR + h * RET_DH:OFF_QR + (h + 1) * RET_DH]
            k = proj_cur[rows, OFF_KR + h * RET_DH:OFF_KR + (h + 1) * RET_DH]
            v = proj_cur[rows, OFF_VR + h * RET_DH:OFF_VR + (h + 1) * RET_DH]
            qr = q * cosb + pltpu.roll(q, RET_DH // 2, 1) * sinb
            kr = (k * cosb + pltpu.roll(k, RET_DH // 2, 1) * sinb) * (RET_DH ** -0.5)
            qb, kb, vb = qr.astype(BF16), kr.astype(BF16), v.astype(BF16)
            sm = ret_ref[h]
            return kr, vb, sm, _dot_nt(qb, kb), _dot(qb, sm.astype(BF16))

        front = ret_front(0)
        for h in range(RET_HEADS):
            kr, vb, sm, qk, qs = front
            if h + 1 < RET_HEADS:
                front = ret_front(h + 1)
            hs = slice(h * RET_DH, (h + 1) * RET_DH)
            gr = proj_cur[rows, OFF_GR + h * RET_DH:OFF_GR + (h + 1) * RET_DH]
            att = qk * retw_ref[h]
            o = _dot(att.astype(BF16), vb) + qs * retcol[:, h:h + 1]
            kw = kr * retcol[:, RET_HEADS + h:RET_HEADS + h + 1]
            ret_ref[h] = ret_decay[h] * sm + _dot(kw.T.astype(BF16), vb)
            on = o * lax.rsqrt(jnp.mean(o * o, axis=-1, keepdims=True) + EPS)
            y_s[rows, ML_WIDTH + SSD_WIDTH + h * RET_DH:ML_WIDTH + SSD_WIDTH + (h + 1) * RET_DH] = (
                on * retg_ref[:, hs] * _silu(gr)).astype(BF16)
            yield

    segments = prepare(xnext_ref, (i + 1) % 2)
    for _ in itertools.chain.from_iterable(chunk_body(ci) for ci in range(tm // c)):
        next(segments, None)
    for _ in segments:
        pass
    xo_ref[...] = x + _dot(y_s[...], wout_ref[...])
    conv_ref[...] = u_s[5:8, :]


def mixer_prompt(x, lw, consts, *, tm):
    b, length, d = x.shape
    c = CHUNK
    tps = length // tm
    n_tiles = b * tps
    x2d = x.reshape(b * length, d)
    row = lambda w: _const_spec((1, w))
    state_spec = lambda *shape: pl.BlockSpec((None,) + shape, lambda i: (i // tps,) + (0,) * len(shape))
    out_shapes = [jax.ShapeDtypeStruct(x2d.shape, F32),
                  jax.ShapeDtypeStruct((b, ML_HEADS, ML_DH, ML_DH), F32),
                  jax.ShapeDtypeStruct((b, ML_HEADS, ML_DH), F32),
                  jax.ShapeDtypeStruct((b, 1, LANES), F32),
                  jax.ShapeDtypeStruct((b, CONV_W - 1, SSD_CONV_DIM), F32),
                  jax.ShapeDtypeStruct((b, SSD_WIDTH, SSD_STATE), F32),
                  jax.ShapeDtypeStruct((b, RET_HEADS, RET_DH, RET_DH), F32)]
    outs = pl.pallas_call(
        functools.partial(_mixer_prompt_kernel, tm=tm, tiles_per_seq=tps, ret_decay=consts["ret_decay"]),
        grid=(n_tiles,),
        in_specs=[pl.BlockSpec((tm, d), lambda i: (i, 0)),
                  pl.BlockSpec((tm, d), lambda i: (jnp.minimum(i + 1, n_tiles - 1), 0)),
                  row(d), _const_spec((d, D_PROJ)), _const_spec((D_MIX, d)),
                  row(LANES), row(LANES), _const_spec((CONV_W, SSD_CONV_DIM)), row(SSD_CONV_DIM),
                  row(ML_WIDTH), row(SSD_WIDTH), row(SSD_WIDTH), row(RET_WIDTH),
                  pl.BlockSpec((tm, LANES), lambda i: (i % tps, 0)),
                  pl.BlockSpec((tm, LANES), lambda i: (i % tps, 0)),
                  _const_spec((RET_HEADS, c, c)), _const_spec((c, LANES)),
                  _const_spec((3 * LANES, SSD_WIDTH))],
        out_specs=[pl.BlockSpec((tm, d), lambda i: (i, 0)),
                   state_spec(ML_HEADS, ML_DH, ML_DH), state_spec(ML_HEADS, ML_DH), state_spec(1, LANES),
                   state_spec(CONV_W - 1, SSD_CONV_DIM), state_spec(SSD_WIDTH, SSD_STATE),
                   state_spec(RET_HEADS, RET_DH, RET_DH)],
        out_shape=out_shapes,
        scratch_shapes=[pltpu.VMEM((2, tm, D_PROJ), F32),
                        pltpu.VMEM((2, tm, LANES), F32), pltpu.VMEM((2, LANES, tm), F32),
                        pltpu.VMEM((2, LANES, tm), F32), pltpu.VMEM((2, tm, SSD_WIDTH), F32),
                        pltpu.VMEM((2, tm, SSD_WIDTH), F32), pltpu.VMEM((tm, D_MIX), BF16),
                        pltpu.VMEM((c + 8, SSD_CONV_DIM), F32)],
        compiler_params=_cparams(("arbitrary",)),
        name="mixer_prompt",
    )(x2d, x2d, lw["norm_mix"], lw["w_in"], lw["w_out"], lw["bias"], lw["alog"], lw["conv_w"], lw["conv_b"],
      lw["ml_norm"], lw["ssd_d"], lw["ssd_norm"], lw["ret_norm"], consts["cos_p"], consts["sin_p"],
      consts["ret_w"], consts["ret_cols"], consts["expand3"])
    return [outs[0].reshape(b, length, d)] + list(outs[1:])


N_SAMPLE_STATES = 6


def _mixer_sample_kernel(*refs, bt, ret_gamma, n_alias):
    (proj_ref, small_ref, bias_ref, alog_ref, convw_ref, convb_ref, mlg_ref, ssdd_ref,
     ssdg_ref, retg_ref, cos_ref, sin_ref, expand_ref,
     c_in, n_in, m_in, conv_in, ssd_in, ret_in) = refs[:19]
    (y_ref, c_out, n_out, m_out, conv_out, ssd_out, ret_out,
     wi_s, wt_s, en_s, dtx_s, eax_s) = refs[19 + n_alias:]
    small = small_ref[...] + bias_ref[...]
    logf, dt = _softplus_parts(small)
    lane_b = lax.broadcasted_iota(jnp.int32, (bt, LANES), 1)
    is_dt = (lane_b >= LANE_DT) & (lane_b < LANE_DT + SSD_HEADS)
    log_a = pltpu.roll(logf, LANES - LANE_F, 1) + m_in[...]
    m_t = jnp.maximum(log_a, small)
    m_out[...] = m_t
    wi_s[...] = jnp.exp(small - m_t)
    wt_s[...] = jnp.exp(log_a - m_t)
    en_s[...] = jnp.exp(-m_t)
    a_vec = -jnp.exp(alog_ref[...])
    expand = expand_ref[...]
    dtx_s[...] = _dot_exact(jnp.where(is_dt, dt, 0.0), expand)
    eax_s[...] = jnp.exp(_dot_exact(jnp.where(is_dt, dt * a_vec, 0.0), expand))

    sub16 = lax.broadcasted_iota(jnp.int32, (BF16_ROWS, LANES), 0)
    row_hi = lax.broadcasted_iota(jnp.int32, (LANES, LANES), 0) < SSD_HEADDIM

    def rows3(a, b, c):
        return jnp.where(sub16 == 0, a, jnp.where(sub16 == 1, b, jnp.where(sub16 == 2, c, 0.0))).astype(BF16)

    def split(v):
        hi = v.astype(BF16).astype(F32)
        return hi, v - hi

    def outer(a, b):
        a_hi, a_lo = split(a)
        b_hi, b_lo = split(b)
        return lax.dot_general(rows3(a_hi, a_hi, a_lo), rows3(b_hi, b_lo, b_hi), (((0,), (0,)), ((), ())),
                               preferred_element_type=F32)

    def row1(a):
        return jnp.where(sub16 == 0, a, 0.0).astype(BF16)

    cosb = cos_ref[...]
    sinb = sin_ref[...]
    hpg = SSD_HEADS // SSD_GROUPS

    sub_l = lax.broadcasted_iota(jnp.int32, (bt, LANES), 0)
    sub_w = lax.broadcasted_iota(jnp.int32, (bt, SSD_WIDTH), 0)

    def body(j):
        pick_l = lambda ref: jnp.sum(jnp.where(sub_l == j, ref[...], 0.0), axis=0, keepdims=True)
        pick_w = lambda ref: jnp.sum(jnp.where(sub_w == j, ref[...], 0.0), axis=0, keepdims=True)
        wi = pick_l(wi_s)
        wt = pick_l(wt_s)
        en = pick_l(en_s)
        def ml_front(h):
            q = proj_ref[j, :,OFF_QML + h * ML_DH:OFF_QML + (h + 1) * ML_DH]
            k = proj_ref[j, :,OFF_KML + h * ML_DH:OFF_KML + (h + 1) * ML_DH] * (ML_DH ** -0.5)
            v = proj_ref[j, :,OFF_VML + h * ML_DH:OFF_VML + (h + 1) * ML_DH]
            return q, k, outer(v, k)

        front = ml_front(0)
        for h in range(ML_HEADS):
            q, k, vk = front
            if h + 1 < ML_HEADS:
                front = ml_front(h + 1)
            hs = slice(h * ML_DH, (h + 1) * ML_DH)
            og = proj_ref[j, :,OFF_OML + h * ML_DH:OFF_OML + (h + 1) * ML_DH]
            w_in_h = wi[:, h:h + 1]
            w_tr_h = wt[:, h:h + 1]
            c_new = w_tr_h * c_in[j, h] + w_in_h * vk
            n_new = w_tr_h * n_in[j, h:h + 1, :] + w_in_h * k
            c_out[j, h] = c_new
            n_out[j, h:h + 1, :] = n_new
            num = _dot_nt(row1(q), c_new.astype(BF16))[0:1, :]
            den = jnp.sum(n_new * q, axis=-1, keepdims=True)
            den = jnp.maximum(jnp.abs(den), en[:, h:h + 1])
            hh = num / den
            hn = hh * lax.rsqrt(jnp.mean(hh * hh, axis=-1, keepdims=True) + EPS)
            y_ref[j, :,hs] = hn * mlg_ref[:, hs] * _sigmoid(og)
            yield
        u = proj_ref[j, :,OFF_XBC:OFF_XBC + SSD_CONV_DIM]
        prev = conv_in[j]
        conv = convb_ref[...] + convw_ref[CONV_W - 1:CONV_W, :] * u
        for jj in range(CONV_W - 1):
            conv = conv + convw_ref[jj:jj + 1, :] * prev[jj:jj + 1, :]
        conv_out[j, 0:CONV_W - 2, :] = prev[1:CONV_W - 1, :]
        conv_out[j, CONV_W - 2:CONV_W - 1, :] = u
        xc = _silu(conv)
        xs = xc[:, :SSD_WIDTH]
        xdt = xs * pick_w(dtx_s)
        ea = pick_w(eax_s)
        y_cols = []
        n_pairs = SSD_HEADS // 2

        def ssd_front(p):
            g = p // (hpg // 2)
            bg = xc[:, SSD_WIDTH + g * SSD_STATE:SSD_WIDTH + (g + 1) * SSD_STATE]
            pw = slice(2 * p * SSD_HEADDIM, (2 * p + 2) * SSD_HEADDIM)
            return pw, outer(xdt[:, pw], bg)

        front = ssd_front(0)
        for p in range(n_pairs):
            pw, xb = front
            if p + 1 < n_pairs:
                front = ssd_front(p + 1)
            g = p // (hpg // 2)
            cg = xc[:, SSD_WIDTH + (SSD_GROUPS + g) * SSD_STATE:SSD_WIDTH + (SSD_GROUPS + g + 1) * SSD_STATE]
            lo = pw.start
            decay = jnp.where(row_hi, ea[:, lo:lo + 1], ea[:, lo + SSD_HEADDIM:lo + SSD_HEADDIM + 1])
            st_new = decay * ssd_in[j, pw, :] + xb
            ssd_out[j, pw, :] = st_new
            y_cols.append(_dot_nt(row1(cg), st_new.astype(BF16))[0:1, :])
            yield
        ys = jnp.concatenate(y_cols, axis=-1) + ssdd_ref[...] * xs
        yz = ys * _silu(proj_ref[j, :,OFF_Z:OFF_Z + SSD_WIDTH])
        for g in range(SSD_GROUPS):
            gw = slice(g * SSD_GROUP_WIDTH, (g + 1) * SSD_GROUP_WIDTH)
            seg = yz[:, gw]
            y_ref[j, :,ML_WIDTH + g * SSD_GROUP_WIDTH:ML_WIDTH + (g + 1) * SSD_GROUP_WIDTH] = (
                seg * lax.rsqrt(jnp.mean(seg * seg, axis=-1, keepdims=True) + EPS) * ssdg_ref[:, gw])
        def ret_front(h):
            q = proj_ref[j, :,OFF_QR + h * RET_DH:OFF_QR + (h + 1) * RET_DH]
            k = proj_ref[j, :,OFF_KR + h * RET_DH:OFF_KR + (h + 1) * RET_DH]
            v = proj_ref[j, :,OFF_VR + h * RET_DH:OFF_VR + (h + 1) * RET_DH]
            qr = q * cosb + pltpu.roll(q, RET_DH // 2, 1) * sinb
            kr = (k * cosb + pltpu.roll(k, RET_DH // 2, 1) * sinb) * (RET_DH ** -0.5)
            return qr, outer(kr, v)

        front = ret_front(0)
        for h in range(RET_HEADS):
            qr, kv = front
            if h + 1 < RET_HEADS:
                front = ret_front(h + 1)
            hs = slice(h * RET_DH, (h + 1) * RET_DH)
            gr = proj_ref[j, :,OFF_GR + h * RET_DH:OFF_GR + (h + 1) * RET_DH]
            s_new = ret_gamma[h] * ret_in[j, h] + kv
            ret_out[j, h] = s_new
            o = _dot(row1(qr), s_new.astype(BF16))[0:1, :]
            on = o * lax.rsqrt(jnp.mean(o * o, axis=-1, keepdims=True) + EPS)
            y_ref[j, :,ML_WIDTH + SSD_WIDTH + h * RET_DH:ML_WIDTH + SSD_WIDTH + (h + 1) * RET_DH] = (
                on * retg_ref[:, hs] * _silu(gr))
            yield

    def body_pair(jj, carry):
        for _ in itertools.zip_longest(body(2 * jj), body(2 * jj + 1)):
            pass
        return carry

    lax.fori_loop(0, bt // 2, body_pair, 0)


def mixer_sample(proj, lw, consts, states, layer, prev_out=None, *, bt=8):
    b = proj.shape[0]
    row = lambda w: pl.BlockSpec((1, w), lambda i: (0, 0))
    blk = lambda *shape: pl.BlockSpec((bt,) + shape, lambda i: (i,) + (0,) * len(shape))
    lblk = lambda *shape: pl.BlockSpec((None, bt) + shape, lambda i: (layer, i) + (0,) * len(shape))
    state_specs = [lblk(ML_HEADS, ML_DH, ML_DH), lblk(ML_HEADS, ML_DH), lblk(LANES),
                   lblk(CONV_W - 1, SSD_CONV_DIM), lblk(SSD_WIDTH, SSD_STATE), lblk(RET_HEADS, RET_DH, RET_DH)]
    state_shapes = [jax.ShapeDtypeStruct(s.shape, F32) for s in states]
    proj3 = proj.reshape(b, 1, D_PROJ)
    small = proj[:, OFF_SMALL:OFF_SMALL + LANES]
    n_fixed = 13 + N_SAMPLE_STATES
    alias_args = list(prev_out) if prev_out is not None else []
    aliases = {n_fixed + k: 1 + k for k in range(len(alias_args))}
    outs = pl.pallas_call(
        functools.partial(_mixer_sample_kernel, bt=bt, ret_gamma=consts["ret_gamma"], n_alias=len(alias_args)),
        grid=(b // bt,),
        in_specs=[blk(1, D_PROJ), blk(LANES), row(LANES), row(LANES),
                  pl.BlockSpec((CONV_W, SSD_CONV_DIM), lambda i: (0, 0)), row(SSD_CONV_DIM),
                  row(ML_WIDTH), row(SSD_WIDTH), row(SSD_WIDTH), row(RET_WIDTH), row(LANES), row(LANES),
                  pl.BlockSpec((LANES, SSD_WIDTH), lambda i: (0, 0))] + state_specs
                 + [pl.BlockSpec(memory_space=pl.ANY)] * len(alias_args),
        out_specs=[blk(1, D_MIX)] + state_specs,
        out_shape=[jax.ShapeDtypeStruct((b, 1, D_MIX), F32)] + state_shapes,
        input_output_aliases=aliases,
        scratch_shapes=[pltpu.VMEM((bt, LANES), F32)] * 3 + [pltpu.VMEM((bt, SSD_WIDTH), F32)] * 2,
        compiler_params=_cparams(("parallel",)),
        name="mixer_sample",
    )(proj3, small, lw["bias"], lw["alog"], lw["conv_w"], lw["conv_b"], lw["ml_norm"], lw["ssd_d"],
      lw["ssd_norm"], lw["ret_norm"], consts["cos_s"], consts["sin_s"], consts["expand"],
      *states, *alias_args)
    return outs[0].reshape(b, D_MIX), list(outs[1:])


def _pack_in_proj(w):
    sizes = [ML_WIDTH] * 4 + [ML_HEADS, ML_HEADS, SSD_WIDTH, SSD_CONV_DIM, SSD_HEADS] + [RET_WIDTH] * 4
    offs = np.concatenate([[0], np.cumsum(sizes)])
    seg = lambda i: w[:, int(offs[i]):int(offs[i + 1])]
    small = jnp.concatenate([seg(4), seg(5), seg(8)], axis=1)
    small = jnp.pad(small, ((0, 0), (0, LANES - small.shape[1])))
    cols = [seg(0), seg(1), seg(2), seg(3), seg(6), seg(7), seg(9), seg(10), seg(11), seg(12), small]
    return jnp.concatenate(cols, axis=1).astype(BF16)


def _lane_row(parts):
    v = jnp.concatenate([p.astype(F32) for p in parts])
    return jnp.pad(v, (0, LANES - v.shape[0]))[None, :]


def kernel(x_prompt, x_sample, state_mlstm_c, state_mlstm_n, state_mlstm_m, state_ssd_conv, state_ssd, state_ret,
           cache_mem_k, cache_mem_v, mem_prompt,
           norm_mix, w_in, ml_i_bias, ml_f_bias, ml_norm, ssd_conv_w, ssd_conv_b, ssd_dt_bias, ssd_a_log, ssd_d,
           ssd_norm, ret_norm, w_out, norm_ca, norm_mem, w_ca_q, w_ca_k, w_ca_v, w_ca_o, norm_ffn,
           ffn_w_gate, ffn_w_up, ffn_w_down, moe_w_router, moe_w_gate, moe_w_up, moe_w_down, norm_final):
    bp, seq, d = x_prompt.shape
    bs = x_sample.shape[0]
    assert x_sample.shape[1] == 1 and seq % CHUNK == 0 and DEPTH % 2 == 0

    ret_w, ret_cols, ret_decay = _retention_constants(CHUNK)
    _, _, ret_gamma = _retention_constants(1)
    cos_p, sin_p = _rotary_tables(jnp.arange(seq, dtype=F32))
    cos_s, sin_s = _rotary_tables(PAST_LEN + jnp.arange(1, dtype=F32))
    consts = {"ret_w": jnp.asarray(ret_w), "ret_cols": jnp.asarray(ret_cols), "ret_decay": ret_decay,
              "ret_gamma": ret_gamma, "cos_p": cos_p, "sin_p": sin_p, "cos_s": cos_s, "sin_s": sin_s,
              "expand": jnp.asarray(_head_expand_matrix()),
              "expand3": jnp.asarray(np.tile(_head_expand_matrix(), (3, 1))).astype(BF16)}
    zeros12 = jnp.zeros((2 * ML_HEADS,), F32)
    ones_row = jnp.ones((1, D_MIX), F32)

    mem2d = mem_prompt.reshape(bp * MEM_LEN, d)
    mem_k_p, mem_v_p, mem_k_out, mem_v_out = kv_proj(mem2d, norm_mem[:, None, :], w_ca_k.astype(BF16),
                                                     w_ca_v.astype(BF16))

    mem_k_p = mem_k_p.reshape(DEPTH, bp, MEM_LEN, d)
    mem_v_p = mem_v_p.reshape(DEPTH, bp, MEM_LEN, d)
    sample_states = (state_mlstm_c, state_mlstm_n,
                     jnp.pad(state_mlstm_m, ((0, 0), (0, 0), (0, LANES - ML_HEADS))), state_ssd_conv,
                     state_ssd.reshape(DEPTH, bs, SSD_WIDTH, SSD_STATE), state_ret)
    st_s = None

    xp = x_prompt
    xs = x_sample.reshape(bs, d)
    new_p = [[] for _ in range(6)]
    for l in range(DEPTH):
        lw = {"norm_mix": norm_mix[l][None, :], "w_in": _pack_in_proj(w_in[l]), "w_out": w_out[l].astype(BF16),
              "bias": _lane_row([ml_i_bias[l], ml_f_bias[l], ssd_dt_bias[l]]),
              "alog": _lane_row([zeros12, ssd_a_log[l]]),
              "conv_w": ssd_conv_w[l], "conv_b": ssd_conv_b[l][None, :], "ml_norm": ml_norm[l][None, :],
              "ssd_d": jnp.repeat(ssd_d[l], SSD_HEADDIM)[None, :], "ssd_norm": ssd_norm[l][None, :],
              "ret_norm": ret_norm[l][None, :]}
        wq, wo = w_ca_q[l].astype(BF16), w_ca_o[l].astype(BF16)
        g_ca, g_ffn = norm_ca[l][None, :], norm_ffn[l][None, :]
        last = l == DEPTH - 1
        j = l // 2
        if l % 2 == 0:
            ffn_w = (ffn_w_gate[j].astype(BF16), ffn_w_up[j].astype(BF16), ffn_w_down[j].astype(BF16))
        else:
            wrt = jnp.pad(moe_w_router[j].T, ((0, 2 * N_EXPERTS - N_EXPERTS), (0, 0)))
            ffn_w = (wrt, moe_w_gate[j].astype(BF16), moe_w_up[j].astype(BF16), moe_w_down[j].astype(BF16))

        def run_ffn(x2d, tm):
            if l % 2 == 0:
                return ffn_dense(x2d, g_ffn, *ffn_w, tm=tm)
            return moe_ffn(x2d, g_ffn, *ffn_w, norm_final[None, :], tm=tm, sub=_moe_sub_rows(tm), final_norm=last)

        xp, c1, n1, m1, conv1, ssd1, ret1 = mixer_prompt(xp, lw, consts, tm=256)
        st_p = (c1, n1, m1[:, 0, :ML_HEADS], conv1,
                ssd1.reshape(bp, SSD_HEADS, SSD_HEADDIM, SSD_STATE), ret1)
        xp = ca_prompt(xp, g_ca, wq, wo, mem_k_p, mem_v_p, l)
        xp = run_ffn(xp.reshape(bp * seq, d), 1024 if l % 2 else 512).reshape(bp, seq, d)

        proj = norm_matmul(xs, lw["norm_mix"], lw["w_in"], tn=D_PROJ // 3)
        y, st_s = mixer_sample(proj, lw, consts, sample_states, l, st_s)
        xs = norm_matmul(y, ones_row, lw["w_out"], residual=xs, norm=False)
        q = norm_matmul(xs, g_ca, wq)
        o = ca_sample(q.reshape(bs, CA_HEADS, CA_DH), cache_mem_k, cache_mem_v, l).reshape(bs, d)
        xs = norm_matmul(o, ones_row[:, :d], wo, residual=xs, norm=False)
        xs = run_ffn(xs, bs)

        for lst, a in zip(new_p, st_p):
            lst.append(a)

    shape5 = (DEPTH, bp, MEM_LEN, CA_HEADS, CA_DH)
    c_s, n_s, m_s, conv_s, ssd_s, ret_s = st_s
    return (xp, xs.reshape(bs, 1, d),
            jnp.stack(new_p[0]), jnp.stack(new_p[1]), jnp.stack(new_p[2]), jnp.stack(new_p[3]),
            jnp.stack(new_p[4]), jnp.stack(new_p[5]), mem_k_out.reshape(shape5), mem_v_out.reshape(shape5),
            c_s, n_s, m_s[:, :, :ML_HEADS], conv_s,
            ssd_s.reshape(DEPTH, bs, SSD_HEADS, SSD_HEADDIM, SSD_STATE), ret_s)
```
